```python
import math
import jax, jax.numpy as jnp
from jax import lax
import numpy as np

D_MODEL = 1024
BATCH = 8
SEQ = 4096
DEPTH = 1

CHUNK = 64
QBLK = 128
MIX_WIDTH = D_MODEL
DIFF_QK_DIM = 64
DIFF_V_DIM = 2 * DIFF_QK_DIM
DIFF_HEADS = (MIX_WIDTH // 2) // DIFF_V_DIM
SB_DIM = 64
SB_HEADS = (MIX_WIDTH // 2) // SB_DIM
NUM_BUCKETS = 32
MAX_DISTANCE = 128
D_FF = ((8 * D_MODEL + 3 * 256 - 1) // (3 * 256)) * 256
EPS = 1e-6

DIFF_Q_COLS = DIFF_HEADS * 2 * DIFF_QK_DIM
DIFF_V_COLS = DIFF_HEADS * DIFF_V_DIM
SB_COLS = SB_HEADS * SB_DIM
IN_COLS = 2 * DIFF_Q_COLS + DIFF_V_COLS + 3 * SB_COLS
SPLIT_POINTS = (DIFF_Q_COLS, 2 * DIFF_Q_COLS, 2 * DIFF_Q_COLS + DIFF_V_COLS,
                2 * DIFF_Q_COLS + DIFF_V_COLS + SB_COLS,
                2 * DIFF_Q_COLS + DIFF_V_COLS + 2 * SB_COLS)

kernel_name = "hymba_diffattn_stickbreaking_block"


def rms_norm(x, w):
    xf = x.astype(jnp.float32)
    y = xf * lax.rsqrt(jnp.mean(xf * xf, axis=-1, keepdims=True) + EPS)
    return (y * w.astype(jnp.float32)).astype(x.dtype)


def t5_bucket(rel):
    nb = NUM_BUCKETS // 2
    max_exact = nb // 2
    ret = (rel > 0).astype(jnp.int32) * nb
    n = jnp.abs(rel)
    nf = jnp.maximum(n, 1).astype(jnp.float32)
    large = max_exact + (jnp.log(nf / max_exact) / math.log(MAX_DISTANCE / max_exact)
                         * (nb - max_exact)).astype(jnp.int32)
    large = jnp.minimum(large, nb - 1)
    return ret + jnp.where(n < max_exact, n, large)


def to_blocks(t):
    b, s = t.shape[0], t.shape[1]
    return jnp.moveaxis(t.reshape((b, s // QBLK, QBLK) + t.shape[2:]), 1, 0)


def from_blocks(t):
    t = jnp.moveaxis(t, 0, 1)
    return t.reshape((t.shape[0], t.shape[1] * t.shape[2]) + t.shape[3:])


def diff_attention(q, k, v, lam, rel_bias):
    s = q.shape[1]
    scale = DIFF_QK_DIM ** -0.5
    k_pos = jnp.arange(s, dtype=jnp.int32)
    k_chunk = k_pos // CHUNK

    def one_block(args):
        qb, blk = args
        q_pos = blk * QBLK + jnp.arange(QBLK, dtype=jnp.int32)
        logits = jnp.einsum('bqhcd,bkhcd->bhcqk', qb, k).astype(jnp.float32) * scale
        bias = rel_bias.astype(jnp.float32)[t5_bucket(k_pos[None, :] - q_pos[:, None])]
        logits = logits + jnp.transpose(bias, (2, 0, 1))[None, :, None]
        mask = k_chunk[None, :] <= (q_pos // CHUNK)[:, None]
        logits = jnp.where(mask[None, None, None], logits, -jnp.inf)
        p = jax.nn.softmax(logits, axis=-1)
        attn = p[:, :, 0] - lam * p[:, :, 1]
        return jnp.einsum('bhqk,bkhd->bqhd', attn.astype(v.dtype), v)

    nblk = s // QBLK
    out = lax.map(one_block, (to_blocks(q), jnp.arange(nblk, dtype=jnp.int32)))
    return from_blocks(out)


def stick_breaking_attention(q, k, v):
    s = q.shape[1]
    scale = SB_DIM ** -0.5
    k_pos = jnp.arange(s, dtype=jnp.int32)

    def one_block(args):
        qb, blk = args
        q_pos = blk * QBLK + jnp.arange(QBLK, dtype=jnp.int32)
        z = jnp.einsum('bqhd,bkhd->bhqk', qb, k).astype(jnp.float32) * scale
        mask = (k_pos[None, :] < q_pos[:, None])[None, None]
        log_1mb = jnp.where(mask, jax.nn.log_sigmoid(-z), 0.0)
        rem = lax.cumsum(log_1mb, axis=3, reverse=True) - log_1mb
        a = jnp.where(mask, jnp.exp(jax.nn.log_sigmoid(z) + rem), 0.0)
        return jnp.einsum('bhqk,bkhd->bqhd', a.astype(v.dtype), v)

    nblk = s // QBLK
    out = lax.map(one_block, (to_blocks(q), jnp.arange(nblk, dtype=jnp.int32)))
    return from_blocks(out)


def setup_inputs(seed: int = 0) -> dict:
    key = jax.random.key(seed)
    ks = jax.random.split(key, 20)
    f32 = jnp.float32

    def gain(k, shape):
        return 1.0 + 0.05 * jax.random.normal(k, shape, f32)

    return {
        "x": jax.random.normal(ks[0], (BATCH, SEQ, D_MODEL), f32),
        "norm1_w": gain(ks[1], (DEPTH, D_MODEL)),
        "w_in": jax.random.normal(ks[2], (DEPTH, D_MODEL, IN_COLS), f32) * D_MODEL ** -0.5,
        "q_norm_w": gain(ks[3], (DEPTH, DIFF_QK_DIM)),
        "k_norm_w": gain(ks[4], (DEPTH, DIFF_QK_DIM)),
        "lambda_q1": 0.1 * jax.random.normal(ks[5], (DEPTH, DIFF_QK_DIM), f32),
        "lambda_k1": 0.1 * jax.random.normal(ks[6], (DEPTH, DIFF_QK_DIM), f32),
        "lambda_q2": 0.1 * jax.random.normal(ks[7], (DEPTH, DIFF_QK_DIM), f32),
        "lambda_k2": 0.1 * jax.random.normal(ks[8], (DEPTH, DIFF_QK_DIM), f32),
        "diff_out_norm_w": gain(ks[9], (DEPTH, DIFF_V_DIM)),
        "sb_out_norm_w": gain(ks[10], (DEPTH, SB_DIM)),
        "w_out": jax.random.normal(ks[11], (DEPTH, MIX_WIDTH, D_MODEL), f32) * MIX_WIDTH ** -0.5,
        "norm2_w": gain(ks[12], (DEPTH, D_MODEL)),
        "w_gate": jax.random.normal(ks[13], (DEPTH, D_MODEL, D_FF), f32) * D_MODEL ** -0.5,
        "w_up": jax.random.normal(ks[14], (DEPTH, D_MODEL, D_FF), f32) * D_MODEL ** -0.5,
        "w_down": jax.random.normal(ks[15], (DEPTH, D_FF, D_MODEL), f32) * D_FF ** -0.5,
        "rel_bias": 0.5 * jax.random.normal(ks[16], (NUM_BUCKETS, DIFF_HEADS), f32),
    }


def reference(x, norm1_w, w_in, q_norm_w, k_norm_w, lambda_q1, lambda_k1, lambda_q2,
              lambda_k2, diff_out_norm_w, sb_out_norm_w, w_out, norm2_w, w_gate, w_up,
              w_down, rel_bias):
    b, s, _ = x.shape
    h = x
    for l in range(DEPTH):
        lambda_init = 0.8 - 0.6 * math.exp(-0.3 * l)
        u = rms_norm(h, norm1_w[l])
        proj = jnp.einsum('bsd,de->bse', u, w_in[l])
        dq, dk, dv, sq, sk, sv = jnp.split(proj, SPLIT_POINTS, axis=-1)

        dq = rms_norm(dq.reshape(b, s, DIFF_HEADS, 2, DIFF_QK_DIM), q_norm_w[l])
        dk = rms_norm(dk.reshape(b, s, DIFF_HEADS, 2, DIFF_QK_DIM), k_norm_w[l])
        dv = dv.reshape(b, s, DIFF_HEADS, DIFF_V_DIM)
        lam = (jnp.exp(jnp.sum(lambda_q1[l].astype(jnp.float32) * lambda_k1[l].astype(jnp.float32)))
               - jnp.exp(jnp.sum(lambda_q2[l].astype(jnp.float32) * lambda_k2[l].astype(jnp.float32)))
               + lambda_init)
        y_diff = diff_attention(dq, dk, dv, lam, rel_bias)
        y_diff = rms_norm(y_diff, diff_out_norm_w[l]) * (1.0 - lambda_init)

        sq = sq.reshape(b, s, SB_HEADS, SB_DIM)
        sk = sk.reshape(b, s, SB_HEADS, SB_DIM)
        sv = sv.reshape(b, s, SB_HEADS, SB_DIM)
        y_sb = rms_norm(stick_breaking_attention(sq, sk, sv), sb_out_norm_w[l])

        mix = jnp.concatenate([y_diff.reshape(b, s, DIFF_HEADS * DIFF_V_DIM),
                               y_sb.reshape(b, s, SB_HEADS * SB_DIM)], axis=-1)
        h = h + jnp.einsum('bse,ed->bsd', mix, w_out[l])

        u2 = rms_norm(h, norm2_w[l])
        gate = jnp.einsum('bsd,df->bsf', u2, w_gate[l])
        up = jnp.einsum('bsd,df->bsf', u2, w_up[l])
        h = h + jnp.einsum('bsf,fd->bsd', jax.nn.silu(gate) * up, w_down[l])
    return h
```

```python
import math
from functools import partial

import jax
import jax.numpy as jnp
from jax import lax
from jax.experimental import pallas as pl
from jax.experimental.pallas import tpu as pltpu

D_MODEL = 1024
CHUNK = 64
DIFF_QK_DIM = 64
DIFF_V_DIM = 128
DIFF_HEADS = 4
SB_DIM = 64
SB_HEADS = 8
NUM_BUCKETS = 32
MAX_DISTANCE = 128
D_FF = 2816
EPS = 1e-6
IN_COLS = 3072
LAMBDA_INIT = 0.8 - 0.6 * math.exp(-0.3 * 0)
LANES = 128

QD_BLK, KD_BLK, VD_BLK = 0, 4, 8
QS_BLK, KS_BLK, VS_BLK = 12, 16, 20

TM_PROJ = 512
COL_CHUNK = 512
TQ = 256
TK = 256
NEAR = 2 * TQ
FF_CHUNK = 256
MASK_VALUE = -1e30
VMEM_LIMIT = 56 * 1024 * 1024

NT_DIMS = (((1,), (1,)), ((), ()))


def _t5_bucket(rel):
    nb = NUM_BUCKETS // 2
    max_exact = nb // 2
    ret = (rel > 0).astype(jnp.int32) * nb
    n = jnp.abs(rel)
    nf = jnp.maximum(n, 1).astype(jnp.float32)
    large = max_exact + (jnp.log(nf / max_exact) / math.log(MAX_DISTANCE / max_exact)
                         * (nb - max_exact)).astype(jnp.int32)
    large = jnp.minimum(large, nb - 1)
    return ret + jnp.where(n < max_exact, n, large)


def _split_hi_lo(x):
    hi = x.astype(jnp.bfloat16)
    lo = (x - hi.astype(jnp.float32)).astype(jnp.bfloat16)
    return hi, lo


def _in_proj_kernel(x_ref, w1_ref, win_ref, qkw_ref, g_ref, o_ref):
    x = x_ref[...]
    ms = jnp.mean(x * x, axis=-1, keepdims=True)
    u = ((x * lax.rsqrt(ms + EPS)) * w1_ref[...]).astype(jnp.bfloat16)
    g = g_ref[...]
    for c in range(IN_COLS // COL_CHUNK):
        lo_c, hi_c = c * COL_CHUNK, (c + 1) * COL_CHUNK
        p = jnp.dot(u, win_ref[:, lo_c:hi_c], preferred_element_type=jnp.float32)
        if hi_c <= 2 * DIFF_HEADS * 2 * DIFF_QK_DIM:
            for j in range(COL_CHUNK // LANES):
                sl = slice(j * LANES, (j + 1) * LANES)
                pj = p[:, sl]
                hi, lo = _split_hi_lo(pj * pj)
                msq = (jnp.dot(hi, g, preferred_element_type=jnp.float32)
                       + jnp.dot(lo, g, preferred_element_type=jnp.float32))
                w = qkw_ref[:, lo_c + j * LANES: lo_c + (j + 1) * LANES]
                o_ref[:, lo_c + j * LANES: lo_c + (j + 1) * LANES] = (
                    (pj * lax.rsqrt(msq + EPS)) * w).astype(jnp.bfloat16)
        else:
            if lo_c == QS_BLK * LANES:
                p = p * (SB_DIM ** -0.5)
            o_ref[:, lo_c:hi_c] = p.astype(jnp.bfloat16)


def _in_proj(x2d, norm1_w, w_in_bf, qkw, gmat):
    n = x2d.shape[0]
    const = lambda i: (0, 0)
    return pl.pallas_call(
        _in_proj_kernel,
        grid=(n // TM_PROJ,),
        in_specs=[
            pl.BlockSpec((TM_PROJ, D_MODEL), lambda i: (i, 0)),
            pl.BlockSpec((1, D_MODEL), const),
            pl.BlockSpec((D_MODEL, IN_COLS), const, pipeline_mode=pl.Buffered(1)),
            pl.BlockSpec((1, 2 * DIFF_HEADS * 2 * DIFF_QK_DIM), const),
            pl.BlockSpec((LANES, LANES), const),
        ],
        out_specs=pl.BlockSpec((TM_PROJ, IN_COLS), lambda i: (i, 0)),
        out_shape=jax.ShapeDtypeStruct((n, IN_COLS), jnp.bfloat16),
        compiler_params=pltpu.CompilerParams(
            dimension_semantics=("arbitrary",), vmem_limit_bytes=VMEM_LIMIT),
        name="in_proj",
    )(x2d, norm1_w, w_in_bf, qkw, gmat)


def _stack_subheads(q):
    lane = lax.broadcasted_iota(jnp.int32, q.shape, 1)
    zero = jnp.zeros_like(q)
    return jnp.concatenate([jnp.where(lane < SB_DIM, q, zero),
                            jnp.where(lane >= SB_DIM, q, zero)], axis=0)


def _diff_attn_kernel(cfar_ref, lam_ref, q_ref, k_ref, v_ref, bias_ref, w_ref, o_ref,
                      m_sc, l_sc, acc_sc):
    h = pl.program_id(1)
    qi = pl.program_id(2)
    qq = _stack_subheads(q_ref[0])

    kstart = pl.multiple_of(jnp.maximum(qi - 1, 0) * TQ, TQ)
    k = k_ref[0, pl.ds(kstart, NEAR), :]
    v = v_ref[0, pl.ds(kstart, NEAR), :]
    s = lax.dot_general(qq, k, NT_DIMS, preferred_element_type=jnp.float32) + bias_ref[0, 0]
    m = jnp.max(s, axis=-1, keepdims=True)
    p = jnp.exp(s - m)
    m_sc[...] = m
    l_sc[...] = jnp.sum(p, axis=-1, keepdims=True)
    acc_sc[...] = jnp.dot(p.astype(jnp.bfloat16), v, preferred_element_type=jnp.float32)

    cfar = cfar_ref[h]

    def far_step(i, carry):
        ks = pl.multiple_of(i * TK, TK)
        kf = k_ref[0, pl.ds(ks, TK), :]
        vf = v_ref[0, pl.ds(ks, TK), :]
        sf = lax.dot_general(qq, kf, NT_DIMS, preferred_element_type=jnp.float32) + cfar
        m_old = m_sc[...]
        m_new = jnp.maximum(m_old, jnp.max(sf, axis=-1, keepdims=True))
        alpha = jnp.exp(m_old - m_new)
        pf = jnp.exp(sf - m_new)
        l_sc[...] = alpha * l_sc[...] + jnp.sum(pf, axis=-1, keepdims=True)
        acc_sc[...] = alpha * acc_sc[...] + jnp.dot(pf.astype(jnp.bfloat16), vf,
                                                    preferred_element_type=jnp.float32)
        m_sc[...] = m_new
        return carry

    lax.fori_loop(0, jnp.maximum(qi - 1, 0), far_step, 0)

    acc = acc_sc[...]
    l = l_sc[...]
    o = acc[:TQ] / l[:TQ] - lam_ref[0] * (acc[TQ:] / l[TQ:])
    ms = jnp.mean(o * o, axis=-1, keepdims=True)
    y = ((o * lax.rsqrt(ms + EPS)) * w_ref[...]) * (1.0 - LAMBDA_INIT)
    o_ref[0] = y.astype(o_ref.dtype)


def _diff_attn(proj, bias_near, cfar, lam, w_out_norm):
    b, s, _ = proj.shape
    nq = s // TQ
    smem = pl.BlockSpec(memory_space=pltpu.SMEM)
    return pl.pallas_call(
        _diff_attn_kernel,
        grid=(b, DIFF_HEADS, nq),
        in_specs=[
            smem, smem,
            pl.BlockSpec((1, TQ, LANES), lambda bi, h, qi: (bi, qi, QD_BLK + h)),
            pl.BlockSpec((1, s, LANES), lambda bi, h, qi: (bi, 0, KD_BLK + h)),
            pl.BlockSpec((1, s, LANES), lambda bi, h, qi: (bi, 0, VD_BLK + h)),
            pl.BlockSpec((1, 1, 2 * TQ, NEAR), lambda bi, h, qi: (jnp.minimum(qi, 1), h, 0, 0)),
            pl.BlockSpec((1, DIFF_V_DIM), lambda bi, h, qi: (0, 0)),
        ],
        out_specs=pl.BlockSpec((1, TQ, LANES), lambda bi, h, qi: (bi, qi, h)),
        out_shape=jax.ShapeDtypeStruct((b, s, DIFF_HEADS * DIFF_V_DIM), jnp.bfloat16),
        scratch_shapes=[
            pltpu.VMEM((2 * TQ, 1), jnp.float32),
            pltpu.VMEM((2 * TQ, 1), jnp.float32),
            pltpu.VMEM((2 * TQ, DIFF_V_DIM), jnp.float32),
        ],
        compiler_params=pltpu.CompilerParams(
            dimension_semantics=("arbitrary", "arbitrary", "arbitrary"),
            vmem_limit_bytes=VMEM_LIMIT),
        name="diff_attn",
    )(cfar, lam, proj, proj, proj, bias_near, w_out_norm)


def _softplus(z):
    return jnp.maximum(z, 0.0) + jnp.log(1.0 + jnp.exp(-jnp.abs(z)))


def _sb_attn_kernel(q_ref, k_ref, v_ref, tri_ref, w_ref, o_ref, r_sc, acc_sc):
    qi = pl.program_id(2)
    qq = _stack_subheads(q_ref[0])
    tri = tri_ref[...]

    def cost_suffix(c):
        hi, lo = _split_hi_lo(c)
        return (jnp.dot(hi, tri, preferred_element_type=jnp.float32)
                + jnp.dot(lo, tri, preferred_element_type=jnp.float32))

    ks = pl.multiple_of(qi * TQ, TQ)
    k = k_ref[0, pl.ds(ks, TK), :]
    v = v_ref[0, pl.ds(ks, TK), :]
    z = lax.dot_general(qq, k, NT_DIMS, preferred_element_type=jnp.float32)
    row = lax.broadcasted_iota(jnp.int32, z.shape, 0)
    row = jnp.where(row >= TQ, row - TQ, row)
    col = lax.broadcasted_iota(jnp.int32, z.shape, 1)
    mask = col < row
    sp = _softplus(z)
    c = jnp.where(mask, sp, 0.0)
    crem = cost_suffix(c)
    a = jnp.where(mask, jnp.exp(z - sp - crem), 0.0)
    acc_sc[...] = jnp.dot(a.astype(jnp.bfloat16), v, preferred_element_type=jnp.float32)
    r_sc[...] = crem[:, 0:1] + c[:, 0:1]

    def step(i, carry):
        ksi = pl.multiple_of((qi - 1 - i) * TK, TK)
        kf = k_ref[0, pl.ds(ksi, TK), :]
        vf = v_ref[0, pl.ds(ksi, TK), :]
        zf = lax.dot_general(qq, kf, NT_DIMS, preferred_element_type=jnp.float32)
        spf = _softplus(zf)
        cremf = cost_suffix(spf)
        r = r_sc[...]
        af = jnp.exp(zf - spf - cremf - r)
        acc_sc[...] += jnp.dot(af.astype(jnp.bfloat16), vf, preferred_element_type=jnp.float32)
        r_sc[...] = r + cremf[:, 0:1] + spf[:, 0:1]
        return carry

    lax.fori_loop(0, qi, step, 0)

    acc = acc_sc[...]
    lane = lax.broadcasted_iota(jnp.int32, (TQ, LANES), 1)
    first = lane < SB_DIM
    o = jnp.where(first, acc[:TQ], acc[TQ:])
    o2 = o * o
    ss_a = jnp.sum(jnp.where(first, o2, 0.0), axis=-1, keepdims=True)
    ss_b = jnp.sum(jnp.where(first, 0.0, o2), axis=-1, keepdims=True)
    ms = jnp.where(first, ss_a, ss_b) * (1.0 / SB_DIM)
    o_ref[0] = ((o * lax.rsqrt(ms + EPS)) * w_ref[...]).astype(o_ref.dtype)


def _sb_attn(proj, tri, w_out_norm):
    b, s, _ = proj.shape
    nq = s // TQ
    npair = SB_HEADS // 2
    return pl.pallas_call(
        _sb_attn_kernel,
        grid=(b, npair, nq),
        in_specs=[
            pl.BlockSpec((1, TQ, LANES), lambda bi, j, qi: (bi, qi, QS_BLK + j)),
            pl.BlockSpec((1, s, LANES), lambda bi, j, qi: (bi, 0, KS_BLK + j)),
            pl.BlockSpec((1, s, LANES), lambda bi, j, qi: (bi, 0, VS_BLK + j)),
            pl.BlockSpec((TK, TK), lambda bi, j, qi: (0, 0)),
            pl.BlockSpec((1, LANES), lambda bi, j, qi: (0, 0)),
        ],
        out_specs=pl.BlockSpec((1, TQ, LANES), lambda bi, j, qi: (bi, qi, j)),
        out_shape=jax.ShapeDtypeStruct((b, s, SB_HEADS * SB_DIM), jnp.bfloat16),
        scratch_shapes=[
            pltpu.VMEM((2 * TQ, 1), jnp.float32),
            pltpu.VMEM((2 * TQ, LANES), jnp.float32),
        ],
        compiler_params=pltpu.CompilerParams(
            dimension_semantics=("arbitrary", "arbitrary", "arbitrary"),
            vmem_limit_bytes=VMEM_LIMIT),
        name="sb_attn",
    )(proj, proj, proj, tri, w_out_norm)


def _out_ffn_kernel(x_ref, yd_ref, ys_ref, woa_ref, wob_ref, w2_ref, wg_ref, wu_ref, wd_ref,
                    o_ref):
    h1 = (x_ref[...]
          + jnp.dot(yd_ref[...], woa_ref[...], preferred_element_type=jnp.float32)
          + jnp.dot(ys_ref[...], wob_ref[...], preferred_element_type=jnp.float32))
    ms = jnp.mean(h1 * h1, axis=-1, keepdims=True)
    u2 = ((h1 * lax.rsqrt(ms + EPS)) * w2_ref[...]).astype(jnp.bfloat16)
    ffn = None
    for c in range(D_FF // FF_CHUNK):
        sl = slice(c * FF_CHUNK, (c + 1) * FF_CHUNK)
        gate = jnp.dot(u2, wg_ref[:, sl], preferred_element_type=jnp.float32)
        up = jnp.dot(u2, wu_ref[:, sl], preferred_element_type=jnp.float32)
        act = (gate * (1.0 / (1.0 + jnp.exp(-gate))) * up).astype(jnp.bfloat16)
        part = jnp.dot(act, wd_ref[sl, :], preferred_element_type=jnp.float32)
        ffn = part if ffn is None else ffn + part
    o_ref[...] = h1 + ffn


def _out_ffn(x2d, yd, ys, woa, wob, norm2_w, wg, wu, wd):
    n = x2d.shape[0]
    const = lambda i: (0, 0)
    half = DIFF_HEADS * DIFF_V_DIM
    resident = partial(pl.BlockSpec, index_map=const, pipeline_mode=pl.Buffered(1))
    return pl.pallas_call(
        _out_ffn_kernel,
        grid=(n // TM_PROJ,),
        in_specs=[
            pl.BlockSpec((TM_PROJ, D_MODEL), lambda i: (i, 0)),
            pl.BlockSpec((TM_PROJ, half), lambda i: (i, 0)),
            pl.BlockSpec((TM_PROJ, half), lambda i: (i, 0)),
            resident((half, D_MODEL)),
            resident((half, D_MODEL)),
            pl.BlockSpec((1, D_MODEL), const),
            resident((D_MODEL, D_FF)),
            resident((D_MODEL, D_FF)),
            resident((D_FF, D_MODEL)),
        ],
        out_specs=pl.BlockSpec((TM_PROJ, D_MODEL), lambda i: (i, 0)),
        out_shape=jax.ShapeDtypeStruct((n, D_MODEL), jnp.float32),
        compiler_params=pltpu.CompilerParams(
            dimension_semantics=("arbitrary",), vmem_limit_bytes=VMEM_LIMIT),
        name="out_ffn",
    )(x2d, yd, ys, woa, wob, norm2_w, wg, wu, wd)


def _near_bias_tiles(rel_bias):
    qq = jnp.arange(TQ, dtype=jnp.int32)[:, None]
    kk = jnp.arange(NEAR, dtype=jnp.int32)[None, :]
    tiles = []
    for koff in (0, -TQ):
        kpos = kk + koff
        bias = rel_bias.astype(jnp.float32)[_t5_bucket(kpos - qq)]
        allowed = jnp.floor_divide(kpos, CHUNK) <= jnp.floor_divide(qq, CHUNK)
        tile = jnp.where(allowed[:, :, None], bias, MASK_VALUE)
        tile = jnp.transpose(tile, (2, 0, 1))
        tiles.append(jnp.concatenate([tile, tile], axis=1))
    return jnp.stack(tiles, axis=0)


def kernel(x, norm1_w, w_in, q_norm_w, k_norm_w, lambda_q1, lambda_k1, lambda_q2, lambda_k2,
           diff_out_norm_w, sb_out_norm_w, w_out, norm2_w, w_gate, w_up, w_down, rel_bias):
    b, s, d = x.shape
    assert (b, s, d) == (8, 4096, D_MODEL) and w_in.shape[0] == 1
    f32, bf16 = jnp.float32, jnp.bfloat16
    x2d = x.reshape(b * s, d)

    qkw = jnp.concatenate([jnp.tile(q_norm_w[0].astype(f32), 2 * DIFF_HEADS) * (DIFF_QK_DIM ** -0.5),
                           jnp.tile(k_norm_w[0].astype(f32), 2 * DIFF_HEADS)])[None, :]
    grp = jnp.arange(LANES) // DIFF_QK_DIM
    gmat = jnp.where(grp[:, None] == grp[None, :], 1.0 / DIFF_QK_DIM, 0.0).astype(bf16)
    lam = (jnp.exp(jnp.sum(lambda_q1[0].astype(f32) * lambda_k1[0].astype(f32)))
           - jnp.exp(jnp.sum(lambda_q2[0].astype(f32) * lambda_k2[0].astype(f32)))
           + LAMBDA_INIT).reshape(1)
    bias_near = _near_bias_tiles(rel_bias)
    cfar = rel_bias.astype(f32)[_t5_bucket(jnp.int32(-NEAR))]
    jj = jnp.arange(TK)
    tri = (jj[:, None] > jj[None, :]).astype(bf16)

    proj = _in_proj(x2d, norm1_w[0].astype(f32)[None, :], w_in[0].astype(bf16), qkw, gmat)
    proj = proj.reshape(b, s, IN_COLS)
    y_diff = _diff_attn(proj, bias_near, cfar, lam, diff_out_norm_w[0].astype(f32)[None, :])
    y_sb = _sb_attn(proj, tri, jnp.tile(sb_out_norm_w[0].astype(f32), 2)[None, :])

    half = DIFF_HEADS * DIFF_V_DIM
    wo = w_out[0].astype(bf16)
    out = _out_ffn(x2d, y_diff.reshape(b * s, half), y_sb.reshape(b * s, half),
                   wo[:half], wo[half:], norm2_w[0].astype(f32)[None, :],
                   w_gate[0].astype(bf16), w_up[0].astype(bf16), w_down[0].astype(bf16))
    return out.reshape(b, s, d)
```

```python
import math
from functools import partial

import jax
import jax.numpy as jnp
from jax import lax
from jax.experimental import pallas as pl
from jax.experimental.pallas import tpu as pltpu

D_MODEL = 1024
CHUNK = 64
DIFF_QK_DIM = 64
DIFF_V_DIM = 128
DIFF_HEADS = 4
SB_DIM = 64
SB_HEADS = 8
NUM_BUCKETS = 32
MAX_DISTANCE = 128
D_FF = 2816
EPS = 1e-6
IN_COLS = 3072
LAMBDA_INIT = 0.8 - 0.6 * math.exp(-0.3 * 0)
LANES = 128

QD_BLK, KD_BLK, VD_BLK = 0, 4, 8
QS_BLK, KS_BLK, VS_BLK = 12, 16, 20

TM_PROJ = 512
COL_CHUNK = 512
TQ = 256
TK = 256
NEAR = 2 * TQ
FF_CHUNK = 256
MASK_VALUE = -1e30
SB_STOP_COST = 104.0
VMEM_LIMIT = 56 * 1024 * 1024

NT_DIMS = (((1,), (1,)), ((), ()))


def _t5_bucket(rel):
    nb = NUM_BUCKETS // 2
    max_exact = nb // 2
    ret = (rel > 0).astype(jnp.int32) * nb
    n = jnp.abs(rel)
    nf = jnp.maximum(n, 1).astype(jnp.float32)
    large = max_exact + (jnp.log(nf / max_exact) / math.log(MAX_DISTANCE / max_exact)
                         * (nb - max_exact)).astype(jnp.int32)
    large = jnp.minimum(large, nb - 1)
    return ret + jnp.where(n < max_exact, n, large)


def _split_hi_lo(x):
    hi = x.astype(jnp.bfloat16)
    lo = (x - hi.astype(jnp.float32)).astype(jnp.bfloat16)
    return hi, lo


def _in_proj_kernel(x_ref, w1_ref, win_ref, qkw_ref, g_ref, o_ref):
    x = x_ref[...]
    ms = jnp.mean(x * x, axis=-1, keepdims=True)
    u = ((x * lax.rsqrt(ms + EPS)) * w1_ref[...]).astype(jnp.bfloat16)
    g = g_ref[...]
    for c in range(IN_COLS // COL_CHUNK):
        lo_c, hi_c = c * COL_CHUNK, (c + 1) * COL_CHUNK
        p = jnp.dot(u, win_ref[:, lo_c:hi_c], preferred_element_type=jnp.float32)
        if hi_c <= 2 * DIFF_HEADS * 2 * DIFF_QK_DIM:
            for j in range(COL_CHUNK // LANES):
                sl = slice(j * LANES, (j + 1) * LANES)
                pj = p[:, sl]
                hi, lo = _split_hi_lo(pj * pj)
                msq = (jnp.dot(hi, g, preferred_element_type=jnp.float32)
                       + jnp.dot(lo, g, preferred_element_type=jnp.float32))
                w = qkw_ref[:, lo_c + j * LANES: lo_c + (j + 1) * LANES]
                o_ref[:, lo_c + j * LANES: lo_c + (j + 1) * LANES] = (
                    (pj * lax.rsqrt(msq + EPS)) * w).astype(jnp.bfloat16)
        else:
            if lo_c == QS_BLK * LANES:
                p = p * (SB_DIM ** -0.5)
            o_ref[:, lo_c:hi_c] = p.astype(jnp.bfloat16)


def _in_proj(x2d, norm1_w, w_in_bf, qkw, gmat):
    n = x2d.shape[0]
    const = lambda i: (0, 0)
    return pl.pallas_call(
        _in_proj_kernel,
        grid=(n // TM_PROJ,),
        in_specs=[
            pl.BlockSpec((TM_PROJ, D_MODEL), lambda i: (i, 0)),
            pl.BlockSpec((1, D_MODEL), const),
            pl.BlockSpec((D_MODEL, IN_COLS), const, pipeline_mode=pl.Buffered(1)),
            pl.BlockSpec((1, 2 * DIFF_HEADS * 2 * DIFF_QK_DIM), const),
            pl.BlockSpec((LANES, LANES), const),
        ],
        out_specs=pl.BlockSpec((TM_PROJ, IN_COLS), lambda i: (i, 0)),
        out_shape=jax.ShapeDtypeStruct((n, IN_COLS), jnp.bfloat16),
        compiler_params=pltpu.CompilerParams(
            dimension_semantics=("arbitrary",), vmem_limit_bytes=VMEM_LIMIT),
        name="in_proj",
    )(x2d, norm1_w, w_in_bf, qkw, gmat)


def _stack_subheads(q):
    lane = lax.broadcasted_iota(jnp.int32, q.shape, 1)
    zero = jnp.zeros_like(q)
    return jnp.concatenate([jnp.where(lane < SB_DIM, q, zero),
                            jnp.where(lane >= SB_DIM, q, zero)], axis=0)


def _diff_attn_kernel(cfar_ref, lam_ref, q_ref, k_ref, v_ref, bias_ref, w_ref, o_ref,
                      m_sc, l_sc, acc_sc):
    h = pl.program_id(1)
    qi = pl.program_id(2)
    qq = _stack_subheads(q_ref[0])

    kstart = pl.multiple_of(jnp.maximum(qi - 1, 0) * TQ, TQ)
    k = k_ref[0, pl.ds(kstart, NEAR), :]
    v = v_ref[0, pl.ds(kstart, NEAR), :]
    s = lax.dot_general(qq, k, NT_DIMS, preferred_element_type=jnp.float32) + bias_ref[0, 0]
    m = jnp.max(s, axis=-1, keepdims=True)
    p = jnp.exp(s - m)
    m_sc[...] = m
    l_sc[...] = jnp.sum(p, axis=-1, keepdims=True)
    acc_sc[...] = jnp.dot(p.astype(jnp.bfloat16), v, preferred_element_type=jnp.float32)

    cfar = cfar_ref[h]

    def far_step(i, carry):
        ks = pl.multiple_of(i * TK, TK)
        kf = k_ref[0, pl.ds(ks, TK), :]
        vf = v_ref[0, pl.ds(ks, TK), :]
        sf = lax.dot_general(qq, kf, NT_DIMS, preferred_element_type=jnp.float32) + cfar
        m_old = m_sc[...]
        m_new = jnp.maximum(m_old, jnp.max(sf, axis=-1, keepdims=True))
        alpha = jnp.exp(m_old - m_new)
        pf = jnp.exp(sf - m_new)
        l_sc[...] = alpha * l_sc[...] + jnp.sum(pf, axis=-1, keepdims=True)
        acc_sc[...] = alpha * acc_sc[...] + jnp.dot(pf.astype(jnp.bfloat16), vf,
                                                    preferred_element_type=jnp.float32)
        m_sc[...] = m_new
        return carry

    lax.fori_loop(0, jnp.maximum(qi - 1, 0), far_step, 0)

    acc = acc_sc[...]
    l = l_sc[...]
    o = acc[:TQ] / l[:TQ] - lam_ref[0] * (acc[TQ:] / l[TQ:])
    ms = jnp.mean(o * o, axis=-1, keepdims=True)
    y = ((o * lax.rsqrt(ms + EPS)) * w_ref[...]) * (1.0 - LAMBDA_INIT)
    o_ref[0] = y.astype(o_ref.dtype)


def _diff_attn(proj, bias_near, cfar, lam, w_out_norm):
    b, s, _ = proj.shape
    nq = s // TQ
    smem = pl.BlockSpec(memory_space=pltpu.SMEM)
    return pl.pallas_call(
        _diff_attn_kernel,
        grid=(b, DIFF_HEADS, nq),
        in_specs=[
            smem, smem,
            pl.BlockSpec((1, TQ, LANES), lambda bi, h, qi: (bi, qi, QD_BLK + h)),
            pl.BlockSpec((1, s, LANES), lambda bi, h, qi: (bi, 0, KD_BLK + h)),
            pl.BlockSpec((1, s, LANES), lambda bi, h, qi: (bi, 0, VD_BLK + h)),
            pl.BlockSpec((1, 1, 2 * TQ, NEAR), lambda bi, h, qi: (jnp.minimum(qi, 1), h, 0, 0)),
            pl.BlockSpec((1, DIFF_V_DIM), lambda bi, h, qi: (0, 0)),
        ],
        out_specs=pl.BlockSpec((1, TQ, LANES), lambda bi, h, qi: (bi, qi, h)),
        out_shape=jax.ShapeDtypeStruct((b, s, DIFF_HEADS * DIFF_V_DIM), jnp.bfloat16),
        scratch_shapes=[
            pltpu.VMEM((2 * TQ, 1), jnp.float32),
            pltpu.VMEM((2 * TQ, 1), jnp.float32),
            pltpu.VMEM((2 * TQ, DIFF_V_DIM), jnp.float32),
        ],
        compiler_params=pltpu.CompilerParams(
            dimension_semantics=("arbitrary", "arbitrary", "arbitrary"),
            vmem_limit_bytes=VMEM_LIMIT),
        name="diff_attn",
    )(cfar, lam, proj, proj, proj, bias_near, w_out_norm)


def _softplus(z):
    return jnp.maximum(z, 0.0) + jnp.log(1.0 + jnp.exp(-jnp.abs(z)))


def _sb_attn_kernel(q_ref, k_ref, v_ref, tri_ref, w_ref, o_ref, r_sc, acc_sc):
    qi = pl.program_id(2)
    qq = _stack_subheads(q_ref[0])
    tri = tri_ref[...]

    def cost_suffix(c):
        hi, lo = _split_hi_lo(c)
        return (jnp.dot(hi, tri, preferred_element_type=jnp.float32)
                + jnp.dot(lo, tri, preferred_element_type=jnp.float32))

    ks = pl.multiple_of(qi * TQ, TQ)
    k = k_ref[0, pl.ds(ks, TK), :]
    v = v_ref[0, pl.ds(ks, TK), :]
    z = lax.dot_general(qq, k, NT_DIMS, preferred_element_type=jnp.float32)
    row = lax.broadcasted_iota(jnp.int32, z.shape, 0)
    row = jnp.where(row >= TQ, row - TQ, row)
    col = lax.broadcasted_iota(jnp.int32, z.shape, 1)
    mask = col < row
    sp = _softplus(z)
    c = jnp.where(mask, sp, 0.0)
    crem = cost_suffix(c)
    a = jnp.where(mask, jnp.exp(z - sp - crem), 0.0)
    acc_sc[...] = jnp.dot(a.astype(jnp.bfloat16), v, preferred_element_type=jnp.float32)
    r_sc[...] = crem[:, 0:1] + c[:, 0:1]

    def more(carry):
        i, r_min = carry
        return jnp.logical_and(i < qi, r_min < SB_STOP_COST)

    def step(carry):
        i, _ = carry
        ksi = pl.multiple_of((qi - 1 - i) * TK, TK)
        kf = k_ref[0, pl.ds(ksi, TK), :]
        vf = v_ref[0, pl.ds(ksi, TK), :]
        zf = lax.dot_general(qq, kf, NT_DIMS, preferred_element_type=jnp.float32)
        spf = _softplus(zf)
        cremf = cost_suffix(spf)
        r = r_sc[...]
        af = jnp.exp(zf - spf - cremf - r)
        acc_sc[...] += jnp.dot(af.astype(jnp.bfloat16), vf, preferred_element_type=jnp.float32)
        r_new = r + cremf[:, 0:1] + spf[:, 0:1]
        r_sc[...] = r_new
        return i + 1, jnp.min(r_new)

    lax.while_loop(more, step, (jnp.int32(0), jnp.float32(0.0)))

    acc = acc_sc[...]
    lane = lax.broadcasted_iota(jnp.int32, (TQ, LANES), 1)
    first = lane < SB_DIM
    o = jnp.where(first, acc[:TQ], acc[TQ:])
    o2 = o * o
    ss_a = jnp.sum(jnp.where(first, o2, 0.0), axis=-1, keepdims=True)
    ss_b = jnp.sum(jnp.where(first, 0.0, o2), axis=-1, keepdims=True)
    ms = jnp.where(first, ss_a, ss_b) * (1.0 / SB_DIM)
    o_ref[0] = ((o * lax.rsqrt(ms + EPS)) * w_ref[...]).astype(o_ref.dtype)


def _sb_attn(proj, tri, w_out_norm):
    b, s, _ = proj.shape
    nq = s // TQ
    npair = SB_HEADS // 2
    return pl.pallas_call(
        _sb_attn_kernel,
        grid=(b, npair, nq),
        in_specs=[
            pl.BlockSpec((1, TQ, LANES), lambda bi, j, qi: (bi, qi, QS_BLK + j)),
            pl.BlockSpec((1, s, LANES), lambda bi, j, qi: (bi, 0, KS_BLK + j)),
            pl.BlockSpec((1, s, LANES), lambda bi, j, qi: (bi, 0, VS_BLK + j)),
            pl.BlockSpec((TK, TK), lambda bi, j, qi: (0, 0)),
            pl.BlockSpec((1, LANES), lambda bi, j, qi: (0, 0)),
        ],
        out_specs=pl.BlockSpec((1, TQ, LANES), lambda bi, j, qi: (bi, qi, j)),
        out_shape=jax.ShapeDtypeStruct((b, s, SB_HEADS * SB_DIM), jnp.bfloat16),
        scratch_shapes=[
            pltpu.VMEM((2 * TQ, 1), jnp.float32),
            pltpu.VMEM((2 * TQ, LANES), jnp.float32),
        ],
        compiler_params=pltpu.CompilerParams(
            dimension_semantics=("arbitrary", "arbitrary", "arbitrary"),
            vmem_limit_bytes=VMEM_LIMIT),
        name="sb_attn",
    )(proj, proj, proj, tri, w_out_norm)


def _out_ffn_kernel(x_ref, yd_ref, ys_ref, woa_ref, wob_ref, w2_ref, wg_ref, wu_ref, wd_ref,
                    o_ref):
    h1 = (x_ref[...]
          + jnp.dot(yd_ref[...], woa_ref[...], preferred_element_type=jnp.float32)
          + jnp.dot(ys_ref[...], wob_ref[...], preferred_element_type=jnp.float32))
    ms = jnp.mean(h1 * h1, axis=-1, keepdims=True)
    u2 = ((h1 * lax.rsqrt(ms + EPS)) * w2_ref[...]).astype(jnp.bfloat16)
    ffn = None
    for c in range(D_FF // FF_CHUNK):
        sl = slice(c * FF_CHUNK, (c + 1) * FF_CHUNK)
        gate = jnp.dot(u2, wg_ref[:, sl], preferred_element_type=jnp.float32)
        up = jnp.dot(u2, wu_ref[:, sl], preferred_element_type=jnp.float32)
        act = (gate * (1.0 / (1.0 + jnp.exp(-gate))) * up).astype(jnp.bfloat16)
        part = jnp.dot(act, wd_ref[sl, :], preferred_element_type=jnp.float32)
        ffn = part if ffn is None else ffn + part
    o_ref[...] = h1 + ffn


def _out_ffn(x2d, yd, ys, woa, wob, norm2_w, wg, wu, wd):
    n = x2d.shape[0]
    const = lambda i: (0, 0)
    half = DIFF_HEADS * DIFF_V_DIM
    resident = partial(pl.BlockSpec, index_map=const, pipeline_mode=pl.Buffered(1))
    return pl.pallas_call(
        _out_ffn_kernel,
        grid=(n // TM_PROJ,),
        in_specs=[
            pl.BlockSpec((TM_PROJ, D_MODEL), lambda i: (i, 0)),
            pl.BlockSpec((TM_PROJ, half), lambda i: (i, 0)),
            pl.BlockSpec((TM_PROJ, half), lambda i: (i, 0)),
            resident((half, D_MODEL)),
            resident((half, D_MODEL)),
            pl.BlockSpec((1, D_MODEL), const),
            resident((D_MODEL, D_FF)),
            resident((D_MODEL, D_FF)),
            resident((D_FF, D_MODEL)),
        ],
        out_specs=pl.BlockSpec((TM_PROJ, D_MODEL), lambda i: (i, 0)),
        out_shape=jax.ShapeDtypeStruct((n, D_MODEL), jnp.float32),
        compiler_params=pltpu.CompilerParams(
            dimension_semantics=("arbitrary",), vmem_limit_bytes=VMEM_LIMIT),
        name="out_ffn",
    )(x2d, yd, ys, woa, wob, norm2_w, wg, wu, wd)


def _near_bias_tiles(rel_bias):
    qq = jnp.arange(TQ, dtype=jnp.int32)[:, None]
    kk = jnp.arange(NEAR, dtype=jnp.int32)[None, :]
    width = 2 * NEAR
    tiles = []
    for koff in (0, -TQ):
        kpos = kk + koff
        table = rel_bias.astype(jnp.float32)[
            _t5_bucket(jnp.arange(width, dtype=jnp.int32) - TQ + koff)].T
        toep = jnp.tile(table, (1, TQ))[:, :TQ * (width - 1)].reshape(DIFF_HEADS, TQ, width - 1)
        bias = toep[:, :, TQ:TQ + NEAR]
        allowed = jnp.floor_divide(kpos, CHUNK) <= jnp.floor_divide(qq, CHUNK)
        tile = jnp.where(allowed[None], bias, MASK_VALUE)
        tiles.append(jnp.concatenate([tile, tile], axis=1))
    return jnp.stack(tiles, axis=0)


def kernel(x, norm1_w, w_in, q_norm_w, k_norm_w, lambda_q1, lambda_k1, lambda_q2, lambda_k2,
           diff_out_norm_w, sb_out_norm_w, w_out, norm2_w, w_gate, w_up, w_down, rel_bias):
    b, s, d = x.shape
    assert (b, s, d) == (8, 4096, D_MODEL) and w_in.shape[0] == 1
    f32, bf16 = jnp.float32, jnp.bfloat16
    x2d = x.reshape(b * s, d)

    qkw = jnp.concatenate([jnp.tile(q_norm_w[0].astype(f32), 2 * DIFF_HEADS) * (DIFF_QK_DIM ** -0.5),
                           jnp.tile(k_norm_w[0].astype(f32), 2 * DIFF_HEADS)])[None, :]
    grp = jnp.arange(LANES) // DIFF_QK_DIM
    gmat = jnp.where(grp[:, None] == grp[None, :], 1.0 / DIFF_QK_DIM, 0.0).astype(bf16)
    lam = (jnp.exp(jnp.sum(lambda_q1[0].astype(f32) * lambda_k1[0].astype(f32)))
           - jnp.exp(jnp.sum(lambda_q2[0].astype(f32) * lambda_k2[0].astype(f32)))
           + LAMBDA_INIT).reshape(1)
    bias_near = _near_bias_tiles(rel_bias)
    cfar = rel_bias.astype(f32)[_t5_bucket(jnp.int32(-NEAR))]
    jj = jnp.arange(TK)
    tri = (jj[:, None] > jj[None, :]).astype(bf16)

    proj = _in_proj(x2d, norm1_w[0].astype(f32)[None, :], w_in[0].astype(bf16), qkw, gmat)
    proj = proj.reshape(b, s, IN_COLS)
    y_diff = _diff_attn(proj, bias_near, cfar, lam, diff_out_norm_w[0].astype(f32)[None, :])
    y_sb = _sb_attn(proj, tri, jnp.tile(sb_out_norm_w[0].astype(f32), 2)[None, :])

    half = DIFF_HEADS * DIFF_V_DIM
    wo = w_out[0].astype(bf16)
    out = _out_ffn(x2d, y_diff.reshape(b * s, half), y_sb.reshape(b * s, half),
                   wo[:half], wo[half:], norm2_w[0].astype(f32)[None, :],
                   w_gate[0].astype(bf16), w_up[0].astype(bf16), w_down[0].astype(bf16))
    return out.reshape(b, s, d)
```

```python
import math
from functools import partial

import jax
import jax.numpy as jnp
from jax import lax
from jax.experimental import pallas as pl
from jax.experimental.pallas import tpu as pltpu

D_MODEL = 1024
CHUNK = 64
DIFF_QK_DIM = 64
DIFF_V_DIM = 128
DIFF_HEADS = 4
SB_DIM = 64
SB_HEADS = 8
NUM_BUCKETS = 32
MAX_DISTANCE = 128
D_FF = 2816
EPS = 1e-6
IN_COLS = 3072
LAMBDA_INIT = 0.8 - 0.6 * math.exp(-0.3 * 0)
LANES = 128

QD_BLK, KD_BLK, VD_BLK = 0, 4, 8
QS_BLK, KS_BLK, VS_BLK = 12, 16, 20

TM_PROJ = 512
COL_CHUNK = 512
TQD = 512
TKD = 512
TQ = 256
TK = 256
FF_CHUNK = 256
MASK_VALUE = -1e30
SB_STOP_COST = 104.0
VMEM_LIMIT = 56 * 1024 * 1024

NT_DIMS = (((1,), (1,)), ((), ()))


def _t5_bucket(rel):
    nb = NUM_BUCKETS // 2
    max_exact = nb // 2
    ret = (rel > 0).astype(jnp.int32) * nb
    n = jnp.abs(rel)
    nf = jnp.maximum(n, 1).astype(jnp.float32)
    large = max_exact + (jnp.log(nf / max_exact) / math.log(MAX_DISTANCE / max_exact)
                         * (nb - max_exact)).astype(jnp.int32)
    large = jnp.minimum(large, nb - 1)
    return ret + jnp.where(n < max_exact, n, large)


def _split_hi_lo(x):
    hi = x.astype(jnp.bfloat16)
    lo = (x - hi.astype(jnp.float32)).astype(jnp.bfloat16)
    return hi, lo


def _in_proj_kernel(x_ref, w1_ref, win_ref, qkw_ref, g_ref, o_ref):
    x = x_ref[...]
    ms = jnp.mean(x * x, axis=-1, keepdims=True)
    u = ((x * lax.rsqrt(ms + EPS)) * w1_ref[...]).astype(jnp.bfloat16)
    g = g_ref[...]
    for c in range(IN_COLS // COL_CHUNK):
        lo_c, hi_c = c * COL_CHUNK, (c + 1) * COL_CHUNK
        p = jnp.dot(u, win_ref[:, lo_c:hi_c], preferred_element_type=jnp.float32)
        if hi_c <= 2 * DIFF_HEADS * 2 * DIFF_QK_DIM:
            for j in range(COL_CHUNK // LANES):
                sl = slice(j * LANES, (j + 1) * LANES)
                pj = p[:, sl]
                hi, lo = _split_hi_lo(pj * pj)
                msq = (jnp.dot(hi, g, preferred_element_type=jnp.float32)
                       + jnp.dot(lo, g, preferred_element_type=jnp.float32))
                w = qkw_ref[:, lo_c + j * LANES: lo_c + (j + 1) * LANES]
                o_ref[:, lo_c + j * LANES: lo_c + (j + 1) * LANES] = (
                    (pj * lax.rsqrt(msq + EPS)) * w).astype(jnp.bfloat16)
        else:
            if lo_c == QS_BLK * LANES:
                p = p * (SB_DIM ** -0.5)
            o_ref[:, lo_c:hi_c] = p.astype(jnp.bfloat16)


def _in_proj(x2d, norm1_w, w_in_bf, qkw, gmat):
    n = x2d.shape[0]
    const = lambda i: (0, 0)
    return pl.pallas_call(
        _in_proj_kernel,
        grid=(n // TM_PROJ,),
        in_specs=[
            pl.BlockSpec((TM_PROJ, D_MODEL), lambda i: (i, 0)),
            pl.BlockSpec((1, D_MODEL), const),
            pl.BlockSpec((D_MODEL, IN_COLS), const, pipeline_mode=pl.Buffered(1)),
            pl.BlockSpec((1, 2 * DIFF_HEADS * 2 * DIFF_QK_DIM), const),
            pl.BlockSpec((LANES, LANES), const),
        ],
        out_specs=pl.BlockSpec((TM_PROJ, IN_COLS), lambda i: (i, 0)),
        out_shape=jax.ShapeDtypeStruct((n, IN_COLS), jnp.bfloat16),
        compiler_params=pltpu.CompilerParams(
            dimension_semantics=("arbitrary",), vmem_limit_bytes=VMEM_LIMIT),
        name="in_proj",
    )(x2d, norm1_w, w_in_bf, qkw, gmat)


def _stack_subheads(q):
    lane = lax.broadcasted_iota(jnp.int32, q.shape, 1)
    zero = jnp.zeros_like(q)
    return jnp.concatenate([jnp.where(lane < SB_DIM, q, zero),
                            jnp.where(lane >= SB_DIM, q, zero)], axis=0)


def _diff_attn_kernel(cfar_ref, lam_ref, q_ref, k_ref, v_ref, bdiag_ref, bsub_ref, w_ref, o_ref,
                      vext_sc, m_sc, acc_sc):
    h = pl.program_id(1)
    qi = pl.program_id(2)

    @pl.when(qi == 0)
    def _build_vext():
        vext_sc[:, :LANES] = v_ref[0]
        vext_sc[:, LANES:] = jnp.ones((vext_sc.shape[0], LANES), vext_sc.dtype)

    q = q_ref[0]
    lane = lax.broadcasted_iota(jnp.int32, q.shape, 1)
    zero = jnp.zeros_like(q)
    q_sub = (jnp.where(lane < DIFF_QK_DIM, q, zero), jnp.where(lane >= DIFF_QK_DIM, q, zero))

    def tile_update(ks, bias, first):
        k = k_ref[0, pl.ds(ks, TKD), :]
        vext = vext_sc[pl.ds(ks, TKD), :]
        for c in range(2):
            s = lax.dot_general(q_sub[c], k, NT_DIMS, preferred_element_type=jnp.float32)
            if first:
                s = s + bias
                m_row = jnp.max(s, axis=-1, keepdims=True)
                p = jnp.exp(s - m_row)
                acc_sc[c] = jnp.dot(p.astype(jnp.bfloat16), vext, preferred_element_type=jnp.float32)
                m_sc[c] = jnp.broadcast_to(m_row, (TQD, LANES))
                continue
            m_old = m_sc[c]
            if bias.ndim == 0:
                m_new = jnp.maximum(m_old, jnp.max(s, axis=-1, keepdims=True) + bias)
                shift = m_new - bias
            else:
                s = s + bias
                m_new = jnp.maximum(m_old, jnp.max(s, axis=-1, keepdims=True))
                shift = m_new
            alpha = jnp.exp(m_old - m_new)
            p = jnp.exp(s - pltpu.repeat(shift, TKD // LANES, axis=1))
            acc_sc[c] = (pltpu.repeat(alpha, 2, axis=1) * acc_sc[c]
                         + jnp.dot(p.astype(jnp.bfloat16), vext, preferred_element_type=jnp.float32))
            m_sc[c] = m_new

    q0 = pl.multiple_of(qi * TQD, TQD)
    tile_update(q0, bdiag_ref[0, 0], True)

    @pl.when(qi >= 1)
    def _sub_diagonal():
        tile_update(pl.multiple_of(q0 - TQD, TQD), bsub_ref[0, 0], False)

    cfar = cfar_ref[h]

    def far_step(i, carry):
        tile_update(pl.multiple_of(i * TKD, TKD), cfar, False)
        return carry

    lax.fori_loop(0, jnp.maximum(qi - 1, 0), far_step, 0)

    acc0 = acc_sc[0]
    acc1 = acc_sc[1]
    o = (acc0[:, :LANES] / acc0[:, LANES:]
         - lam_ref[0] * (acc1[:, :LANES] / acc1[:, LANES:]))
    ms = jnp.mean(o * o, axis=-1, keepdims=True)
    y = ((o * lax.rsqrt(ms + EPS)) * w_ref[...]) * (1.0 - LAMBDA_INIT)
    o_ref[0] = y.astype(o_ref.dtype)


def _diff_attn(proj, bias_tiles, cfar, lam, w_out_norm):
    b, s, _ = proj.shape
    nq = s // TQD
    smem = pl.BlockSpec(memory_space=pltpu.SMEM)
    return pl.pallas_call(
        _diff_attn_kernel,
        grid=(b, DIFF_HEADS, nq),
        in_specs=[
            smem, smem,
            pl.BlockSpec((1, TQD, LANES), lambda bi, h, qi: (bi, qi, QD_BLK + h)),
            pl.BlockSpec((1, s, LANES), lambda bi, h, qi: (bi, 0, KD_BLK + h)),
            pl.BlockSpec((1, s, LANES), lambda bi, h, qi: (bi, 0, VD_BLK + h)),
            pl.BlockSpec((1, 1, TQD, TKD), lambda bi, h, qi: (0, h, 0, 0)),
            pl.BlockSpec((1, 1, TQD, TKD), lambda bi, h, qi: (1, h, 0, 0)),
            pl.BlockSpec((1, DIFF_V_DIM), lambda bi, h, qi: (0, 0)),
        ],
        out_specs=pl.BlockSpec((1, TQD, LANES), lambda bi, h, qi: (bi, qi, h)),
        out_shape=jax.ShapeDtypeStruct((b, s, DIFF_HEADS * DIFF_V_DIM), jnp.bfloat16),
        scratch_shapes=[
            pltpu.VMEM((s, 2 * LANES), jnp.bfloat16),
            pltpu.VMEM((2, TQD, LANES), jnp.float32),
            pltpu.VMEM((2, TQD, 2 * LANES), jnp.float32),
        ],
        compiler_params=pltpu.CompilerParams(
            dimension_semantics=("arbitrary", "arbitrary", "arbitrary"),
            vmem_limit_bytes=VMEM_LIMIT),
        name="diff_attn",
    )(cfar, lam, proj, proj, proj, bias_tiles, bias_tiles, w_out_norm)


def _softplus(z):
    return jnp.maximum(z, 0.0) + jnp.log(1.0 + jnp.exp(-jnp.abs(z)))


def _sb_attn_kernel(q_ref, k_ref, v_ref, tri_ref, w_ref, o_ref, r_sc, acc_sc):
    qi = pl.program_id(2)
    qq = _stack_subheads(q_ref[0])
    tri = tri_ref[...]

    def cost_suffix(c):
        hi, lo = _split_hi_lo(c)
        return (jnp.dot(hi, tri, preferred_element_type=jnp.float32)
                + jnp.dot(lo, tri, preferred_element_type=jnp.float32))

    ks = pl.multiple_of(qi * TQ, TQ)
    k = k_ref[0, pl.ds(ks, TK), :]
    v = v_ref[0, pl.ds(ks, TK), :]
    z = lax.dot_general(qq, k, NT_DIMS, preferred_element_type=jnp.float32)
    row = lax.broadcasted_iota(jnp.int32, z.shape, 0)
    row = jnp.where(row >= TQ, row - TQ, row)
    col = lax.broadcasted_iota(jnp.int32, z.shape, 1)
    mask = col < row
    sp = _softplus(z)
    c = jnp.where(mask, sp, 0.0)
    crem = cost_suffix(c)
    a = jnp.where(mask, jnp.exp(z - sp - crem), 0.0)
    acc_sc[...] = jnp.dot(a.astype(jnp.bfloat16), v, preferred_element_type=jnp.float32)
    r_sc[...] = crem[:, 0:1] + c[:, 0:1]

    def more(carry):
        i, r_min = carry
        return jnp.logical_and(i < qi, r_min < SB_STOP_COST)

    def step(carry):
        i, _ = carry
        ksi = pl.multiple_of((qi - 1 - i) * TK, TK)
        kf = k_ref[0, pl.ds(ksi, TK), :]
        vf = v_ref[0, pl.ds(ksi, TK), :]
        zf = lax.dot_general(qq, kf, NT_DIMS, preferred_element_type=jnp.float32)
        spf = _softplus(zf)
        cremf = cost_suffix(spf)
        r = r_sc[...]
        af = jnp.exp(zf - spf - cremf - r)
        acc_sc[...] += jnp.dot(af.astype(jnp.bfloat16), vf, preferred_element_type=jnp.float32)
        r_new = r + cremf[:, 0:1] + spf[:, 0:1]
        r_sc[...] = r_new
        return i + 1, jnp.min(r_new)

    lax.while_loop(more, step, (jnp.int32(0), jnp.float32(0.0)))

    acc = acc_sc[...]
    lane = lax.broadcasted_iota(jnp.int32, (TQ, LANES), 1)
    first = lane < SB_DIM
    o = jnp.where(first, acc[:TQ], acc[TQ:])
    o2 = o * o
    ss_a = jnp.sum(jnp.where(first, o2, 0.0), axis=-1, keepdims=True)
    ss_b = jnp.sum(jnp.where(first, 0.0, o2), axis=-1, keepdims=True)
    ms = jnp.where(first, ss_a, ss_b) * (1.0 / SB_DIM)
    o_ref[0] = ((o * lax.rsqrt(ms + EPS)) * w_ref[...]).astype(o_ref.dtype)


def _sb_attn(proj, tri, w_out_norm):
    b, s, _ = proj.shape
    nq = s // TQ
    npair = SB_HEADS // 2
    return pl.pallas_call(
        _sb_attn_kernel,
        grid=(b, npair, nq),
        in_specs=[
            pl.BlockSpec((1, TQ, LANES), lambda bi, j, qi: (bi, qi, QS_BLK + j)),
            pl.BlockSpec((1, s, LANES), lambda bi, j, qi: (bi, 0, KS_BLK + j)),
            pl.BlockSpec((1, s, LANES), lambda bi, j, qi: (bi, 0, VS_BLK + j)),
            pl.BlockSpec((TK, TK), lambda bi, j, qi: (0, 0)),
            pl.BlockSpec((1, LANES), lambda bi, j, qi: (0, 0)),
        ],
        out_specs=pl.BlockSpec((1, TQ, LANES), lambda bi, j, qi: (bi, qi, j)),
        out_shape=jax.ShapeDtypeStruct((b, s, SB_HEADS * SB_DIM), jnp.bfloat16),
        scratch_shapes=[
            pltpu.VMEM((2 * TQ, 1), jnp.float32),
            pltpu.VMEM((2 * TQ, LANES), jnp.float32),
        ],
        compiler_params=pltpu.CompilerParams(
            dimension_semantics=("arbitrary", "arbitrary", "arbitrary"),
            vmem_limit_bytes=VMEM_LIMIT),
        name="sb_attn",
    )(proj, proj, proj, tri, w_out_norm)


def _out_ffn_kernel(x_ref, yd_ref, ys_ref, woa_ref, wob_ref, w2_ref, wg_ref, wu_ref, wd_ref,
                    o_ref):
    h1 = (x_ref[...]
          + jnp.dot(yd_ref[...], woa_ref[...], preferred_element_type=jnp.float32)
          + jnp.dot(ys_ref[...], wob_ref[...], preferred_element_type=jnp.float32))
    ms = jnp.mean(h1 * h1, axis=-1, keepdims=True)
    u2 = ((h1 * lax.rsqrt(ms + EPS)) * w2_ref[...]).astype(jnp.bfloat16)
    ffn = None
    for c in range(D_FF // FF_CHUNK):
        sl = slice(c * FF_CHUNK, (c + 1) * FF_CHUNK)
        gate = jnp.dot(u2, wg_ref[:, sl], preferred_element_type=jnp.float32)
        up = jnp.dot(u2, wu_ref[:, sl], preferred_element_type=jnp.float32)
        act = (gate * (1.0 / (1.0 + jnp.exp(-gate))) * up).astype(jnp.bfloat16)
        part = jnp.dot(act, wd_ref[sl, :], preferred_element_type=jnp.float32)
        ffn = part if ffn is None else ffn + part
    o_ref[...] = h1 + ffn


def _out_ffn(x2d, yd, ys, woa, wob, norm2_w, wg, wu, wd):
    n = x2d.shape[0]
    const = lambda i: (0, 0)
    half = DIFF_HEADS * DIFF_V_DIM
    resident = partial(pl.BlockSpec, index_map=const, pipeline_mode=pl.Buffered(1))
    return pl.pallas_call(
        _out_ffn_kernel,
        grid=(n // TM_PROJ,),
        in_specs=[
            pl.BlockSpec((TM_PROJ, D_MODEL), lambda i: (i, 0)),
            pl.BlockSpec((TM_PROJ, half), lambda i: (i, 0)),
            pl.BlockSpec((TM_PROJ, half), lambda i: (i, 0)),
            resident((half, D_MODEL)),
            resident((half, D_MODEL)),
            pl.BlockSpec((1, D_MODEL), const),
            resident((D_MODEL, D_FF)),
            resident((D_MODEL, D_FF)),
            resident((D_FF, D_MODEL)),
        ],
        out_specs=pl.BlockSpec((TM_PROJ, D_MODEL), lambda i: (i, 0)),
        out_shape=jax.ShapeDtypeStruct((n, D_MODEL), jnp.float32),
        compiler_params=pltpu.CompilerParams(
            dimension_semantics=("arbitrary",), vmem_limit_bytes=VMEM_LIMIT),
        name="out_ffn",
    )(x2d, yd, ys, woa, wob, norm2_w, wg, wu, wd)


def _diff_bias_tiles(rel_bias):
    qq = jnp.arange(TQD, dtype=jnp.int32)[:, None]
    kk = jnp.arange(TKD, dtype=jnp.int32)[None, :]
    width = TQD + TKD + 1
    tiles = []
    for koff in (0, -TKD):
        kpos = kk + koff
        table = rel_bias.astype(jnp.float32)[
            _t5_bucket(jnp.arange(width, dtype=jnp.int32) - TQD + koff)].T
        toep = jnp.tile(table, (1, TQD))[:, :TQD * (width - 1)].reshape(DIFF_HEADS, TQD, width - 1)
        bias = toep[:, :, TQD:TQD + TKD]
        allowed = jnp.floor_divide(kpos, CHUNK) <= jnp.floor_divide(qq, CHUNK)
        tiles.append(jnp.where(allowed[None], bias, MASK_VALUE))
    return jnp.stack(tiles, axis=0)


def kernel(x, norm1_w, w_in, q_norm_w, k_norm_w, lambda_q1, lambda_k1, lambda_q2, lambda_k2,
           diff_out_norm_w, sb_out_norm_w, w_out, norm2_w, w_gate, w_up, w_down, rel_bias):
    b, s, d = x.shape
    assert (b, s, d) == (8, 4096, D_MODEL) and w_in.shape[0] == 1
    f32, bf16 = jnp.float32, jnp.bfloat16
    x2d = x.reshape(b * s, d)

    qkw = jnp.concatenate([jnp.tile(q_norm_w[0].astype(f32), 2 * DIFF_HEADS) * (DIFF_QK_DIM ** -0.5),
                           jnp.tile(k_norm_w[0].astype(f32), 2 * DIFF_HEADS)])[None, :]
    grp = jnp.arange(LANES) // DIFF_QK_DIM
    gmat = jnp.where(grp[:, None] == grp[None, :], 1.0 / DIFF_QK_DIM, 0.0).astype(bf16)
    lam = (jnp.exp(jnp.sum(lambda_q1[0].astype(f32) * lambda_k1[0].astype(f32)))
           - jnp.exp(jnp.sum(lambda_q2[0].astype(f32) * lambda_k2[0].astype(f32)))
           + LAMBDA_INIT).reshape(1)
    bias_tiles = _diff_bias_tiles(rel_bias)
    cfar = rel_bias.astype(f32)[_t5_bucket(jnp.int32(-(TKD + 1)))]
    jj = jnp.arange(TK)
    tri = (jj[:, None] > jj[None, :]).astype(bf16)

    proj = _in_proj(x2d, norm1_w[0].astype(f32)[None, :], w_in[0].astype(bf16), qkw, gmat)
    proj = proj.reshape(b, s, IN_COLS)
    y_diff = _diff_attn(proj, bias_tiles, cfar, lam, diff_out_norm_w[0].astype(f32)[None, :])
    y_sb = _sb_attn(proj, tri, jnp.tile(sb_out_norm_w[0].astype(f32), 2)[None, :])

    half = DIFF_HEADS * DIFF_V_DIM
    wo = w_out[0].astype(bf16)
    out = _out_ffn(x2d, y_diff.reshape(b * s, half), y_sb.reshape(b * s, half),
                   wo[:half], wo[half:], norm2_w[0].astype(f32)[None, :],
                   w_gate[0].astype(bf16), w_up[0].astype(bf16), w_down[0].astype(bf16))
    return out.reshape(b, s, d)
```

```python
import math
from functools import partial

import jax
import jax.numpy as jnp
from jax import lax
from jax.experimental import pallas as pl
from jax.experimental.pallas import tpu as pltpu

D_MODEL = 1024
CHUNK = 64
DIFF_QK_DIM = 64
DIFF_V_DIM = 128
DIFF_HEADS = 4
SB_DIM = 64
SB_HEADS = 8
NUM_BUCKETS = 32
MAX_DISTANCE = 128
D_FF = 2816
EPS = 1e-6
IN_COLS = 3072
LAMBDA_INIT = 0.8 - 0.6 * math.exp(-0.3 * 0)
LANES = 128

QD_BLK, KD_BLK, VD_BLK = 0, 4, 8
QS_BLK, KS_BLK, VS_BLK = 12, 16, 20

TM_PROJ = 512
COL_CHUNK = 512
TQD = 512
TKD = 512
HEADS_PER_STEP = 2
LOG2E = math.log2(math.e)
TQ = 256
TK = 256
FF_CHUNK = 256
MASK_VALUE = -1e30
SB_STOP_COST = 104.0
VMEM_LIMIT = 56 * 1024 * 1024

NT_DIMS = (((1,), (1,)), ((), ()))


def _t5_bucket(rel):
    nb = NUM_BUCKETS // 2
    max_exact = nb // 2
    ret = (rel > 0).astype(jnp.int32) * nb
    n = jnp.abs(rel)
    nf = jnp.maximum(n, 1).astype(jnp.float32)
    large = max_exact + (jnp.log(nf / max_exact) / math.log(MAX_DISTANCE / max_exact)
                         * (nb - max_exact)).astype(jnp.int32)
    large = jnp.minimum(large, nb - 1)
    return ret + jnp.where(n < max_exact, n, large)


def _split_hi_lo(x):
    hi = x.astype(jnp.bfloat16)
    lo = (x - hi.astype(jnp.float32)).astype(jnp.bfloat16)
    return hi, lo


def _in_proj_kernel(x_ref, w1_ref, win_ref, qkw_ref, g_ref, o_ref):
    x = x_ref[...]
    ms = jnp.mean(x * x, axis=-1, keepdims=True)
    u = ((x * lax.rsqrt(ms + EPS)) * w1_ref[...]).astype(jnp.bfloat16)
    g = g_ref[...]
    for c in range(IN_COLS // COL_CHUNK):
        lo_c, hi_c = c * COL_CHUNK, (c + 1) * COL_CHUNK
        p = jnp.dot(u, win_ref[:, lo_c:hi_c], preferred_element_type=jnp.float32)
        if hi_c <= 2 * DIFF_HEADS * 2 * DIFF_QK_DIM:
            for j in range(COL_CHUNK // LANES):
                sl = slice(j * LANES, (j + 1) * LANES)
                pj = p[:, sl]
                hi, lo = _split_hi_lo(pj * pj)
                msq = (jnp.dot(hi, g, preferred_element_type=jnp.float32)
                       + jnp.dot(lo, g, preferred_element_type=jnp.float32))
                w = qkw_ref[:, lo_c + j * LANES: lo_c + (j + 1) * LANES]
                o_ref[:, lo_c + j * LANES: lo_c + (j + 1) * LANES] = (
                    (pj * lax.rsqrt(msq + EPS)) * w).astype(jnp.bfloat16)
        else:
            if lo_c == QS_BLK * LANES:
                p = p * (SB_DIM ** -0.5)
            o_ref[:, lo_c:hi_c] = p.astype(jnp.bfloat16)


def _in_proj(x2d, norm1_w, w_in_bf, qkw, gmat):
    n = x2d.shape[0]
    const = lambda i: (0, 0)
    return pl.pallas_call(
        _in_proj_kernel,
        grid=(n // TM_PROJ,),
        in_specs=[
            pl.BlockSpec((TM_PROJ, D_MODEL), lambda i: (i, 0)),
            pl.BlockSpec((1, D_MODEL), const),
            pl.BlockSpec((D_MODEL, IN_COLS), const, pipeline_mode=pl.Buffered(1)),
            pl.BlockSpec((1, 2 * DIFF_HEADS * 2 * DIFF_QK_DIM), const),
            pl.BlockSpec((LANES, LANES), const),
        ],
        out_specs=pl.BlockSpec((TM_PROJ, IN_COLS), lambda i: (i, 0)),
        out_shape=jax.ShapeDtypeStruct((n, IN_COLS), jnp.bfloat16),
        compiler_params=pltpu.CompilerParams(
            dimension_semantics=("arbitrary",), vmem_limit_bytes=VMEM_LIMIT),
        name="in_proj",
    )(x2d, norm1_w, w_in_bf, qkw, gmat)


def _stack_subheads(q):
    lane = lax.broadcasted_iota(jnp.int32, q.shape, 1)
    zero = jnp.zeros_like(q)
    return jnp.concatenate([jnp.where(lane < SB_DIM, q, zero),
                            jnp.where(lane >= SB_DIM, q, zero)], axis=0)


def _diff_attn_kernel(cfar_ref, lam_ref, q_ref, k_ref, v_ref, bdiag_ref, bsub_ref, w_ref, o_ref,
                      vext_sc, m_sc, acc_sc):
    hp = pl.program_id(1)
    qi = pl.program_id(2)

    @pl.when(qi == 0)
    def _build_vext():
        ones = jnp.ones((vext_sc.shape[1], LANES), vext_sc.dtype)
        for hh in range(HEADS_PER_STEP):
            vext_sc[hh, :, :LANES] = v_ref[0, :, hh * LANES:(hh + 1) * LANES]
            vext_sc[hh, :, LANES:] = ones

    lane = lax.broadcasted_iota(jnp.int32, (TQD, LANES), 1)
    zero = jnp.zeros((TQD, LANES), q_ref.dtype)
    q_sub = []
    for hh in range(HEADS_PER_STEP):
        q = q_ref[0, :, hh * LANES:(hh + 1) * LANES]
        q_sub.append(jnp.where(lane < DIFF_QK_DIM, q, zero))
        q_sub.append(jnp.where(lane >= DIFF_QK_DIM, q, zero))

    def tile_update(ks, bias_ref, cfar, first):
        for hh in range(HEADS_PER_STEP):
            k = k_ref[0, pl.ds(ks, TKD), hh * LANES:(hh + 1) * LANES]
            vext = vext_sc[hh, pl.ds(ks, TKD), :]
            for c in range(2):
                idx = 2 * hh + c
                s = lax.dot_general(q_sub[idx], k, NT_DIMS, preferred_element_type=jnp.float32)
                if first:
                    s = s + bias_ref[0, hh]
                    m_row = jnp.max(s, axis=-1, keepdims=True)
                    p = jnp.exp2(s - m_row)
                    acc_sc[idx] = jnp.dot(p.astype(jnp.bfloat16), vext,
                                          preferred_element_type=jnp.float32)
                    m_sc[idx] = jnp.broadcast_to(m_row, (TQD, LANES))
                    continue
                m_old = m_sc[idx]
                if bias_ref is None:
                    m_new = jnp.maximum(m_old, jnp.max(s, axis=-1, keepdims=True) + cfar[hh])
                    shift = m_new - cfar[hh]
                else:
                    s = s + bias_ref[0, hh]
                    m_new = jnp.maximum(m_old, jnp.max(s, axis=-1, keepdims=True))
                    shift = m_new
                alpha = jnp.exp2(m_old - m_new)
                p = jnp.exp2(s - jnp.concatenate([shift] * (TKD // LANES), axis=1))
                acc_sc[idx] = (jnp.concatenate([alpha, alpha], axis=1) * acc_sc[idx]
                               + jnp.dot(p.astype(jnp.bfloat16), vext,
                                         preferred_element_type=jnp.float32))
                m_sc[idx] = m_new

    q0 = pl.multiple_of(qi * TQD, TQD)
    tile_update(q0, bdiag_ref, None, True)

    @pl.when(qi >= 1)
    def _sub_diagonal():
        tile_update(pl.multiple_of(q0 - TQD, TQD), bsub_ref, None, False)

    cfar = [cfar_ref[HEADS_PER_STEP * hp + hh] for hh in range(HEADS_PER_STEP)]

    def far_step(i, carry):
        tile_update(pl.multiple_of(i * TKD, TKD), None, cfar, False)
        return carry

    lax.fori_loop(0, jnp.maximum(qi - 1, 0), far_step, 0)

    for hh in range(HEADS_PER_STEP):
        acc0 = acc_sc[2 * hh]
        acc1 = acc_sc[2 * hh + 1]
        o = (acc0[:, :LANES] / acc0[:, LANES:]
             - lam_ref[0] * (acc1[:, :LANES] / acc1[:, LANES:]))
        ms = jnp.mean(o * o, axis=-1, keepdims=True)
        y = ((o * lax.rsqrt(ms + EPS)) * w_ref[...]) * (1.0 - LAMBDA_INIT)
        o_ref[0, :, hh * LANES:(hh + 1) * LANES] = y.astype(o_ref.dtype)


def _diff_attn(proj, bias_tiles, cfar, lam, w_out_norm):
    b, s, _ = proj.shape
    nq = s // TQD
    hps = HEADS_PER_STEP
    wide = hps * LANES
    smem = pl.BlockSpec(memory_space=pltpu.SMEM)
    return pl.pallas_call(
        _diff_attn_kernel,
        grid=(b, DIFF_HEADS // hps, nq),
        in_specs=[
            smem, smem,
            pl.BlockSpec((1, TQD, wide), lambda bi, hp, qi: (bi, qi, QD_BLK // hps + hp)),
            pl.BlockSpec((1, s, wide), lambda bi, hp, qi: (bi, 0, KD_BLK // hps + hp)),
            pl.BlockSpec((1, s, wide), lambda bi, hp, qi: (bi, 0, VD_BLK // hps + hp)),
            pl.BlockSpec((1, hps, TQD, TKD), lambda bi, hp, qi: (0, hp, 0, 0)),
            pl.BlockSpec((1, hps, TQD, TKD), lambda bi, hp, qi: (1, hp, 0, 0)),
            pl.BlockSpec((1, DIFF_V_DIM), lambda bi, hp, qi: (0, 0)),
        ],
        out_specs=pl.BlockSpec((1, TQD, wide), lambda bi, hp, qi: (bi, qi, hp)),
        out_shape=jax.ShapeDtypeStruct((b, s, DIFF_HEADS * DIFF_V_DIM), jnp.bfloat16),
        scratch_shapes=[
            pltpu.VMEM((hps, s, 2 * LANES), jnp.bfloat16),
            pltpu.VMEM((2 * hps, TQD, LANES), jnp.float32),
            pltpu.VMEM((2 * hps, TQD, 2 * LANES), jnp.float32),
        ],
        compiler_params=pltpu.CompilerParams(
            dimension_semantics=("arbitrary", "arbitrary", "arbitrary"),
            vmem_limit_bytes=VMEM_LIMIT),
        name="diff_attn",
    )(cfar, lam, proj, proj, proj, bias_tiles, bias_tiles, w_out_norm)


def _softplus(z):
    return jnp.maximum(z, 0.0) + jnp.log(1.0 + jnp.exp(-jnp.abs(z)))


def _sb_attn_kernel(q_ref, k_ref, v_ref, tri_ref, w_ref, o_ref, r_sc, acc_sc):
    qi = pl.program_id(2)
    qq = _stack_subheads(q_ref[0])
    tri = tri_ref[...]

    def cost_suffix(c):
        hi, lo = _split_hi_lo(c)
        return (jnp.dot(hi, tri, preferred_element_type=jnp.float32)
                + jnp.dot(lo, tri, preferred_element_type=jnp.float32))

    ks = pl.multiple_of(qi * TQ, TQ)
    k = k_ref[0, pl.ds(ks, TK), :]
    v = v_ref[0, pl.ds(ks, TK), :]
    z = lax.dot_general(qq, k, NT_DIMS, preferred_element_type=jnp.float32)
    row = lax.broadcasted_iota(jnp.int32, z.shape, 0)
    row = jnp.where(row >= TQ, row - TQ, row)
    col = lax.broadcasted_iota(jnp.int32, z.shape, 1)
    mask = col < row
    sp = _softplus(z)
    c = jnp.where(mask, sp, 0.0)
    crem = cost_suffix(c)
    a = jnp.where(mask, jnp.exp(z - sp - crem), 0.0)
    acc_sc[...] = jnp.dot(a.astype(jnp.bfloat16), v, preferred_element_type=jnp.float32)
    r_sc[...] = crem[:, 0:1] + c[:, 0:1]

    def more(carry):
        i, r_min = carry
        return jnp.logical_and(i < qi, r_min < SB_STOP_COST)

    def step(carry):
        i, _ = carry
        ksi = pl.multiple_of((qi - 1 - i) * TK, TK)
        kf = k_ref[0, pl.ds(ksi, TK), :]
        vf = v_ref[0, pl.ds(ksi, TK), :]
        zf = lax.dot_general(qq, kf, NT_DIMS, preferred_element_type=jnp.float32)
        spf = _softplus(zf)
        cremf = cost_suffix(spf)
        r = r_sc[...]
        af = jnp.exp(zf - spf - cremf - r)
        acc_sc[...] += jnp.dot(af.astype(jnp.bfloat16), vf, preferred_element_type=jnp.float32)
        r_new = r + cremf[:, 0:1] + spf[:, 0:1]
        r_sc[...] = r_new
        return i + 1, jnp.min(r_new)

    lax.while_loop(more, step, (jnp.int32(0), jnp.float32(0.0)))

    acc = acc_sc[...]
    lane = lax.broadcasted_iota(jnp.int32, (TQ, LANES), 1)
    first = lane < SB_DIM
    o = jnp.where(first, acc[:TQ], acc[TQ:])
    o2 = o * o
    ss_a = jnp.sum(jnp.where(first, o2, 0.0), axis=-1, keepdims=True)
    ss_b = jnp.sum(jnp.where(first, 0.0, o2), axis=-1, keepdims=True)
    ms = jnp.where(first, ss_a, ss_b) * (1.0 / SB_DIM)
    o_ref[0] = ((o * lax.rsqrt(ms + EPS)) * w_ref[...]).astype(o_ref.dtype)


def _sb_attn(proj, tri, w_out_norm):
    b, s, _ = proj.shape
    nq = s // TQ
    npair = SB_HEADS // 2
    return pl.pallas_call(
        _sb_attn_kernel,
        grid=(b, npair, nq),
        in_specs=[
            pl.BlockSpec((1, TQ, LANES), lambda bi, j, qi: (bi, qi, QS_BLK + j)),
            pl.BlockSpec((1, s, LANES), lambda bi, j, qi: (bi, 0, KS_BLK + j)),
            pl.BlockSpec((1, s, LANES), lambda bi, j, qi: (bi, 0, VS_BLK + j)),
            pl.BlockSpec((TK, TK), lambda bi, j, qi: (0, 0)),
            pl.BlockSpec((1, LANES), lambda bi, j, qi: (0, 0)),
        ],
        out_specs=pl.BlockSpec((1, TQ, LANES), lambda bi, j, qi: (bi, qi, j)),
        out_shape=jax.ShapeDtypeStruct((b, s, SB_HEADS * SB_DIM), jnp.bfloat16),
        scratch_shapes=[
            pltpu.VMEM((2 * TQ, 1), jnp.float32),
            pltpu.VMEM((2 * TQ, LANES), jnp.float32),
        ],
        compiler_params=pltpu.CompilerParams(
            dimension_semantics=("arbitrary", "arbitrary", "arbitrary"),
            vmem_limit_bytes=VMEM_LIMIT),
        name="sb_attn",
    )(proj, proj, proj, tri, w_out_norm)


def _out_ffn_kernel(x_ref, yd_ref, ys_ref, woa_ref, wob_ref, w2_ref, wg_ref, wu_ref, wd_ref,
                    o_ref):
    h1 = (x_ref[...]
          + jnp.dot(yd_ref[...], woa_ref[...], preferred_element_type=jnp.float32)
          + jnp.dot(ys_ref[...], wob_ref[...], preferred_element_type=jnp.float32))
    ms = jnp.mean(h1 * h1, axis=-1, keepdims=True)
    u2 = ((h1 * lax.rsqrt(ms + EPS)) * w2_ref[...]).astype(jnp.bfloat16)
    ffn = None
    for c in range(D_FF // FF_CHUNK):
        sl = slice(c * FF_CHUNK, (c + 1) * FF_CHUNK)
        gate = jnp.dot(u2, wg_ref[:, sl], preferred_element_type=jnp.float32)
        up = jnp.dot(u2, wu_ref[:, sl], preferred_element_type=jnp.float32)
        act = (gate * (1.0 / (1.0 + jnp.exp(-gate))) * up).astype(jnp.bfloat16)
        part = jnp.dot(act, wd_ref[sl, :], preferred_element_type=jnp.float32)
        ffn = part if ffn is None else ffn + part
    o_ref[...] = h1 + ffn


def _out_ffn(x2d, yd, ys, woa, wob, norm2_w, wg, wu, wd):
    n = x2d.shape[0]
    const = lambda i: (0, 0)
    half = DIFF_HEADS * DIFF_V_DIM
    resident = partial(pl.BlockSpec, index_map=const, pipeline_mode=pl.Buffered(1))
    return pl.pallas_call(
        _out_ffn_kernel,
        grid=(n // TM_PROJ,),
        in_specs=[
            pl.BlockSpec((TM_PROJ, D_MODEL), lambda i: (i, 0)),
            pl.BlockSpec((TM_PROJ, half), lambda i: (i, 0)),
            pl.BlockSpec((TM_PROJ, half), lambda i: (i, 0)),
            resident((half, D_MODEL)),
            resident((half, D_MODEL)),
            pl.BlockSpec((1, D_MODEL), const),
            resident((D_MODEL, D_FF)),
            resident((D_MODEL, D_FF)),
            resident((D_FF, D_MODEL)),
        ],
        out_specs=pl.BlockSpec((TM_PROJ, D_MODEL), lambda i: (i, 0)),
        out_shape=jax.ShapeDtypeStruct((n, D_MODEL), jnp.float32),
        compiler_params=pltpu.CompilerParams(
            dimension_semantics=("arbitrary",), vmem_limit_bytes=VMEM_LIMIT),
        name="out_ffn",
    )(x2d, yd, ys, woa, wob, norm2_w, wg, wu, wd)


def _diff_bias_tiles(rel_bias):
    qq = jnp.arange(TQD, dtype=jnp.int32)[:, None]
    kk = jnp.arange(TKD, dtype=jnp.int32)[None, :]
    width = TQD + TKD + 1
    tiles = []
    for koff in (0, -TKD):
        kpos = kk + koff
        table = rel_bias.astype(jnp.float32)[
            _t5_bucket(jnp.arange(width, dtype=jnp.int32) - TQD + koff)].T
        toep = jnp.tile(table, (1, TQD))[:, :TQD * (width - 1)].reshape(DIFF_HEADS, TQD, width - 1)
        bias = toep[:, :, TQD:TQD + TKD]
        allowed = jnp.floor_divide(kpos, CHUNK) <= jnp.floor_divide(qq, CHUNK)
        tiles.append(jnp.where(allowed[None], bias, MASK_VALUE))
    return jnp.stack(tiles, axis=0)


def kernel(x, norm1_w, w_in, q_norm_w, k_norm_w, lambda_q1, lambda_k1, lambda_q2, lambda_k2,
           diff_out_norm_w, sb_out_norm_w, w_out, norm2_w, w_gate, w_up, w_down, rel_bias):
    b, s, d = x.shape
    assert (b, s, d) == (8, 4096, D_MODEL) and w_in.shape[0] == 1
    f32, bf16 = jnp.float32, jnp.bfloat16
    x2d = x.reshape(b * s, d)

    qkw = jnp.concatenate([jnp.tile(q_norm_w[0].astype(f32), 2 * DIFF_HEADS)
                           * (DIFF_QK_DIM ** -0.5 * LOG2E),
                           jnp.tile(k_norm_w[0].astype(f32), 2 * DIFF_HEADS)])[None, :]
    grp = jnp.arange(LANES) // DIFF_QK_DIM
    gmat = jnp.where(grp[:, None] == grp[None, :], 1.0 / DIFF_QK_DIM, 0.0).astype(bf16)
    lam = (jnp.exp(jnp.sum(lambda_q1[0].astype(f32) * lambda_k1[0].astype(f32)))
           - jnp.exp(jnp.sum(lambda_q2[0].astype(f32) * lambda_k2[0].astype(f32)))
           + LAMBDA_INIT).reshape(1)
    bias_tiles = _diff_bias_tiles(rel_bias) * LOG2E
    cfar = rel_bias.astype(f32)[_t5_bucket(jnp.int32(-(TKD + 1)))] * LOG2E
    jj = jnp.arange(TK)
    tri = (jj[:, None] > jj[None, :]).astype(bf16)

    proj = _in_proj(x2d, norm1_w[0].astype(f32)[None, :], w_in[0].astype(bf16), qkw, gmat)
    proj = proj.reshape(b, s, IN_COLS)
    y_diff = _diff_attn(proj, bias_tiles, cfar, lam, diff_out_norm_w[0].astype(f32)[None, :])
    y_sb = _sb_attn(proj, tri, jnp.tile(sb_out_norm_w[0].astype(f32), 2)[None, :])

    half = DIFF_HEADS * DIFF_V_DIM
    wo = w_out[0].astype(bf16)
    out = _out_ffn(x2d, y_diff.reshape(b * s, half), y_sb.reshape(b * s, half),
                   wo[:half], wo[half:], norm2_w[0].astype(f32)[None, :],
                   w_gate[0].astype(bf16), w_up[0].astype(bf16), w_down[0].astype(bf16))
    return out.reshape(b, s, d)
```

```python
import math
from functools import partial

import jax
import jax.numpy as jnp
from jax import lax
from jax.experimental import pallas as pl
from jax.experimental.pallas import tpu as pltpu

D_MODEL = 1024
CHUNK = 64
DIFF_QK_DIM = 64
DIFF_V_DIM = 128
DIFF_HEADS = 4
SB_DIM = 64
SB_HEADS = 8
NUM_BUCKETS = 32
MAX_DISTANCE = 128
D_FF = 2816
EPS = 1e-6
IN_COLS = 3072
LAMBDA_INIT = 0.8 - 0.6 * math.exp(-0.3 * 0)
LANES = 128

QD_BLK, KD_BLK, VD_BLK = 0, 4, 8
QS_BLK, KS_BLK, VS_BLK = 12, 16, 20

TM_PROJ = 512
COL_CHUNK = 512
TQD = 512
TKD = 512
HEADS_PER_STEP = 2
LOG2E = math.log2(math.e)
TQ = 256
TK = 256
SB_PAIRS = SB_HEADS // 2
FF_CHUNK = 256
MASK_VALUE = -1e30
SB_STOP_COST = 104.0
VMEM_LIMIT = 56 * 1024 * 1024

NT_DIMS = (((1,), (1,)), ((), ()))


def _t5_bucket(rel):
    nb = NUM_BUCKETS // 2
    max_exact = nb // 2
    ret = (rel > 0).astype(jnp.int32) * nb
    n = jnp.abs(rel)
    nf = jnp.maximum(n, 1).astype(jnp.float32)
    large = max_exact + (jnp.log(nf / max_exact) / math.log(MAX_DISTANCE / max_exact)
                         * (nb - max_exact)).astype(jnp.int32)
    large = jnp.minimum(large, nb - 1)
    return ret + jnp.where(n < max_exact, n, large)


def _split_hi_lo(x):
    hi = x.astype(jnp.bfloat16)
    lo = (x - hi.astype(jnp.float32)).astype(jnp.bfloat16)
    return hi, lo


def _in_proj_kernel(x_ref, w1_ref, win_ref, qkw_ref, g_ref, o_ref):
    x = x_ref[...]
    ms = jnp.mean(x * x, axis=-1, keepdims=True)
    u = ((x * lax.rsqrt(ms + EPS)) * w1_ref[...]).astype(jnp.bfloat16)
    g = g_ref[...]
    for c in range(IN_COLS // COL_CHUNK):
        lo_c, hi_c = c * COL_CHUNK, (c + 1) * COL_CHUNK
        p = jnp.dot(u, win_ref[:, lo_c:hi_c], preferred_element_type=jnp.float32)
        if hi_c <= 2 * DIFF_HEADS * 2 * DIFF_QK_DIM:
            for j in range(COL_CHUNK // LANES):
                sl = slice(j * LANES, (j + 1) * LANES)
                pj = p[:, sl]
                hi, lo = _split_hi_lo(pj * pj)
                msq = (jnp.dot(hi, g, preferred_element_type=jnp.float32)
                       + jnp.dot(lo, g, preferred_element_type=jnp.float32))
                w = qkw_ref[:, lo_c + j * LANES: lo_c + (j + 1) * LANES]
                o_ref[:, lo_c + j * LANES: lo_c + (j + 1) * LANES] = (
                    (pj * lax.rsqrt(msq + EPS)) * w).astype(jnp.bfloat16)
        else:
            if lo_c == QS_BLK * LANES:
                p = p * (SB_DIM ** -0.5)
            o_ref[:, lo_c:hi_c] = p.astype(jnp.bfloat16)


def _in_proj(x2d, norm1_w, w_in_bf, qkw, gmat):
    n = x2d.shape[0]
    const = lambda i: (0, 0)
    return pl.pallas_call(
        _in_proj_kernel,
        grid=(n // TM_PROJ,),
        in_specs=[
            pl.BlockSpec((TM_PROJ, D_MODEL), lambda i: (i, 0)),
            pl.BlockSpec((1, D_MODEL), const),
            pl.BlockSpec((D_MODEL, IN_COLS), const, pipeline_mode=pl.Buffered(1)),
            pl.BlockSpec((1, 2 * DIFF_HEADS * 2 * DIFF_QK_DIM), const),
            pl.BlockSpec((LANES, LANES), const),
        ],
        out_specs=pl.BlockSpec((TM_PROJ, IN_COLS), lambda i: (i, 0)),
        out_shape=jax.ShapeDtypeStruct((n, IN_COLS), jnp.bfloat16),
        compiler_params=pltpu.CompilerParams(
            dimension_semantics=("arbitrary",), vmem_limit_bytes=VMEM_LIMIT),
        name="in_proj",
    )(x2d, norm1_w, w_in_bf, qkw, gmat)


def _stack_subheads(q):
    lane = lax.broadcasted_iota(jnp.int32, q.shape, 1)
    zero = jnp.zeros_like(q)
    return jnp.concatenate([jnp.where(lane < SB_DIM, q, zero),
                            jnp.where(lane >= SB_DIM, q, zero)], axis=0)


def _diff_attn_kernel(cfar_ref, lam_ref, q_ref, k_ref, v_ref, bdiag_ref, bsub_ref, w_ref, o_ref,
                      vext_sc, m_sc, acc_sc):
    hp = pl.program_id(1)
    qi = pl.program_id(2)

    @pl.when(qi == 0)
    def _build_vext():
        ones = jnp.ones((vext_sc.shape[1], LANES), vext_sc.dtype)
        for hh in range(HEADS_PER_STEP):
            vext_sc[hh, :, :LANES] = v_ref[0, :, hh * LANES:(hh + 1) * LANES]
            vext_sc[hh, :, LANES:] = ones

    lane = lax.broadcasted_iota(jnp.int32, (TQD, LANES), 1)
    zero = jnp.zeros((TQD, LANES), q_ref.dtype)
    q_sub = []
    for hh in range(HEADS_PER_STEP):
        q = q_ref[0, :, hh * LANES:(hh + 1) * LANES]
        q_sub.append(jnp.where(lane < DIFF_QK_DIM, q, zero))
        q_sub.append(jnp.where(lane >= DIFF_QK_DIM, q, zero))

    def tile_update(ks, bias_ref, cfar, first):
        for hh in range(HEADS_PER_STEP):
            k = k_ref[0, pl.ds(ks, TKD), hh * LANES:(hh + 1) * LANES]
            vext = vext_sc[hh, pl.ds(ks, TKD), :]
            for c in range(2):
                idx = 2 * hh + c
                s = lax.dot_general(q_sub[idx], k, NT_DIMS, preferred_element_type=jnp.float32)
                if first:
                    s = s + bias_ref[0, hh]
                    m_row = jnp.max(s, axis=-1, keepdims=True)
                    p = jnp.exp2(s - m_row)
                    acc_sc[idx] = jnp.dot(p.astype(jnp.bfloat16), vext,
                                          preferred_element_type=jnp.float32)
                    m_sc[idx] = jnp.broadcast_to(m_row, (TQD, LANES))
                    continue
                m_old = m_sc[idx]
                if bias_ref is None:
                    m_new = jnp.maximum(m_old, jnp.max(s, axis=-1, keepdims=True) + cfar[hh])
                    shift = m_new - cfar[hh]
                else:
                    s = s + bias_ref[0, hh]
                    m_new = jnp.maximum(m_old, jnp.max(s, axis=-1, keepdims=True))
                    shift = m_new
                alpha = jnp.exp2(m_old - m_new)
                p = jnp.exp2(s - jnp.concatenate([shift] * (TKD // LANES), axis=1))
                acc_sc[idx] = (jnp.concatenate([alpha, alpha], axis=1) * acc_sc[idx]
                               + jnp.dot(p.astype(jnp.bfloat16), vext,
                                         preferred_element_type=jnp.float32))
                m_sc[idx] = m_new

    q0 = pl.multiple_of(qi * TQD, TQD)
    tile_update(q0, bdiag_ref, None, True)

    @pl.when(qi >= 1)
    def _sub_diagonal():
        tile_update(pl.multiple_of(q0 - TQD, TQD), bsub_ref, None, False)

    cfar = [cfar_ref[HEADS_PER_STEP * hp + hh] for hh in range(HEADS_PER_STEP)]

    def far_step(i, carry):
        tile_update(pl.multiple_of(i * TKD, TKD), None, cfar, False)
        return carry

    lax.fori_loop(0, jnp.maximum(qi - 1, 0), far_step, 0)

    for hh in range(HEADS_PER_STEP):
        acc0 = acc_sc[2 * hh]
        acc1 = acc_sc[2 * hh + 1]
        o = (acc0[:, :LANES] / acc0[:, LANES:]
             - lam_ref[0] * (acc1[:, :LANES] / acc1[:, LANES:]))
        ms = jnp.mean(o * o, axis=-1, keepdims=True)
        y = ((o * lax.rsqrt(ms + EPS)) * w_ref[...]) * (1.0 - LAMBDA_INIT)
        o_ref[0, :, hh * LANES:(hh + 1) * LANES] = y.astype(o_ref.dtype)


def _diff_attn(proj, bias_tiles, cfar, lam, w_out_norm):
    b, s, _ = proj.shape
    nq = s // TQD
    hps = HEADS_PER_STEP
    wide = hps * LANES
    smem = pl.BlockSpec(memory_space=pltpu.SMEM)
    return pl.pallas_call(
        _diff_attn_kernel,
        grid=(b, DIFF_HEADS // hps, nq),
        in_specs=[
            smem, smem,
            pl.BlockSpec((1, TQD, wide), lambda bi, hp, qi: (bi, qi, QD_BLK // hps + hp)),
            pl.BlockSpec((1, s, wide), lambda bi, hp, qi: (bi, 0, KD_BLK // hps + hp)),
            pl.BlockSpec((1, s, wide), lambda bi, hp, qi: (bi, 0, VD_BLK // hps + hp)),
            pl.BlockSpec((1, hps, TQD, TKD), lambda bi, hp, qi: (0, hp, 0, 0)),
            pl.BlockSpec((1, hps, TQD, TKD), lambda bi, hp, qi: (1, hp, 0, 0)),
            pl.BlockSpec((1, DIFF_V_DIM), lambda bi, hp, qi: (0, 0)),
        ],
        out_specs=pl.BlockSpec((1, TQD, wide), lambda bi, hp, qi: (bi, qi, hp)),
        out_shape=jax.ShapeDtypeStruct((b, s, DIFF_HEADS * DIFF_V_DIM), jnp.bfloat16),
        scratch_shapes=[
            pltpu.VMEM((hps, s, 2 * LANES), jnp.bfloat16),
            pltpu.VMEM((2 * hps, TQD, LANES), jnp.float32),
            pltpu.VMEM((2 * hps, TQD, 2 * LANES), jnp.float32),
        ],
        compiler_params=pltpu.CompilerParams(
            dimension_semantics=("arbitrary", "arbitrary", "arbitrary"),
            vmem_limit_bytes=VMEM_LIMIT),
        name="diff_attn",
    )(cfar, lam, proj, proj, proj, bias_tiles, bias_tiles, w_out_norm)


def _softplus(z):
    return jnp.maximum(z, 0.0) + jnp.log(1.0 + jnp.exp(-jnp.abs(z)))


def _sb_attn_kernel(q_ref, k_ref, v_ref, tri_ref, mask_ref, w_ref, o_ref, r_sc, acc_sc):
    qi = pl.program_id(1)
    tri = tri_ref[...]
    qq = [_stack_subheads(q_ref[0, :, j * LANES:(j + 1) * LANES]) for j in range(SB_PAIRS)]

    def cost_suffix(c):
        hi, lo = _split_hi_lo(c)
        return (jnp.dot(hi, tri, preferred_element_type=jnp.float32)
                + jnp.dot(lo, tri, preferred_element_type=jnp.float32))

    ks = pl.multiple_of(qi * TQ, TQ)
    for j in range(SB_PAIRS):
        k = k_ref[0, pl.ds(ks, TK), j * LANES:(j + 1) * LANES]
        v = v_ref[0, pl.ds(ks, TK), j * LANES:(j + 1) * LANES]
        z = lax.dot_general(qq[j], k, NT_DIMS, preferred_element_type=jnp.float32)
        sp = _softplus(z)
        c = sp * mask_ref[...]
        crem = cost_suffix(c)
        a = jnp.exp(z - sp - crem) * mask_ref[...]
        acc_sc[j] = jnp.dot(a.astype(jnp.bfloat16), v, preferred_element_type=jnp.float32)
        r_sc[j] = crem[:, 0:1] + c[:, 0:1]

    def more(carry):
        i, r_min = carry
        return jnp.logical_and(i < qi, r_min < SB_STOP_COST)

    def step(carry):
        i, _ = carry
        ksi = pl.multiple_of((qi - 1 - i) * TK, TK)
        r_min = None
        for j in range(SB_PAIRS):
            kf = k_ref[0, pl.ds(ksi, TK), j * LANES:(j + 1) * LANES]
            vf = v_ref[0, pl.ds(ksi, TK), j * LANES:(j + 1) * LANES]
            zf = lax.dot_general(qq[j], kf, NT_DIMS, preferred_element_type=jnp.float32)
            spf = _softplus(zf)
            cremf = cost_suffix(spf)
            r = r_sc[j]
            af = jnp.exp(zf - spf - cremf - r)
            acc_sc[j] += jnp.dot(af.astype(jnp.bfloat16), vf, preferred_element_type=jnp.float32)
            r_new = r + cremf[:, 0:1] + spf[:, 0:1]
            r_sc[j] = r_new
            r_min = jnp.min(r_new) if r_min is None else jnp.minimum(r_min, jnp.min(r_new))
        return i + 1, r_min

    lax.while_loop(more, step, (jnp.int32(0), jnp.float32(0.0)))

    lane = lax.broadcasted_iota(jnp.int32, (TQ, LANES), 1)
    first = lane < SB_DIM
    for j in range(SB_PAIRS):
        acc = acc_sc[j]
        o = jnp.where(first, acc[:TQ], acc[TQ:])
        o2 = o * o
        ss_a = jnp.sum(jnp.where(first, o2, 0.0), axis=-1, keepdims=True)
        ss_b = jnp.sum(jnp.where(first, 0.0, o2), axis=-1, keepdims=True)
        ms = jnp.where(first, ss_a, ss_b) * (1.0 / SB_DIM)
        o_ref[0, :, j * LANES:(j + 1) * LANES] = (
            (o * lax.rsqrt(ms + EPS)) * w_ref[...]).astype(o_ref.dtype)


def _sb_attn(proj, tri, mask, w_out_norm):
    b, s, _ = proj.shape
    nq = s // TQ
    wide = SB_PAIRS * LANES
    return pl.pallas_call(
        _sb_attn_kernel,
        grid=(b, nq),
        in_specs=[
            pl.BlockSpec((1, TQ, wide), lambda bi, qi: (bi, qi, QS_BLK // SB_PAIRS)),
            pl.BlockSpec((1, s, wide), lambda bi, qi: (bi, 0, KS_BLK // SB_PAIRS)),
            pl.BlockSpec((1, s, wide), lambda bi, qi: (bi, 0, VS_BLK // SB_PAIRS)),
            pl.BlockSpec((TK, TK), lambda bi, qi: (0, 0)),
            pl.BlockSpec((2 * TQ, TK), lambda bi, qi: (0, 0)),
            pl.BlockSpec((1, LANES), lambda bi, qi: (0, 0)),
        ],
        out_specs=pl.BlockSpec((1, TQ, wide), lambda bi, qi: (bi, qi, 0)),
        out_shape=jax.ShapeDtypeStruct((b, s, SB_HEADS * SB_DIM), jnp.bfloat16),
        scratch_shapes=[
            pltpu.VMEM((SB_PAIRS, 2 * TQ, 1), jnp.float32),
            pltpu.VMEM((SB_PAIRS, 2 * TQ, LANES), jnp.float32),
        ],
        compiler_params=pltpu.CompilerParams(
            dimension_semantics=("arbitrary", "arbitrary"),
            vmem_limit_bytes=VMEM_LIMIT),
        name="sb_attn",
    )(proj, proj, proj, tri, mask, w_out_norm)


def _out_ffn_kernel(x_ref, yd_ref, ys_ref, woa_ref, wob_ref, w2_ref, wg_ref, wu_ref, wd_ref,
                    o_ref):
    h1 = (x_ref[...]
          + jnp.dot(yd_ref[...], woa_ref[...], preferred_element_type=jnp.float32)
          + jnp.dot(ys_ref[...], wob_ref[...], preferred_element_type=jnp.float32))
    ms = jnp.mean(h1 * h1, axis=-1, keepdims=True)
    u2 = ((h1 * lax.rsqrt(ms + EPS)) * w2_ref[...]).astype(jnp.bfloat16)
    ffn = None
    for c in range(D_FF // FF_CHUNK):
        sl = slice(c * FF_CHUNK, (c + 1) * FF_CHUNK)
        gate = jnp.dot(u2, wg_ref[:, sl], preferred_element_type=jnp.float32)
        up = jnp.dot(u2, wu_ref[:, sl], preferred_element_type=jnp.float32)
        act = (gate * (1.0 / (1.0 + jnp.exp(-gate))) * up).astype(jnp.bfloat16)
        part = jnp.dot(act, wd_ref[sl, :], preferred_element_type=jnp.float32)
        ffn = part if ffn is None else ffn + part
    o_ref[...] = h1 + ffn


def _out_ffn(x2d, yd, ys, woa, wob, norm2_w, wg, wu, wd):
    n = x2d.shape[0]
    const = lambda i: (0, 0)
    half = DIFF_HEADS * DIFF_V_DIM
    resident = partial(pl.BlockSpec, index_map=const, pipeline_mode=pl.Buffered(1))
    return pl.pallas_call(
        _out_ffn_kernel,
        grid=(n // TM_PROJ,),
        in_specs=[
            pl.BlockSpec((TM_PROJ, D_MODEL), lambda i: (i, 0)),
            pl.BlockSpec((TM_PROJ, half), lambda i: (i, 0)),
            pl.BlockSpec((TM_PROJ, half), lambda i: (i, 0)),
            resident((half, D_MODEL)),
            resident((half, D_MODEL)),
            pl.BlockSpec((1, D_MODEL), const),
            resident((D_MODEL, D_FF)),
            resident((D_MODEL, D_FF)),
            resident((D_FF, D_MODEL)),
        ],
        out_specs=pl.BlockSpec((TM_PROJ, D_MODEL), lambda i: (i, 0)),
        out_shape=jax.ShapeDtypeStruct((n, D_MODEL), jnp.float32),
        compiler_params=pltpu.CompilerParams(
            dimension_semantics=("arbitrary",), vmem_limit_bytes=VMEM_LIMIT),
        name="out_ffn",
    )(x2d, yd, ys, woa, wob, norm2_w, wg, wu, wd)


def _diff_bias_tiles(rel_bias):
    qq = jnp.arange(TQD, dtype=jnp.int32)[:, None]
    kk = jnp.arange(TKD, dtype=jnp.int32)[None, :]
    width = TQD + TKD + 1
    tiles = []
    for koff in (0, -TKD):
        kpos = kk + koff
        table = rel_bias.astype(jnp.float32)[
            _t5_bucket(jnp.arange(width, dtype=jnp.int32) - TQD + koff)].T
        toep = jnp.tile(table, (1, TQD))[:, :TQD * (width - 1)].reshape(DIFF_HEADS, TQD, width - 1)
        bias = toep[:, :, TQD:TQD + TKD]
        allowed = jnp.floor_divide(kpos, CHUNK) <= jnp.floor_divide(qq, CHUNK)
        tiles.append(jnp.where(allowed[None], bias, MASK_VALUE))
    return jnp.stack(tiles, axis=0)


def kernel(x, norm1_w, w_in, q_norm_w, k_norm_w, lambda_q1, lambda_k1, lambda_q2, lambda_k2,
           diff_out_norm_w, sb_out_norm_w, w_out, norm2_w, w_gate, w_up, w_down, rel_bias):
    b, s, d = x.shape
    assert (b, s, d) == (8, 4096, D_MODEL) and w_in.shape[0] == 1
    f32, bf16 = jnp.float32, jnp.bfloat16
    x2d = x.reshape(b * s, d)

    qkw = jnp.concatenate([jnp.tile(q_norm_w[0].astype(f32), 2 * DIFF_HEADS)
                           * (DIFF_QK_DIM ** -0.5 * LOG2E),
                           jnp.tile(k_norm_w[0].astype(f32), 2 * DIFF_HEADS)])[None, :]
    grp = jnp.arange(LANES) // DIFF_QK_DIM
    gmat = jnp.where(grp[:, None] == grp[None, :], 1.0 / DIFF_QK_DIM, 0.0).astype(bf16)
    lam = (jnp.exp(jnp.sum(lambda_q1[0].astype(f32) * lambda_k1[0].astype(f32)))
           - jnp.exp(jnp.sum(lambda_q2[0].astype(f32) * lambda_k2[0].astype(f32)))
           + LAMBDA_INIT).reshape(1)
    bias_tiles = _diff_bias_tiles(rel_bias) * LOG2E
    cfar = rel_bias.astype(f32)[_t5_bucket(jnp.int32(-(TKD + 1)))] * LOG2E
    jj = jnp.arange(TK)
    tri = (jj[:, None] > jj[None, :]).astype(bf16)
    causal = (jj[None, :] < jnp.arange(TQ)[:, None]).astype(f32)
    sb_mask = jnp.concatenate([causal, causal], axis=0)

    proj = _in_proj(x2d, norm1_w[0].astype(f32)[None, :], w_in[0].astype(bf16), qkw, gmat)
    proj = proj.reshape(b, s, IN_COLS)
    y_diff = _diff_attn(proj, bias_tiles, cfar, lam, diff_out_norm_w[0].astype(f32)[None, :])
    y_sb = _sb_attn(proj, tri, sb_mask, jnp.tile(sb_out_norm_w[0].astype(f32), 2)[None, :])

    half = DIFF_HEADS * DIFF_V_DIM
    wo = w_out[0].astype(bf16)
    out = _out_ffn(x2d, y_diff.reshape(b * s, half), y_sb.reshape(b * s, half),
                   wo[:half], wo[half:], norm2_w[0].astype(f32)[None, :],
                   w_gate[0].astype(bf16), w_up[0].astype(bf16), w_down[0].astype(bf16))
    return out.reshape(b, s, d)
```

```python
import math
from functools import partial

import jax
import jax.numpy as jnp
from jax import lax
from jax.experimental import pallas as pl
from jax.experimental.pallas import tpu as pltpu

D_MODEL = 1024
CHUNK = 64
DIFF_QK_DIM = 64
DIFF_V_DIM = 128
DIFF_HEADS = 4
SB_DIM = 64
SB_HEADS = 8
NUM_BUCKETS = 32
MAX_DISTANCE = 128
D_FF = 2816
EPS = 1e-6
IN_COLS = 3072
LAMBDA_INIT = 0.8 - 0.6 * math.exp(-0.3 * 0)
LANES = 128

QD_BLK, KD_BLK, VD_BLK = 0, 4, 8
QS_BLK, KS_BLK, VS_BLK = 12, 16, 20

TM_PROJ = 512
COL_CHUNK = 512
TQD = 512
TKD = 512
HEADS_PER_STEP = 2
ONES_ROWS = 16
LOG2E = math.log2(math.e)
TQ = 256
TK = 256
SB_PAIRS = SB_HEADS // 2
FF_CHUNK = 256
MASK_VALUE = -1e30
SB_STOP_COST = 104.0
VMEM_LIMIT = 56 * 1024 * 1024

NT_DIMS = (((1,), (1,)), ((), ()))


def _t5_bucket(rel):
    nb = NUM_BUCKETS // 2
    max_exact = nb // 2
    ret = (rel > 0).astype(jnp.int32) * nb
    n = jnp.abs(rel)
    nf = jnp.maximum(n, 1).astype(jnp.float32)
    large = max_exact + (jnp.log(nf / max_exact) / math.log(MAX_DISTANCE / max_exact)
                         * (nb - max_exact)).astype(jnp.int32)
    large = jnp.minimum(large, nb - 1)
    return ret + jnp.where(n < max_exact, n, large)


def _split_hi_lo(x):
    hi = x.astype(jnp.bfloat16)
    lo = (x - hi.astype(jnp.float32)).astype(jnp.bfloat16)
    return hi, lo


def _in_proj_kernel(x_ref, w1_ref, win_ref, qkw_ref, g_ref, o_ref):
    x = x_ref[...]
    ms = jnp.mean(x * x, axis=-1, keepdims=True)
    u = ((x * lax.rsqrt(ms + EPS)) * w1_ref[...]).astype(jnp.bfloat16)
    g = g_ref[...]
    for c in range(IN_COLS // COL_CHUNK):
        lo_c, hi_c = c * COL_CHUNK, (c + 1) * COL_CHUNK
        p = jnp.dot(u, win_ref[:, lo_c:hi_c], preferred_element_type=jnp.float32)
        if hi_c <= 2 * DIFF_HEADS * 2 * DIFF_QK_DIM:
            for j in range(COL_CHUNK // LANES):
                sl = slice(j * LANES, (j + 1) * LANES)
                pj = p[:, sl]
                hi, lo = _split_hi_lo(pj * pj)
                msq = (jnp.dot(hi, g, preferred_element_type=jnp.float32)
                       + jnp.dot(lo, g, preferred_element_type=jnp.float32))
                w = qkw_ref[:, lo_c + j * LANES: lo_c + (j + 1) * LANES]
                o_ref[:, lo_c + j * LANES: lo_c + (j + 1) * LANES] = (
                    (pj * lax.rsqrt(msq + EPS)) * w).astype(jnp.bfloat16)
        else:
            if lo_c == QS_BLK * LANES:
                p = p * (SB_DIM ** -0.5)
            o_ref[:, lo_c:hi_c] = p.astype(jnp.bfloat16)


def _in_proj(x2d, norm1_w, w_in_bf, qkw, gmat):
    n = x2d.shape[0]
    const = lambda i: (0, 0)
    return pl.pallas_call(
        _in_proj_kernel,
        grid=(n // TM_PROJ,),
        in_specs=[
            pl.BlockSpec((TM_PROJ, D_MODEL), lambda i: (i, 0)),
            pl.BlockSpec((1, D_MODEL), const),
            pl.BlockSpec((D_MODEL, IN_COLS), const, pipeline_mode=pl.Buffered(1)),
            pl.BlockSpec((1, 2 * DIFF_HEADS * 2 * DIFF_QK_DIM), const),
            pl.BlockSpec((LANES, LANES), const),
        ],
        out_specs=pl.BlockSpec((TM_PROJ, IN_COLS), lambda i: (i, 0)),
        out_shape=jax.ShapeDtypeStruct((n, IN_COLS), jnp.bfloat16),
        compiler_params=pltpu.CompilerParams(
            dimension_semantics=("arbitrary",), vmem_limit_bytes=VMEM_LIMIT),
        name="in_proj",
    )(x2d, norm1_w, w_in_bf, qkw, gmat)


def _stack_subheads(q):
    lane = lax.broadcasted_iota(jnp.int32, q.shape, 1)
    zero = jnp.zeros_like(q)
    return jnp.concatenate([jnp.where(lane < SB_DIM, q, zero),
                            jnp.where(lane >= SB_DIM, q, zero)], axis=0)


def _diff_attn_kernel(cfar_ref, lam_ref, q_ref, k_ref, v_ref, bdiag_ref, bsub_ref, w_ref, o_ref,
                      vt_sc, m_sc, acc_sc):
    hp = pl.program_id(1)
    qi = pl.program_id(2)

    @pl.when(qi == 0)
    def _build_vt():
        ones = jnp.ones((ONES_ROWS, TKD), vt_sc.dtype)
        for hh in range(HEADS_PER_STEP):
            for t in range(vt_sc.shape[1]):
                blk = v_ref[0, t * TKD:(t + 1) * TKD, hh * LANES:(hh + 1) * LANES]
                vt_sc[hh, t, :DIFF_V_DIM, :] = blk.astype(jnp.float32).T.astype(vt_sc.dtype)
                vt_sc[hh, t, DIFF_V_DIM:, :] = ones

    lane = lax.broadcasted_iota(jnp.int32, (TQD, LANES), 1)
    zero = jnp.zeros((TQD, LANES), q_ref.dtype)
    q_sub = []
    for hh in range(HEADS_PER_STEP):
        q = q_ref[0, :, hh * LANES:(hh + 1) * LANES]
        q_sub.append(jnp.where(lane < DIFF_QK_DIM, q, zero))
        q_sub.append(jnp.where(lane >= DIFF_QK_DIM, q, zero))

    def tile_update(kt, bias_ref, cfar, first):
        ks = pl.multiple_of(kt * TKD, TKD)
        n_maps = 2 * HEADS_PER_STEP

        def scores(idx):
            hh = idx // 2
            k = k_ref[0, pl.ds(ks, TKD), hh * LANES:(hh + 1) * LANES]
            return lax.dot_general(k, q_sub[idx], NT_DIMS, preferred_element_type=jnp.float32)

        def weights(idx, s):
            hh = idx // 2
            if first:
                s = s + bias_ref[0, hh]
                m_new = jnp.max(s, axis=0, keepdims=True)
                m_sc[idx] = jnp.broadcast_to(m_new, (8, TQD))
                return jnp.exp2(s - m_new).astype(jnp.bfloat16), None
            m_old = m_sc[idx][0:1, :]
            if bias_ref is None:
                m_new = jnp.maximum(m_old, jnp.max(s, axis=0, keepdims=True) + cfar[hh])
                shift = m_new - cfar[hh]
            else:
                s = s + bias_ref[0, hh]
                m_new = jnp.maximum(m_old, jnp.max(s, axis=0, keepdims=True))
                shift = m_new
            m_sc[idx] = jnp.broadcast_to(m_new, (8, TQD))
            return jnp.exp2(s - shift).astype(jnp.bfloat16), jnp.exp2(m_old - m_new)

        def accumulate(idx, p, alpha):
            pv = jnp.dot(vt_sc[idx // 2, kt], p, preferred_element_type=jnp.float32)
            acc_sc[idx] = pv if first else alpha * acc_sc[idx] + pv

        s_next = [scores(0), scores(1)]
        pending = None
        for idx in range(n_maps):
            p, alpha = weights(idx, s_next.pop(0))
            if idx + 2 < n_maps:
                s_next.append(scores(idx + 2))
            if pending is not None:
                accumulate(*pending)
            pending = (idx, p, alpha)
        accumulate(*pending)

    tile_update(qi, bdiag_ref, None, True)

    @pl.when(qi >= 1)
    def _sub_diagonal():
        tile_update(qi - 1, bsub_ref, None, False)

    cfar = [cfar_ref[HEADS_PER_STEP * hp + hh] for hh in range(HEADS_PER_STEP)]

    def far_step(i, carry):
        tile_update(i, None, cfar, False)
        return carry

    lax.fori_loop(0, jnp.maximum(qi - 1, 0), far_step, 0)

    for hh in range(HEADS_PER_STEP):
        acc0 = acc_sc[2 * hh]
        acc1 = acc_sc[2 * hh + 1]
        o_t = (acc0[:DIFF_V_DIM] / acc0[DIFF_V_DIM:DIFF_V_DIM + 1]
               - lam_ref[0] * (acc1[:DIFF_V_DIM] / acc1[DIFF_V_DIM:DIFF_V_DIM + 1]))
        o = o_t.T
        ms = jnp.mean(o * o, axis=-1, keepdims=True)
        y = ((o * lax.rsqrt(ms + EPS)) * w_ref[...]) * (1.0 - LAMBDA_INIT)
        o_ref[0, :, hh * LANES:(hh + 1) * LANES] = y.astype(o_ref.dtype)


def _diff_attn(proj, bias_tiles, cfar, lam, w_out_norm):
    b, s, _ = proj.shape
    nq = s // TQD
    hps = HEADS_PER_STEP
    wide = hps * LANES
    smem = pl.BlockSpec(memory_space=pltpu.SMEM)
    return pl.pallas_call(
        _diff_attn_kernel,
        grid=(b, DIFF_HEADS // hps, nq),
        in_specs=[
            smem, smem,
            pl.BlockSpec((1, TQD, wide), lambda bi, hp, qi: (bi, qi, QD_BLK // hps + hp)),
            pl.BlockSpec((1, s, wide), lambda bi, hp, qi: (bi, 0, KD_BLK // hps + hp)),
            pl.BlockSpec((1, s, wide), lambda bi, hp, qi: (bi, 0, VD_BLK // hps + hp)),
            pl.BlockSpec((1, hps, TQD, TKD), lambda bi, hp, qi: (0, hp, 0, 0)),
            pl.BlockSpec((1, hps, TQD, TKD), lambda bi, hp, qi: (1, hp, 0, 0)),
            pl.BlockSpec((1, DIFF_V_DIM), lambda bi, hp, qi: (0, 0)),
        ],
        out_specs=pl.BlockSpec((1, TQD, wide), lambda bi, hp, qi: (bi, qi, hp)),
        out_shape=jax.ShapeDtypeStruct((b, s, DIFF_HEADS * DIFF_V_DIM), jnp.bfloat16),
        scratch_shapes=[
            pltpu.VMEM((hps, s // TKD, DIFF_V_DIM + ONES_ROWS, TKD), jnp.bfloat16),
            pltpu.VMEM((2 * hps, 8, TQD), jnp.float32),
            pltpu.VMEM((2 * hps, DIFF_V_DIM + ONES_ROWS, TQD), jnp.float32),
        ],
        compiler_params=pltpu.CompilerParams(
            dimension_semantics=("arbitrary", "arbitrary", "arbitrary"),
            vmem_limit_bytes=VMEM_LIMIT),
        name="diff_attn",
    )(cfar, lam, proj, proj, proj, bias_tiles, bias_tiles, w_out_norm)


def _softplus(z):
    return jnp.maximum(z, 0.0) + jnp.log(1.0 + jnp.exp(-jnp.abs(z)))


def _sb_attn_kernel(q_ref, k_ref, v_ref, tri_ref, mask_ref, w_ref, o_ref, r_sc, acc_sc):
    qi = pl.program_id(1)
    tri = tri_ref[...]
    qq = [_stack_subheads(q_ref[0, :, j * LANES:(j + 1) * LANES]) for j in range(SB_PAIRS)]

    def cost_suffix(c):
        hi, lo = _split_hi_lo(c)
        return (jnp.dot(hi, tri, preferred_element_type=jnp.float32)
                + jnp.dot(lo, tri, preferred_element_type=jnp.float32))

    ks = pl.multiple_of(qi * TQ, TQ)
    for j in range(SB_PAIRS):
        k = k_ref[0, pl.ds(ks, TK), j * LANES:(j + 1) * LANES]
        v = v_ref[0, pl.ds(ks, TK), j * LANES:(j + 1) * LANES]
        z = lax.dot_general(qq[j], k, NT_DIMS, preferred_element_type=jnp.float32)
        sp = _softplus(z)
        c = sp * mask_ref[...]
        crem = cost_suffix(c)
        a = jnp.exp(z - sp - crem) * mask_ref[...]
        acc_sc[j] = jnp.dot(a.astype(jnp.bfloat16), v, preferred_element_type=jnp.float32)
        r_sc[j] = crem[:, 0:1] + c[:, 0:1]

    def more(carry):
        i, r_min = carry
        return jnp.logical_and(i < qi, r_min < SB_STOP_COST)

    def step(carry):
        i, _ = carry
        ksi = pl.multiple_of((qi - 1 - i) * TK, TK)
        r_min = None
        for j in range(SB_PAIRS):
            kf = k_ref[0, pl.ds(ksi, TK), j * LANES:(j + 1) * LANES]
            vf = v_ref[0, pl.ds(ksi, TK), j * LANES:(j + 1) * LANES]
            zf = lax.dot_general(qq[j], kf, NT_DIMS, preferred_element_type=jnp.float32)
            spf = _softplus(zf)
            cremf = cost_suffix(spf)
            r = r_sc[j]
            af = jnp.exp(zf - spf - cremf - r)
            acc_sc[j] += jnp.dot(af.astype(jnp.bfloat16), vf, preferred_element_type=jnp.float32)
            r_new = r + cremf[:, 0:1] + spf[:, 0:1]
            r_sc[j] = r_new
            r_min = jnp.min(r_new) if r_min is None else jnp.minimum(r_min, jnp.min(r_new))
        return i + 1, r_min

    lax.while_loop(more, step, (jnp.int32(0), jnp.float32(0.0)))

    lane = lax.broadcasted_iota(jnp.int32, (TQ, LANES), 1)
    first = lane < SB_DIM
    for j in range(SB_PAIRS):
        acc = acc_sc[j]
        o = jnp.where(first, acc[:TQ], acc[TQ:])
        o2 = o * o
        ss_a = jnp.sum(jnp.where(first, o2, 0.0), axis=-1, keepdims=True)
        ss_b = jnp.sum(jnp.where(first, 0.0, o2), axis=-1, keepdims=True)
        ms = jnp.where(first, ss_a, ss_b) * (1.0 / SB_DIM)
        o_ref[0, :, j * LANES:(j + 1) * LANES] = (
            (o * lax.rsqrt(ms + EPS)) * w_ref[...]).astype(o_ref.dtype)


def _sb_attn(proj, tri, mask, w_out_norm):
    b, s, _ = proj.shape
    nq = s // TQ
    wide = SB_PAIRS * LANES
    return pl.pallas_call(
        _sb_attn_kernel,
        grid=(b, nq),
        in_specs=[
            pl.BlockSpec((1, TQ, wide), lambda bi, qi: (bi, qi, QS_BLK // SB_PAIRS)),
            pl.BlockSpec((1, s, wide), lambda bi, qi: (bi, 0, KS_BLK // SB_PAIRS)),
            pl.BlockSpec((1, s, wide), lambda bi, qi: (bi, 0, VS_BLK // SB_PAIRS)),
            pl.BlockSpec((TK, TK), lambda bi, qi: (0, 0)),
            pl.BlockSpec((2 * TQ, TK), lambda bi, qi: (0, 0)),
            pl.BlockSpec((1, LANES), lambda bi, qi: (0, 0)),
        ],
        out_specs=pl.BlockSpec((1, TQ, wide), lambda bi, qi: (bi, qi, 0)),
        out_shape=jax.ShapeDtypeStruct((b, s, SB_HEADS * SB_DIM), jnp.bfloat16),
        scratch_shapes=[
            pltpu.VMEM((SB_PAIRS, 2 * TQ, 1), jnp.float32),
            pltpu.VMEM((SB_PAIRS, 2 * TQ, LANES), jnp.float32),
        ],
        compiler_params=pltpu.CompilerParams(
            dimension_semantics=("arbitrary", "arbitrary"),
            vmem_limit_bytes=VMEM_LIMIT),
        name="sb_attn",
    )(proj, proj, proj, tri, mask, w_out_norm)


def _out_ffn_kernel(x_ref, yd_ref, ys_ref, woa_ref, wob_ref, w2_ref, wg_ref, wu_ref, wd_ref,
                    o_ref):
    h1 = (x_ref[...]
          + jnp.dot(yd_ref[...], woa_ref[...], preferred_element_type=jnp.float32)
          + jnp.dot(ys_ref[...], wob_ref[...], preferred_element_type=jnp.float32))
    ms = jnp.mean(h1 * h1, axis=-1, keepdims=True)
    u2 = ((h1 * lax.rsqrt(ms + EPS)) * w2_ref[...]).astype(jnp.bfloat16)
    ffn = None
    for c in range(D_FF // FF_CHUNK):
        sl = slice(c * FF_CHUNK, (c + 1) * FF_CHUNK)
        gate = jnp.dot(u2, wg_ref[:, sl], preferred_element_type=jnp.float32)
        up = jnp.dot(u2, wu_ref[:, sl], preferred_element_type=jnp.float32)
        act = (gate * (1.0 / (1.0 + jnp.exp(-gate))) * up).astype(jnp.bfloat16)
        part = jnp.dot(act, wd_ref[sl, :], preferred_element_type=jnp.float32)
        ffn = part if ffn is None else ffn + part
    o_ref[...] = h1 + ffn


def _out_ffn(x2d, yd, ys, woa, wob, norm2_w, wg, wu, wd):
    n = x2d.shape[0]
    const = lambda i: (0, 0)
    half = DIFF_HEADS * DIFF_V_DIM
    resident = partial(pl.BlockSpec, index_map=const, pipeline_mode=pl.Buffered(1))
    return pl.pallas_call(
        _out_ffn_kernel,
        grid=(n // TM_PROJ,),
        in_specs=[
            pl.BlockSpec((TM_PROJ, D_MODEL), lambda i: (i, 0)),
            pl.BlockSpec((TM_PROJ, half), lambda i: (i, 0)),
            pl.BlockSpec((TM_PROJ, half), lambda i: (i, 0)),
            resident((half, D_MODEL)),
            resident((half, D_MODEL)),
            pl.BlockSpec((1, D_MODEL), const),
            resident((D_MODEL, D_FF)),
            resident((D_MODEL, D_FF)),
            resident((D_FF, D_MODEL)),
        ],
        out_specs=pl.BlockSpec((TM_PROJ, D_MODEL), lambda i: (i, 0)),
        out_shape=jax.ShapeDtypeStruct((n, D_MODEL), jnp.float32),
        compiler_params=pltpu.CompilerParams(
            dimension_semantics=("arbitrary",), vmem_limit_bytes=VMEM_LIMIT),
        name="out_ffn",
    )(x2d, yd, ys, woa, wob, norm2_w, wg, wu, wd)


def _diff_bias_tiles(rel_bias):
    qq = jnp.arange(TQD, dtype=jnp.int32)[:, None]
    kk = jnp.arange(TKD, dtype=jnp.int32)[None, :]
    width = TQD + TKD + 1
    tiles = []
    for koff in (0, -TKD):
        kpos = kk + koff
        table = rel_bias.astype(jnp.float32)[
            _t5_bucket(jnp.arange(width, dtype=jnp.int32) - TQD + koff)].T
        toep = jnp.tile(table, (1, TQD))[:, :TQD * (width - 1)].reshape(DIFF_HEADS, TQD, width - 1)
        bias = toep[:, :, TQD:TQD + TKD]
        allowed = jnp.floor_divide(kpos, CHUNK) <= jnp.floor_divide(qq, CHUNK)
        tiles.append(jnp.where(allowed[None], bias, MASK_VALUE))
    return jnp.swapaxes(jnp.stack(tiles, axis=0), 2, 3)


def kernel(x, norm1_w, w_in, q_norm_w, k_norm_w, lambda_q1, lambda_k1, lambda_q2, lambda_k2,
           diff_out_norm_w, sb_out_norm_w, w_out, norm2_w, w_gate, w_up, w_down, rel_bias):
    b, s, d = x.shape
    assert (b, s, d) == (8, 4096, D_MODEL) and w_in.shape[0] == 1
    f32, bf16 = jnp.float32, jnp.bfloat16
    x2d = x.reshape(b * s, d)

    qkw = jnp.concatenate([jnp.tile(q_norm_w[0].astype(f32), 2 * DIFF_HEADS)
                           * (DIFF_QK_DIM ** -0.5 * LOG2E),
                           jnp.tile(k_norm_w[0].astype(f32), 2 * DIFF_HEADS)])[None, :]
    grp = jnp.arange(LANES) // DIFF_QK_DIM
    gmat = jnp.where(grp[:, None] == grp[None, :], 1.0 / DIFF_QK_DIM, 0.0).astype(bf16)
    lam = (jnp.exp(jnp.sum(lambda_q1[0].astype(f32) * lambda_k1[0].astype(f32)))
           - jnp.exp(jnp.sum(lambda_q2[0].astype(f32) * lambda_k2[0].astype(f32)))
           + LAMBDA_INIT).reshape(1)
    bias_tiles = _diff_bias_tiles(rel_bias) * LOG2E
    cfar = rel_bias.astype(f32)[_t5_bucket(jnp.int32(-(TKD + 1)))] * LOG2E
    jj = jnp.arange(TK)
    tri = (jj[:, None] > jj[None, :]).astype(bf16)
    causal = (jj[None, :] < jnp.arange(TQ)[:, None]).astype(f32)
    sb_mask = jnp.concatenate([causal, causal], axis=0)

    proj = _in_proj(x2d, norm1_w[0].astype(f32)[None, :], w_in[0].astype(bf16), qkw, gmat)
    proj = proj.reshape(b, s, IN_COLS)
    y_diff = _diff_attn(proj, bias_tiles, cfar, lam, diff_out_norm_w[0].astype(f32)[None, :])
    y_sb = _sb_attn(proj, tri, sb_mask, jnp.tile(sb_out_norm_w[0].astype(f32), 2)[None, :])

    half = DIFF_HEADS * DIFF_V_DIM
    wo = w_out[0].astype(bf16)
    out = _out_ffn(x2d, y_diff.reshape(b * s, half), y_sb.reshape(b * s, half),
                   wo[:half], wo[half:], norm2_w[0].astype(f32)[None, :],
                   w_gate[0].astype(bf16), w_up[0].astype(bf16), w_down[0].astype(bf16))
    return out.reshape(b, s, d)
```

```python
import math
from functools import partial

import jax
import jax.numpy as jnp
from jax import lax
from jax.experimental import pallas as pl
from jax.experimental.pallas import tpu as pltpu

D_MODEL = 1024
CHUNK = 64
DIFF_QK_DIM = 64
DIFF_V_DIM = 128
DIFF_HEADS = 4
SB_DIM = 64
SB_HEADS = 8
NUM_BUCKETS = 32
MAX_DISTANCE = 128
D_FF = 2816
EPS = 1e-6
IN_COLS = 3072
LAMBDA_INIT = 0.8 - 0.6 * math.exp(-0.3 * 0)
LANES = 128

QD_BLK, KD_BLK, VD_BLK = 0, 4, 8
QS_BLK, KS_BLK, VS_BLK = 12, 16, 20

TM_PROJ = 512
COL_CHUNK = 512
TQD = 512
TKD = 512
HEADS_PER_STEP = 2
ONES_ROWS = 16
LOG2E = math.log2(math.e)
TQ = 256
TK = 256
SB_PAIRS = SB_HEADS // 2
FF_CHUNK = 256
MASK_VALUE = -1e30
SB_STOP_COST = 104.0
VMEM_LIMIT = 56 * 1024 * 1024

NT_DIMS = (((1,), (1,)), ((), ()))


def _t5_bucket(rel):
    nb = NUM_BUCKETS // 2
    max_exact = nb // 2
    ret = (rel > 0).astype(jnp.int32) * nb
    n = jnp.abs(rel)
    nf = jnp.maximum(n, 1).astype(jnp.float32)
    large = max_exact + (jnp.log(nf / max_exact) / math.log(MAX_DISTANCE / max_exact)
                         * (nb - max_exact)).astype(jnp.int32)
    large = jnp.minimum(large, nb - 1)
    return ret + jnp.where(n < max_exact, n, large)


def _split_hi_lo(x):
    hi = x.astype(jnp.bfloat16)
    lo = (x - hi.astype(jnp.float32)).astype(jnp.bfloat16)
    return hi, lo


def _in_proj_kernel(x_ref, w1_ref, win_ref, qkw_ref, g_ref, o_ref):
    x = x_ref[...]
    ms = jnp.mean(x * x, axis=-1, keepdims=True)
    u = ((x * lax.rsqrt(ms + EPS)) * w1_ref[...]).astype(jnp.bfloat16)
    g = g_ref[...]
    for c in range(IN_COLS // COL_CHUNK):
        lo_c, hi_c = c * COL_CHUNK, (c + 1) * COL_CHUNK
        p = jnp.dot(u, win_ref[:, lo_c:hi_c], preferred_element_type=jnp.float32)
        if hi_c <= 2 * DIFF_HEADS * 2 * DIFF_QK_DIM:
            for j in range(COL_CHUNK // LANES):
                sl = slice(j * LANES, (j + 1) * LANES)
                pj = p[:, sl]
                hi, lo = _split_hi_lo(pj * pj)
                msq = (jnp.dot(hi, g, preferred_element_type=jnp.float32)
                       + jnp.dot(lo, g, preferred_element_type=jnp.float32))
                w = qkw_ref[:, lo_c + j * LANES: lo_c + (j + 1) * LANES]
                o_ref[:, lo_c + j * LANES: lo_c + (j + 1) * LANES] = (
                    (pj * lax.rsqrt(msq + EPS)) * w).astype(jnp.bfloat16)
        else:
            if lo_c == QS_BLK * LANES:
                p = p * (SB_DIM ** -0.5)
            o_ref[:, lo_c:hi_c] = p.astype(jnp.bfloat16)


def _in_proj(x2d, norm1_w, w_in_bf, qkw, gmat):
    n = x2d.shape[0]
    const = lambda i: (0, 0)
    return pl.pallas_call(
        _in_proj_kernel,
        grid=(n // TM_PROJ,),
        in_specs=[
            pl.BlockSpec((TM_PROJ, D_MODEL), lambda i: (i, 0)),
            pl.BlockSpec((1, D_MODEL), const),
            pl.BlockSpec((D_MODEL, IN_COLS), const, pipeline_mode=pl.Buffered(1)),
            pl.BlockSpec((1, 2 * DIFF_HEADS * 2 * DIFF_QK_DIM), const),
            pl.BlockSpec((LANES, LANES), const),
        ],
        out_specs=pl.BlockSpec((TM_PROJ, IN_COLS), lambda i: (i, 0)),
        out_shape=jax.ShapeDtypeStruct((n, IN_COLS), jnp.bfloat16),
        compiler_params=pltpu.CompilerParams(
            dimension_semantics=("arbitrary",), vmem_limit_bytes=VMEM_LIMIT),
        name="in_proj",
    )(x2d, norm1_w, w_in_bf, qkw, gmat)


def _stack_subheads(q):
    lane = lax.broadcasted_iota(jnp.int32, q.shape, 1)
    zero = jnp.zeros_like(q)
    return jnp.concatenate([jnp.where(lane < SB_DIM, q, zero),
                            jnp.where(lane >= SB_DIM, q, zero)], axis=0)


def _diff_attn_kernel(cfar_ref, lam_ref, q_ref, k_ref, v_ref, bdiag_ref, bsub_ref, w_ref, o_ref,
                      vt_sc, m_sc, acc_sc):
    hp = pl.program_id(1)
    qi = pl.program_id(2)

    @pl.when(qi == 0)
    def _build_vt():
        ones = jnp.ones((ONES_ROWS, TKD), vt_sc.dtype)
        for hh in range(HEADS_PER_STEP):
            for t in range(vt_sc.shape[1]):
                blk = v_ref[0, t * TKD:(t + 1) * TKD, hh * LANES:(hh + 1) * LANES]
                vt_sc[hh, t, :DIFF_V_DIM, :] = blk.astype(jnp.float32).T.astype(vt_sc.dtype)
                vt_sc[hh, t, DIFF_V_DIM:, :] = ones

    lane = lax.broadcasted_iota(jnp.int32, (TQD, LANES), 1)
    zero = jnp.zeros((TQD, LANES), q_ref.dtype)
    q_sub = []
    for hh in range(HEADS_PER_STEP):
        q = q_ref[0, :, hh * LANES:(hh + 1) * LANES]
        q_sub.append(jnp.where(lane < DIFF_QK_DIM, q, zero))
        q_sub.append(jnp.where(lane >= DIFF_QK_DIM, q, zero))

    def tile_update(kt, bias_ref, cfar, first):
        ks = pl.multiple_of(kt * TKD, TKD)
        n_maps = 2 * HEADS_PER_STEP

        def scores(idx):
            hh = idx // 2
            k = k_ref[0, pl.ds(ks, TKD), hh * LANES:(hh + 1) * LANES]
            return lax.dot_general(k, q_sub[idx], NT_DIMS, preferred_element_type=jnp.float32)

        def weights(idx, s):
            hh = idx // 2
            if first:
                s = s + bias_ref[0, hh]
                m_new = jnp.max(s, axis=0, keepdims=True)
                m_sc[idx] = jnp.broadcast_to(m_new, (8, TQD))
                return jnp.exp2(s - m_new).astype(jnp.bfloat16), None
            m_old = m_sc[idx][0:1, :]
            if bias_ref is None:
                m_new = jnp.maximum(m_old, jnp.max(s, axis=0, keepdims=True) + cfar[hh])
                shift = m_new - cfar[hh]
            else:
                s = s + bias_ref[0, hh]
                m_new = jnp.maximum(m_old, jnp.max(s, axis=0, keepdims=True))
                shift = m_new
            m_sc[idx] = jnp.broadcast_to(m_new, (8, TQD))
            return jnp.exp2(s - shift).astype(jnp.bfloat16), jnp.exp2(m_old - m_new)

        def accumulate(idx, p, alpha):
            pv = jnp.dot(vt_sc[idx // 2, kt], p, preferred_element_type=jnp.float32)
            acc_sc[idx] = pv if first else alpha * acc_sc[idx] + pv

        s_next = [scores(0), scores(1)]
        pending = None
        for idx in range(n_maps):
            p, alpha = weights(idx, s_next.pop(0))
            if idx + 2 < n_maps:
                s_next.append(scores(idx + 2))
            if pending is not None:
                accumulate(*pending)
            pending = (idx, p, alpha)
        accumulate(*pending)

    tile_update(qi, bdiag_ref, None, True)

    @pl.when(qi >= 1)
    def _sub_diagonal():
        tile_update(qi - 1, bsub_ref, None, False)

    cfar = [cfar_ref[HEADS_PER_STEP * hp + hh] for hh in range(HEADS_PER_STEP)]

    def far_step(i, carry):
        tile_update(i, None, cfar, False)
        return carry

    lax.fori_loop(0, jnp.maximum(qi - 1, 0), far_step, 0)

    for hh in range(HEADS_PER_STEP):
        acc0 = acc_sc[2 * hh]
        acc1 = acc_sc[2 * hh + 1]
        o_t = (acc0[:DIFF_V_DIM] / acc0[DIFF_V_DIM:DIFF_V_DIM + 1]
               - lam_ref[0] * (acc1[:DIFF_V_DIM] / acc1[DIFF_V_DIM:DIFF_V_DIM + 1]))
        o = o_t.T
        ms = jnp.mean(o * o, axis=-1, keepdims=True)
        y = ((o * lax.rsqrt(ms + EPS)) * w_ref[...]) * (1.0 - LAMBDA_INIT)
        o_ref[0, :, hh * LANES:(hh + 1) * LANES] = y.astype(o_ref.dtype)


def _diff_attn(proj, bias_tiles, cfar, lam, w_out_norm):
    b, s, _ = proj.shape
    nq = s // TQD
    hps = HEADS_PER_STEP
    wide = hps * LANES
    smem = pl.BlockSpec(memory_space=pltpu.SMEM)
    return pl.pallas_call(
        _diff_attn_kernel,
        grid=(b, DIFF_HEADS // hps, nq),
        in_specs=[
            smem, smem,
            pl.BlockSpec((1, TQD, wide), lambda bi, hp, qi: (bi, qi, QD_BLK // hps + hp)),
            pl.BlockSpec((1, s, wide), lambda bi, hp, qi: (bi, 0, KD_BLK // hps + hp)),
            pl.BlockSpec((1, s, wide), lambda bi, hp, qi: (bi, 0, VD_BLK // hps + hp)),
            pl.BlockSpec((1, hps, TQD, TKD), lambda bi, hp, qi: (0, hp, 0, 0)),
            pl.BlockSpec((1, hps, TQD, TKD), lambda bi, hp, qi: (1, hp, 0, 0)),
            pl.BlockSpec((1, DIFF_V_DIM), lambda bi, hp, qi: (0, 0)),
        ],
        out_specs=pl.BlockSpec((1, TQD, wide), lambda bi, hp, qi: (bi, qi, hp)),
        out_shape=jax.ShapeDtypeStruct((b, s, DIFF_HEADS * DIFF_V_DIM), jnp.bfloat16),
        scratch_shapes=[
            pltpu.VMEM((hps, s // TKD, DIFF_V_DIM + ONES_ROWS, TKD), jnp.bfloat16),
            pltpu.VMEM((2 * hps, 8, TQD), jnp.float32),
            pltpu.VMEM((2 * hps, DIFF_V_DIM + ONES_ROWS, TQD), jnp.float32),
        ],
        compiler_params=pltpu.CompilerParams(
            dimension_semantics=("arbitrary", "arbitrary", "arbitrary"),
            vmem_limit_bytes=VMEM_LIMIT),
        name="diff_attn",
    )(cfar, lam, proj, proj, proj, bias_tiles, bias_tiles, w_out_norm)


def _softplus(z):
    return jnp.maximum(z, 0.0) + jnp.log(1.0 + jnp.exp(-jnp.abs(z)))


def _sb_attn_kernel(q_ref, k_ref, v_ref, tri_ref, mask_ref, w_ref, o_ref, vt_sc, r_sc, acc_sc):
    qi = pl.program_id(1)
    tri_t = tri_ref[...]

    @pl.when(qi == 0)
    def _build_vt():
        for j in range(SB_PAIRS):
            for t in range(vt_sc.shape[1]):
                blk = v_ref[0, t * TK:(t + 1) * TK, j * LANES:(j + 1) * LANES]
                vt_sc[j, t] = blk.astype(jnp.float32).T.astype(vt_sc.dtype)

    qq = [_stack_subheads(q_ref[0, :, j * LANES:(j + 1) * LANES]) for j in range(SB_PAIRS)]

    def tile_update(kt, diagonal):
        ks = pl.multiple_of(kt * TK, TK)
        z = [lax.dot_general(k_ref[0, pl.ds(ks, TK), j * LANES:(j + 1) * LANES], qq[j], NT_DIMS,
                             preferred_element_type=jnp.float32) for j in range(SB_PAIRS)]
        csum = []
        for j in range(SB_PAIRS):
            sp = _softplus(z[j])
            c = sp * mask_ref[...] if diagonal else sp
            hi, lo = _split_hi_lo(c)
            csum.append(jnp.dot(tri_t, hi, preferred_element_type=jnp.float32)
                        + jnp.dot(tri_t, lo, preferred_element_type=jnp.float32))
        r_min = None
        for j in range(SB_PAIRS):
            if diagonal:
                a = jnp.exp(z[j] - csum[j]) * mask_ref[...]
                r_new = csum[j][0:1, :]
            else:
                r = r_sc[j][0:1, :]
                a = jnp.exp(z[j] - csum[j] - r)
                r_new = r + csum[j][0:1, :]
            av = jnp.dot(vt_sc[j, kt], a.astype(jnp.bfloat16), preferred_element_type=jnp.float32)
            acc_sc[j] = av if diagonal else acc_sc[j] + av
            r_sc[j] = jnp.broadcast_to(r_new, (8, 2 * TQ))
            r_min = jnp.min(r_new) if r_min is None else jnp.minimum(r_min, jnp.min(r_new))
        return r_min

    tile_update(qi, True)

    def more(carry):
        i, r_min = carry
        return jnp.logical_and(i < qi, r_min < SB_STOP_COST)

    def step(carry):
        i, _ = carry
        return i + 1, tile_update(qi - 1 - i, False)

    lax.while_loop(more, step, (jnp.int32(0), jnp.float32(0.0)))

    lane = lax.broadcasted_iota(jnp.int32, (TQ, LANES), 1)
    first = lane < SB_DIM
    for j in range(SB_PAIRS):
        acc = acc_sc[j].T
        o = jnp.where(first, acc[:TQ], acc[TQ:])
        o2 = o * o
        ss_a = jnp.sum(jnp.where(first, o2, 0.0), axis=-1, keepdims=True)
        ss_b = jnp.sum(jnp.where(first, 0.0, o2), axis=-1, keepdims=True)
        ms = jnp.where(first, ss_a, ss_b) * (1.0 / SB_DIM)
        o_ref[0, :, j * LANES:(j + 1) * LANES] = (
            (o * lax.rsqrt(ms + EPS)) * w_ref[...]).astype(o_ref.dtype)


def _sb_attn(proj, tri_t, mask_t, w_out_norm):
    b, s, _ = proj.shape
    nq = s // TQ
    wide = SB_PAIRS * LANES
    return pl.pallas_call(
        _sb_attn_kernel,
        grid=(b, nq),
        in_specs=[
            pl.BlockSpec((1, TQ, wide), lambda bi, qi: (bi, qi, QS_BLK // SB_PAIRS)),
            pl.BlockSpec((1, s, wide), lambda bi, qi: (bi, 0, KS_BLK // SB_PAIRS)),
            pl.BlockSpec((1, s, wide), lambda bi, qi: (bi, 0, VS_BLK // SB_PAIRS)),
            pl.BlockSpec((TK, TK), lambda bi, qi: (0, 0)),
            pl.BlockSpec((TK, 2 * TQ), lambda bi, qi: (0, 0)),
            pl.BlockSpec((1, LANES), lambda bi, qi: (0, 0)),
        ],
        out_specs=pl.BlockSpec((1, TQ, wide), lambda bi, qi: (bi, qi, 0)),
        out_shape=jax.ShapeDtypeStruct((b, s, SB_HEADS * SB_DIM), jnp.bfloat16),
        scratch_shapes=[
            pltpu.VMEM((SB_PAIRS, s // TK, LANES, TK), jnp.bfloat16),
            pltpu.VMEM((SB_PAIRS, 8, 2 * TQ), jnp.float32),
            pltpu.VMEM((SB_PAIRS, LANES, 2 * TQ), jnp.float32),
        ],
        compiler_params=pltpu.CompilerParams(
            dimension_semantics=("arbitrary", "arbitrary"),
            vmem_limit_bytes=VMEM_LIMIT),
        name="sb_attn",
    )(proj, proj, proj, tri_t, mask_t, w_out_norm)


def _out_ffn_kernel(x_ref, yd_ref, ys_ref, woa_ref, wob_ref, w2_ref, wg_ref, wu_ref, wd_ref,
                    o_ref):
    h1 = (x_ref[...]
          + jnp.dot(yd_ref[...], woa_ref[...], preferred_element_type=jnp.float32)
          + jnp.dot(ys_ref[...], wob_ref[...], preferred_element_type=jnp.float32))
    ms = jnp.mean(h1 * h1, axis=-1, keepdims=True)
    u2 = ((h1 * lax.rsqrt(ms + EPS)) * w2_ref[...]).astype(jnp.bfloat16)
    ffn = None
    for c in range(D_FF // FF_CHUNK):
        sl = slice(c * FF_CHUNK, (c + 1) * FF_CHUNK)
        gate = jnp.dot(u2, wg_ref[:, sl], preferred_element_type=jnp.float32)
        up = jnp.dot(u2, wu_ref[:, sl], preferred_element_type=jnp.float32)
        act = (gate * (1.0 / (1.0 + jnp.exp(-gate))) * up).astype(jnp.bfloat16)
        part = jnp.dot(act, wd_ref[sl, :], preferred_element_type=jnp.float32)
        ffn = part if ffn is None else ffn + part
    o_ref[...] = h1 + ffn


def _out_ffn(x2d, yd, ys, woa, wob, norm2_w, wg, wu, wd):
    n = x2d.shape[0]
    const = lambda i: (0, 0)
    half = DIFF_HEADS * DIFF_V_DIM
    resident = partial(pl.BlockSpec, index_map=const, pipeline_mode=pl.Buffered(1))
    return pl.pallas_call(
        _out_ffn_kernel,
        grid=(n // TM_PROJ,),
        in_specs=[
            pl.BlockSpec((TM_PROJ, D_MODEL), lambda i: (i, 0)),
            pl.BlockSpec((TM_PROJ, half), lambda i: (i, 0)),
            pl.BlockSpec((TM_PROJ, half), lambda i: (i, 0)),
            resident((half, D_MODEL)),
            resident((half, D_MODEL)),
            pl.BlockSpec((1, D_MODEL), const),
            resident((D_MODEL, D_FF)),
            resident((D_MODEL, D_FF)),
            resident((D_FF, D_MODEL)),
        ],
        out_specs=pl.BlockSpec((TM_PROJ, D_MODEL), lambda i: (i, 0)),
        out_shape=jax.ShapeDtypeStruct((n, D_MODEL), jnp.float32),
        compiler_params=pltpu.CompilerParams(
            dimension_semantics=("arbitrary",), vmem_limit_bytes=VMEM_LIMIT),
        name="out_ffn",
    )(x2d, yd, ys, woa, wob, norm2_w, wg, wu, wd)


def _diff_bias_tiles(rel_bias):
    qq = jnp.arange(TQD, dtype=jnp.int32)[:, None]
    kk = jnp.arange(TKD, dtype=jnp.int32)[None, :]
    width = TQD + TKD + 1
    tiles = []
    for koff in (0, -TKD):
        kpos = kk + koff
        table = rel_bias.astype(jnp.float32)[
            _t5_bucket(jnp.arange(width, dtype=jnp.int32) - TQD + koff)].T
        toep = jnp.tile(table, (1, TQD))[:, :TQD * (width - 1)].reshape(DIFF_HEADS, TQD, width - 1)
        bias = toep[:, :, TQD:TQD + TKD]
        allowed = jnp.floor_divide(kpos, CHUNK) <= jnp.floor_divide(qq, CHUNK)
        tiles.append(jnp.where(allowed[None], bias, MASK_VALUE))
    return jnp.swapaxes(jnp.stack(tiles, axis=0), 2, 3)


def kernel(x, norm1_w, w_in, q_norm_w, k_norm_w, lambda_q1, lambda_k1, lambda_q2, lambda_k2,
           diff_out_norm_w, sb_out_norm_w, w_out, norm2_w, w_gate, w_up, w_down, rel_bias):
    b, s, d = x.shape
    assert (b, s, d) == (8, 4096, D_MODEL) and w_in.shape[0] == 1
    f32, bf16 = jnp.float32, jnp.bfloat16
    x2d = x.reshape(b * s, d)

    qkw = jnp.concatenate([jnp.tile(q_norm_w[0].astype(f32), 2 * DIFF_HEADS)
                           * (DIFF_QK_DIM ** -0.5 * LOG2E),
                           jnp.tile(k_norm_w[0].astype(f32), 2 * DIFF_HEADS)])[None, :]
    grp = jnp.arange(LANES) // DIFF_QK_DIM
    gmat = jnp.where(grp[:, None] == grp[None, :], 1.0 / DIFF_QK_DIM, 0.0).astype(bf16)
    lam = (jnp.exp(jnp.sum(lambda_q1[0].astype(f32) * lambda_k1[0].astype(f32)))
           - jnp.exp(jnp.sum(lambda_q2[0].astype(f32) * lambda_k2[0].astype(f32)))
           + LAMBDA_INIT).reshape(1)
    bias_tiles = _diff_bias_tiles(rel_bias) * LOG2E
    cfar = rel_bias.astype(f32)[_t5_bucket(jnp.int32(-(TKD + 1)))] * LOG2E
    jj = jnp.arange(TK)
    tri_t = (jj[None, :] >= jj[:, None]).astype(bf16)
    causal_t = (jj[:, None] < jnp.arange(TQ)[None, :]).astype(f32)
    sb_mask_t = jnp.concatenate([causal_t, causal_t], axis=1)

    proj = _in_proj(x2d, norm1_w[0].astype(f32)[None, :], w_in[0].astype(bf16), qkw, gmat)
    proj = proj.reshape(b, s, IN_COLS)
    y_diff = _diff_attn(proj, bias_tiles, cfar, lam, diff_out_norm_w[0].astype(f32)[None, :])
    y_sb = _sb_attn(proj, tri_t, sb_mask_t, jnp.tile(sb_out_norm_w[0].astype(f32), 2)[None, :])

    half = DIFF_HEADS * DIFF_V_DIM
    wo = w_out[0].astype(bf16)
    out = _out_ffn(x2d, y_diff.reshape(b * s, half), y_sb.reshape(b * s, half),
                   wo[:half], wo[half:], norm2_w[0].astype(f32)[None, :],
                   w_gate[0].astype(bf16), w_up[0].astype(bf16), w_down[0].astype(bf16))
    return out.reshape(b, s, d)
```

```python
import math
from functools import partial

import jax
import jax.numpy as jnp
from jax import lax
from jax.experimental import pallas as pl
from jax.experimental.pallas import tpu as pltpu

D_MODEL = 1024
CHUNK = 64
DIFF_QK_DIM = 64
DIFF_V_DIM = 128
DIFF_HEADS = 4
SB_DIM = 64
SB_HEADS = 8
NUM_BUCKETS = 32
MAX_DISTANCE = 128
D_FF = 2816
EPS = 1e-6
IN_COLS = 3072
LAMBDA_INIT = 0.8 - 0.6 * math.exp(-0.3 * 0)
LANES = 128

QD_BLK, KD_BLK, VD_BLK = 0, 4, 8
QS_BLK, KS_BLK, VS_BLK = 12, 16, 20

TM_PROJ = 512
COL_CHUNK = 512
TQD = 512
TKD = 512
HEADS_PER_STEP = 2
ONES_ROWS = 16
SCORE_LOOKAHEAD = 2
LOG2E = math.log2(math.e)
TQ = 256
TK = 256
SB_PAIRS = SB_HEADS // 2
FF_CHUNK = 256
MASK_VALUE = -1e30
SB_STOP_COST = 104.0
VMEM_LIMIT = 56 * 1024 * 1024

NT_DIMS = (((1,), (1,)), ((), ()))


def _t5_bucket(rel):
    nb = NUM_BUCKETS // 2
    max_exact = nb // 2
    ret = (rel > 0).astype(jnp.int32) * nb
    n = jnp.abs(rel)
    nf = jnp.maximum(n, 1).astype(jnp.float32)
    large = max_exact + (jnp.log(nf / max_exact) / math.log(MAX_DISTANCE / max_exact)
                         * (nb - max_exact)).astype(jnp.int32)
    large = jnp.minimum(large, nb - 1)
    return ret + jnp.where(n < max_exact, n, large)


def _split_hi_lo(x):
    hi = x.astype(jnp.bfloat16)
    lo = (x - hi.astype(jnp.float32)).astype(jnp.bfloat16)
    return hi, lo


def _in_proj_kernel(x_ref, w1_ref, win_ref, qkw_ref, g_ref, o_ref):
    x = x_ref[...]
    ms = jnp.mean(x * x, axis=-1, keepdims=True)
    u = ((x * lax.rsqrt(ms + EPS)) * w1_ref[...]).astype(jnp.bfloat16)
    g = g_ref[...]
    for c in range(IN_COLS // COL_CHUNK):
        lo_c, hi_c = c * COL_CHUNK, (c + 1) * COL_CHUNK
        p = jnp.dot(u, win_ref[:, lo_c:hi_c], preferred_element_type=jnp.float32)
        if hi_c <= 2 * DIFF_HEADS * 2 * DIFF_QK_DIM:
            for j in range(COL_CHUNK // LANES):
                sl = slice(j * LANES, (j + 1) * LANES)
                pj = p[:, sl]
                hi, lo = _split_hi_lo(pj * pj)
                msq = (jnp.dot(hi, g, preferred_element_type=jnp.float32)
                       + jnp.dot(lo, g, preferred_element_type=jnp.float32))
                w = qkw_ref[:, lo_c + j * LANES: lo_c + (j + 1) * LANES]
                o_ref[:, lo_c + j * LANES: lo_c + (j + 1) * LANES] = (
                    (pj * lax.rsqrt(msq + EPS)) * w).astype(jnp.bfloat16)
        else:
            if lo_c == QS_BLK * LANES:
                p = p * (SB_DIM ** -0.5)
            o_ref[:, lo_c:hi_c] = p.astype(jnp.bfloat16)


def _in_proj(x2d, norm1_w, w_in_bf, qkw, gmat):
    n = x2d.shape[0]
    const = lambda i: (0, 0)
    return pl.pallas_call(
        _in_proj_kernel,
        grid=(n // TM_PROJ,),
        in_specs=[
            pl.BlockSpec((TM_PROJ, D_MODEL), lambda i: (i, 0)),
            pl.BlockSpec((1, D_MODEL), const),
            pl.BlockSpec((D_MODEL, IN_COLS), const, pipeline_mode=pl.Buffered(1)),
            pl.BlockSpec((1, 2 * DIFF_HEADS * 2 * DIFF_QK_DIM), const),
            pl.BlockSpec((LANES, LANES), const),
        ],
        out_specs=pl.BlockSpec((TM_PROJ, IN_COLS), lambda i: (i, 0)),
        out_shape=jax.ShapeDtypeStruct((n, IN_COLS), jnp.bfloat16),
        compiler_params=pltpu.CompilerParams(
            dimension_semantics=("arbitrary",), vmem_limit_bytes=VMEM_LIMIT),
        name="in_proj",
    )(x2d, norm1_w, w_in_bf, qkw, gmat)


def _stack_subheads(q):
    lane = lax.broadcasted_iota(jnp.int32, q.shape, 1)
    zero = jnp.zeros_like(q)
    return jnp.concatenate([jnp.where(lane < SB_DIM, q, zero),
                            jnp.where(lane >= SB_DIM, q, zero)], axis=0)


def _diff_attn_kernel(cfar_ref, lam_ref, q_ref, k_ref, v_ref, bdiag_ref, bsub_ref, w_ref, o_ref,
                      vt_sc, m_sc, acc_sc):
    hp = pl.program_id(1)
    qi = pl.program_id(2)

    @pl.when(qi == 0)
    def _build_vt():
        ones = jnp.ones((ONES_ROWS, TKD), vt_sc.dtype)
        for hh in range(HEADS_PER_STEP):
            for t in range(vt_sc.shape[1]):
                blk = v_ref[0, t * TKD:(t + 1) * TKD, hh * LANES:(hh + 1) * LANES]
                vt_sc[hh, t, :DIFF_V_DIM, :] = blk.astype(jnp.float32).T.astype(vt_sc.dtype)
                vt_sc[hh, t, DIFF_V_DIM:, :] = ones

    lane = lax.broadcasted_iota(jnp.int32, (TQD, LANES), 1)
    zero = jnp.zeros((TQD, LANES), q_ref.dtype)
    q_sub = []
    for hh in range(HEADS_PER_STEP):
        q = q_ref[0, :, hh * LANES:(hh + 1) * LANES]
        q_sub.append(jnp.where(lane < DIFF_QK_DIM, q, zero))
        q_sub.append(jnp.where(lane >= DIFF_QK_DIM, q, zero))

    n_maps = 2 * HEADS_PER_STEP

    def run_tiles(tiles):
        def scores(kt, idx):
            ks = pl.multiple_of(kt * TKD, TKD)
            k = k_ref[0, pl.ds(ks, TKD), (idx // 2) * LANES:(idx // 2 + 1) * LANES]
            return lax.dot_general(k, q_sub[idx], NT_DIMS, preferred_element_type=jnp.float32)

        def weights(s, idx, bias_ref, cfar, first):
            hh = idx // 2
            if first:
                s = s + bias_ref[0, hh]
                m_new = jnp.max(s, axis=0, keepdims=True)
                m_sc[idx] = jnp.broadcast_to(m_new, (8, TQD))
                return jnp.exp2(s - m_new).astype(jnp.bfloat16), None
            m_old = m_sc[idx][0:1, :]
            if bias_ref is None:
                m_new = jnp.maximum(m_old, jnp.max(s, axis=0, keepdims=True) + cfar[hh])
                shift = m_new - cfar[hh]
            else:
                s = s + bias_ref[0, hh]
                m_new = jnp.maximum(m_old, jnp.max(s, axis=0, keepdims=True))
                shift = m_new
            m_sc[idx] = jnp.broadcast_to(m_new, (8, TQD))
            return jnp.exp2(s - shift).astype(jnp.bfloat16), jnp.exp2(m_old - m_new)

        def accumulate(kt, idx, p, alpha):
            pv = jnp.dot(vt_sc[idx // 2, kt], p, preferred_element_type=jnp.float32)
            acc_sc[idx] = pv if alpha is None else alpha * acc_sc[idx] + pv

        work = [(kt, idx, bias_ref, cfar, first)
                for (kt, bias_ref, cfar, first) in tiles for idx in range(n_maps)]
        s_next = [scores(w[0], w[1]) for w in work[:SCORE_LOOKAHEAD]]
        pending = None
        for n, (kt, idx, bias_ref, cfar, first) in enumerate(work):
            p, alpha = weights(s_next.pop(0), idx, bias_ref, cfar, first)
            if n + SCORE_LOOKAHEAD < len(work):
                nxt = work[n + SCORE_LOOKAHEAD]
                s_next.append(scores(nxt[0], nxt[1]))
            if pending is not None:
                accumulate(*pending)
            pending = (kt, idx, p, alpha)
        accumulate(*pending)

    cfar = [cfar_ref[HEADS_PER_STEP * hp + hh] for hh in range(HEADS_PER_STEP)]

    @pl.when(qi == 0)
    def _first_query_tile():
        run_tiles([(qi, bdiag_ref, None, True)])

    @pl.when(qi >= 1)
    def _near_tiles():
        run_tiles([(qi, bdiag_ref, None, True), (qi - 1, bsub_ref, None, False)])

    n_far = jnp.maximum(qi - 1, 0)

    def far_pair(i, carry):
        run_tiles([(2 * i, None, cfar, False), (2 * i + 1, None, cfar, False)])
        return carry

    lax.fori_loop(0, n_far // 2, far_pair, 0)

    @pl.when(n_far % 2 == 1)
    def _last_far_tile():
        run_tiles([(n_far - 1, None, cfar, False)])

    for hh in range(HEADS_PER_STEP):
        acc0 = acc_sc[2 * hh]
        acc1 = acc_sc[2 * hh + 1]
        o_t = (acc0[:DIFF_V_DIM] / acc0[DIFF_V_DIM:DIFF_V_DIM + 1]
               - lam_ref[0] * (acc1[:DIFF_V_DIM] / acc1[DIFF_V_DIM:DIFF_V_DIM + 1]))
        o = o_t.T
        ms = jnp.mean(o * o, axis=-1, keepdims=True)
        y = ((o * lax.rsqrt(ms + EPS)) * w_ref[...]) * (1.0 - LAMBDA_INIT)
        o_ref[0, :, hh * LANES:(hh + 1) * LANES] = y.astype(o_ref.dtype)


def _diff_attn(proj, bias_tiles, cfar, lam, w_out_norm):
    b, s, _ = proj.shape
    nq = s // TQD
    hps = HEADS_PER_STEP
    wide = hps * LANES
    smem = pl.BlockSpec(memory_space=pltpu.SMEM)
    return pl.pallas_call(
        _diff_attn_kernel,
        grid=(b, DIFF_HEADS // hps, nq),
        in_specs=[
            smem, smem,
            pl.BlockSpec((1, TQD, wide), lambda bi, hp, qi: (bi, qi, QD_BLK // hps + hp)),
            pl.BlockSpec((1, s, wide), lambda bi, hp, qi: (bi, 0, KD_BLK // hps + hp)),
            pl.BlockSpec((1, s, wide), lambda bi, hp, qi: (bi, 0, VD_BLK // hps + hp)),
            pl.BlockSpec((1, hps, TQD, TKD), lambda bi, hp, qi: (0, hp, 0, 0)),
            pl.BlockSpec((1, hps, TQD, TKD), lambda bi, hp, qi: (1, hp, 0, 0)),
            pl.BlockSpec((1, DIFF_V_DIM), lambda bi, hp, qi: (0, 0)),
        ],
        out_specs=pl.BlockSpec((1, TQD, wide), lambda bi, hp, qi: (bi, qi, hp)),
        out_shape=jax.ShapeDtypeStruct((b, s, DIFF_HEADS * DIFF_V_DIM), jnp.bfloat16),
        scratch_shapes=[
            pltpu.VMEM((hps, s // TKD, DIFF_V_DIM + ONES_ROWS, TKD), jnp.bfloat16),
            pltpu.VMEM((2 * hps, 8, TQD), jnp.float32),
            pltpu.VMEM((2 * hps, DIFF_V_DIM + ONES_ROWS, TQD), jnp.float32),
        ],
        compiler_params=pltpu.CompilerParams(
            dimension_semantics=("arbitrary", "arbitrary", "arbitrary"),
            vmem_limit_bytes=VMEM_LIMIT),
        name="diff_attn",
    )(cfar, lam, proj, proj, proj, bias_tiles, bias_tiles, w_out_norm)


def _softplus(z):
    return jnp.maximum(z, 0.0) + jnp.log(1.0 + jnp.exp(-jnp.abs(z)))


def _sb_attn_kernel(q_ref, k_ref, v_ref, tri_ref, mask_ref, w_ref, o_ref, vt_sc, r_sc, acc_sc):
    qi = pl.program_id(1)
    tri_t = tri_ref[...]

    @pl.when(qi == 0)
    def _build_vt():
        for j in range(SB_PAIRS):
            for t in range(vt_sc.shape[1]):
                blk = v_ref[0, t * TK:(t + 1) * TK, j * LANES:(j + 1) * LANES]
                vt_sc[j, t] = blk.astype(jnp.float32).T.astype(vt_sc.dtype)

    qq = [_stack_subheads(q_ref[0, :, j * LANES:(j + 1) * LANES]) for j in range(SB_PAIRS)]

    def tile_update(kt, diagonal):
        ks = pl.multiple_of(kt * TK, TK)
        z = [lax.dot_general(k_ref[0, pl.ds(ks, TK), j * LANES:(j + 1) * LANES], qq[j], NT_DIMS,
                             preferred_element_type=jnp.float32) for j in range(SB_PAIRS)]
        csum = []
        for j in range(SB_PAIRS):
            sp = _softplus(z[j])
            c = sp * mask_ref[...] if diagonal else sp
            hi, lo = _split_hi_lo(c)
            csum.append(jnp.dot(tri_t, hi, preferred_element_type=jnp.float32)
                        + jnp.dot(tri_t, lo, preferred_element_type=jnp.float32))
        r_min = None
        for j in range(SB_PAIRS):
            if diagonal:
                a = jnp.exp(z[j] - csum[j]) * mask_ref[...]
                r_new = csum[j][0:1, :]
            else:
                r = r_sc[j][0:1, :]
                a = jnp.exp(z[j] - csum[j] - r)
                r_new = r + csum[j][0:1, :]
            av = jnp.dot(vt_sc[j, kt], a.astype(jnp.bfloat16), preferred_element_type=jnp.float32)
            acc_sc[j] = av if diagonal else acc_sc[j] + av
            r_sc[j] = jnp.broadcast_to(r_new, (8, 2 * TQ))
            r_min = jnp.min(r_new) if r_min is None else jnp.minimum(r_min, jnp.min(r_new))
        return r_min

    tile_update(qi, True)

    def more(carry):
        i, r_min = carry
        return jnp.logical_and(i < qi, r_min < SB_STOP_COST)

    def step(carry):
        i, _ = carry
        return i + 1, tile_update(qi - 1 - i, False)

    lax.while_loop(more, step, (jnp.int32(0), jnp.float32(0.0)))

    lane = lax.broadcasted_iota(jnp.int32, (TQ, LANES), 1)
    first = lane < SB_DIM
    for j in range(SB_PAIRS):
        acc = acc_sc[j].T
        o = jnp.where(first, acc[:TQ], acc[TQ:])
        o2 = o * o
        ss_a = jnp.sum(jnp.where(first, o2, 0.0), axis=-1, keepdims=True)
        ss_b = jnp.sum(jnp.where(first, 0.0, o2), axis=-1, keepdims=True)
        ms = jnp.where(first, ss_a, ss_b) * (1.0 / SB_DIM)
        o_ref[0, :, j * LANES:(j + 1) * LANES] = (
            (o * lax.rsqrt(ms + EPS)) * w_ref[...]).astype(o_ref.dtype)


def _sb_attn(proj, tri_t, mask_t, w_out_norm):
    b, s, _ = proj.shape
    nq = s // TQ
    wide = SB_PAIRS * LANES
    return pl.pallas_call(
        _sb_attn_kernel,
        grid=(b, nq),
        in_specs=[
            pl.BlockSpec((1, TQ, wide), lambda bi, qi: (bi, qi, QS_BLK // SB_PAIRS)),
            pl.BlockSpec((1, s, wide), lambda bi, qi: (bi, 0, KS_BLK // SB_PAIRS)),
            pl.BlockSpec((1, s, wide), lambda bi, qi: (bi, 0, VS_BLK // SB_PAIRS)),
            pl.BlockSpec((TK, TK), lambda bi, qi: (0, 0)),
            pl.BlockSpec((TK, 2 * TQ), lambda bi, qi: (0, 0)),
            pl.BlockSpec((1, LANES), lambda bi, qi: (0, 0)),
        ],
        out_specs=pl.BlockSpec((1, TQ, wide), lambda bi, qi: (bi, qi, 0)),
        out_shape=jax.ShapeDtypeStruct((b, s, SB_HEADS * SB_DIM), jnp.bfloat16),
        scratch_shapes=[
            pltpu.VMEM((SB_PAIRS, s // TK, LANES, TK), jnp.bfloat16),
            pltpu.VMEM((SB_PAIRS, 8, 2 * TQ), jnp.float32),
            pltpu.VMEM((SB_PAIRS, LANES, 2 * TQ), jnp.float32),
        ],
        compiler_params=pltpu.CompilerParams(
            dimension_semantics=("arbitrary", "arbitrary"),
            vmem_limit_bytes=VMEM_LIMIT),
        name="sb_attn",
    )(proj, proj, proj, tri_t, mask_t, w_out_norm)


def _out_ffn_kernel(x_ref, yd_ref, ys_ref, woa_ref, wob_ref, w2_ref, wg_ref, wu_ref, wd_ref,
                    o_ref):
    h1 = (x_ref[...]
          + jnp.dot(yd_ref[...], woa_ref[...], preferred_element_type=jnp.float32)
          + jnp.dot(ys_ref[...], wob_ref[...], preferred_element_type=jnp.float32))
    ms = jnp.mean(h1 * h1, axis=-1, keepdims=True)
    u2 = ((h1 * lax.rsqrt(ms + EPS)) * w2_ref[...]).astype(jnp.bfloat16)
    ffn = None
    for c in range(D_FF // FF_CHUNK):
        sl = slice(c * FF_CHUNK, (c + 1) * FF_CHUNK)
        gate = jnp.dot(u2, wg_ref[:, sl], preferred_element_type=jnp.float32)
        up = jnp.dot(u2, wu_ref[:, sl], preferred_element_type=jnp.float32)
        act = (gate * (1.0 / (1.0 + jnp.exp(-gate))) * up).astype(jnp.bfloat16)
        part = jnp.dot(act, wd_ref[sl, :], preferred_element_type=jnp.float32)
        ffn = part if ffn is None else ffn + part
    o_ref[...] = h1 + ffn


def _out_ffn(x2d, yd, ys, woa, wob, norm2_w, wg, wu, wd):
    n = x2d.shape[0]
    const = lambda i: (0, 0)
    half = DIFF_HEADS * DIFF_V_DIM
    resident = partial(pl.BlockSpec, index_map=const, pipeline_mode=pl.Buffered(1))
    return pl.pallas_call(
        _out_ffn_kernel,
        grid=(n // TM_PROJ,),
        in_specs=[
            pl.BlockSpec((TM_PROJ, D_MODEL), lambda i: (i, 0)),
            pl.BlockSpec((TM_PROJ, half), lambda i: (i, 0)),
            pl.BlockSpec((TM_PROJ, half), lambda i: (i, 0)),
            resident((half, D_MODEL)),
            resident((half, D_MODEL)),
            pl.BlockSpec((1, D_MODEL), const),
            resident((D_MODEL, D_FF)),
            resident((D_MODEL, D_FF)),
            resident((D_FF, D_MODEL)),
        ],
        out_specs=pl.BlockSpec((TM_PROJ, D_MODEL), lambda i: (i, 0)),
        out_shape=jax.ShapeDtypeStruct((n, D_MODEL), jnp.float32),
        compiler_params=pltpu.CompilerParams(
            dimension_semantics=("arbitrary",), vmem_limit_bytes=VMEM_LIMIT),
        name="out_ffn",
    )(x2d, yd, ys, woa, wob, norm2_w, wg, wu, wd)


def _diff_bias_tiles(rel_bias):
    qq = jnp.arange(TQD, dtype=jnp.int32)[:, None]
    kk = jnp.arange(TKD, dtype=jnp.int32)[None, :]
    width = TQD + TKD + 1
    tiles = []
    for koff in (0, -TKD):
        kpos = kk + koff
        table = rel_bias.astype(jnp.float32)[
            _t5_bucket(jnp.arange(width, dtype=jnp.int32) - TQD + koff)].T
        toep = jnp.tile(table, (1, TQD))[:, :TQD * (width - 1)].reshape(DIFF_HEADS, TQD, width - 1)
        bias = toep[:, :, TQD:TQD + TKD]
        allowed = jnp.floor_divide(kpos, CHUNK) <= jnp.floor_divide(qq, CHUNK)
        tiles.append(jnp.where(allowed[None], bias, MASK_VALUE))
    return jnp.swapaxes(jnp.stack(tiles, axis=0), 2, 3)


def kernel(x, norm1_w, w_in, q_norm_w, k_norm_w, lambda_q1, lambda_k1, lambda_q2, lambda_k2,
           diff_out_norm_w, sb_out_norm_w, w_out, norm2_w, w_gate, w_up, w_down, rel_bias):
    b, s, d = x.shape
    assert (b, s, d) == (8, 4096, D_MODEL) and w_in.shape[0] == 1
    f32, bf16 = jnp.float32, jnp.bfloat16
    x2d = x.reshape(b * s, d)

    qkw = jnp.concatenate([jnp.tile(q_norm_w[0].astype(f32), 2 * DIFF_HEADS)
                           * (DIFF_QK_DIM ** -0.5 * LOG2E),
                           jnp.tile(k_norm_w[0].astype(f32), 2 * DIFF_HEADS)])[None, :]
    grp = jnp.arange(LANES) // DIFF_QK_DIM
    gmat = jnp.where(grp[:, None] == grp[None, :], 1.0 / DIFF_QK_DIM, 0.0).astype(bf16)
    lam = (jnp.exp(jnp.sum(lambda_q1[0].astype(f32) * lambda_k1[0].astype(f32)))
           - jnp.exp(jnp.sum(lambda_q2[0].astype(f32) * lambda_k2[0].astype(f32)))
           + LAMBDA_INIT).reshape(1)
    bias_tiles = _diff_bias_tiles(rel_bias) * LOG2E
    cfar = rel_bias.astype(f32)[_t5_bucket(jnp.int32(-(TKD + 1)))] * LOG2E
    jj = jnp.arange(TK)
    tri_t = (jj[None, :] >= jj[:, None]).astype(bf16)
    causal_t = (jj[:, None] < jnp.arange(TQ)[None, :]).astype(f32)
    sb_mask_t = jnp.concatenate([causal_t, causal_t], axis=1)

    proj = _in_proj(x2d, norm1_w[0].astype(f32)[None, :], w_in[0].astype(bf16), qkw, gmat)
    proj = proj.reshape(b, s, IN_COLS)
    y_diff = _diff_attn(proj, bias_tiles, cfar, lam, diff_out_norm_w[0].astype(f32)[None, :])
    y_sb = _sb_attn(proj, tri_t, sb_mask_t, jnp.tile(sb_out_norm_w[0].astype(f32), 2)[None, :])

    half = DIFF_HEADS * DIFF_V_DIM
    wo = w_out[0].astype(bf16)
    out = _out_ffn(x2d, y_diff.reshape(b * s, half), y_sb.reshape(b * s, half),
                   wo[:half], wo[half:], norm2_w[0].astype(f32)[None, :],
                   w_gate[0].astype(bf16), w_up[0].astype(bf16), w_down[0].astype(bf16))
    return out.reshape(b, s, d)
```

```python
import math
from functools import partial

import jax
import jax.numpy as jnp
from jax import lax
from jax.experimental import pallas as pl
from jax.experimental.pallas import tpu as pltpu

D_MODEL = 1024
CHUNK = 64
DIFF_QK_DIM = 64
DIFF_V_DIM = 128
DIFF_HEADS = 4
SB_DIM = 64
SB_HEADS = 8
NUM_BUCKETS = 32
MAX_DISTANCE = 128
D_FF = 2816
EPS = 1e-6
IN_COLS = 3072
LAMBDA_INIT = 0.8 - 0.6 * math.exp(-0.3 * 0)
LANES = 128

QD_BLK, KD_BLK, VD_BLK = 0, 4, 8
QS_BLK, KS_BLK, VS_BLK = 12, 16, 20

TM_PROJ = 512
COL_CHUNK = 512
GROUP_SLAB = 256
TQD = 512
TKD = 512
HEADS_PER_STEP = 2
ONES_ROWS = 16
SCORE_LOOKAHEAD = 2
LOG2E = math.log2(math.e)
TQ = 256
TK = 256
SB_PAIRS = SB_HEADS // 2
FF_CHUNK = 256
MASK_VALUE = -1e30
SB_STOP_COST = 104.0
VMEM_LIMIT = 56 * 1024 * 1024

NT_DIMS = (((1,), (1,)), ((), ()))


def _t5_bucket(rel):
    nb = NUM_BUCKETS // 2
    max_exact = nb // 2
    ret = (rel > 0).astype(jnp.int32) * nb
    n = jnp.abs(rel)
    nf = jnp.maximum(n, 1).astype(jnp.float32)
    large = max_exact + (jnp.log(nf / max_exact) / math.log(MAX_DISTANCE / max_exact)
                         * (nb - max_exact)).astype(jnp.int32)
    large = jnp.minimum(large, nb - 1)
    return ret + jnp.where(n < max_exact, n, large)


def _split_hi_lo(x):
    hi = x.astype(jnp.bfloat16)
    lo = (x - hi.astype(jnp.float32)).astype(jnp.bfloat16)
    return hi, lo


def _in_proj_kernel(x_ref, w1_ref, win_ref, qkw_ref, g_ref, o_ref):
    x = x_ref[...]
    ms = jnp.mean(x * x, axis=-1, keepdims=True)
    u = ((x * lax.rsqrt(ms + EPS)) * w1_ref[...]).astype(jnp.bfloat16)
    g = g_ref[...]
    for c in range(IN_COLS // COL_CHUNK):
        lo_c, hi_c = c * COL_CHUNK, (c + 1) * COL_CHUNK
        p = jnp.dot(u, win_ref[:, lo_c:hi_c], preferred_element_type=jnp.float32)
        if hi_c <= 2 * DIFF_HEADS * 2 * DIFF_QK_DIM:
            for j in range(COL_CHUNK // GROUP_SLAB):
                sl = slice(j * GROUP_SLAB, (j + 1) * GROUP_SLAB)
                pj = p[:, sl]
                msq = jnp.dot((pj * pj).astype(jnp.bfloat16), g, preferred_element_type=jnp.float32)
                w = qkw_ref[:, lo_c + j * GROUP_SLAB: lo_c + (j + 1) * GROUP_SLAB]
                o_ref[:, lo_c + j * GROUP_SLAB: lo_c + (j + 1) * GROUP_SLAB] = (
                    (pj * lax.rsqrt(msq + EPS)) * w).astype(jnp.bfloat16)
        else:
            if lo_c == QS_BLK * LANES:
                p = p * (SB_DIM ** -0.5)
            o_ref[:, lo_c:hi_c] = p.astype(jnp.bfloat16)


def _in_proj(x2d, norm1_w, w_in_bf, qkw, gmat):
    n = x2d.shape[0]
    const = lambda i: (0, 0)
    return pl.pallas_call(
        _in_proj_kernel,
        grid=(n // TM_PROJ,),
        in_specs=[
            pl.BlockSpec((TM_PROJ, D_MODEL), lambda i: (i, 0)),
            pl.BlockSpec((1, D_MODEL), const),
            pl.BlockSpec((D_MODEL, IN_COLS), const, pipeline_mode=pl.Buffered(1)),
            pl.BlockSpec((1, 2 * DIFF_HEADS * 2 * DIFF_QK_DIM), const),
            pl.BlockSpec((GROUP_SLAB, GROUP_SLAB), const),
        ],
        out_specs=pl.BlockSpec((TM_PROJ, IN_COLS), lambda i: (i, 0)),
        out_shape=jax.ShapeDtypeStruct((n, IN_COLS), jnp.bfloat16),
        compiler_params=pltpu.CompilerParams(
            dimension_semantics=("arbitrary",), vmem_limit_bytes=VMEM_LIMIT),
        name="in_proj",
    )(x2d, norm1_w, w_in_bf, qkw, gmat)


def _stack_subheads(q):
    lane = lax.broadcasted_iota(jnp.int32, q.shape, 1)
    zero = jnp.zeros_like(q)
    return jnp.concatenate([jnp.where(lane < SB_DIM, q, zero),
                            jnp.where(lane >= SB_DIM, q, zero)], axis=0)


def _diff_attn_kernel(cfar_ref, lam_ref, q_ref, k_ref, v_ref, bdiag_ref, bsub_ref, w_ref, o_ref,
                      vt_sc, m_sc, acc_sc):
    hp = pl.program_id(1)
    qi = pl.program_id(2)

    @pl.when(qi == 0)
    def _build_vt():
        ones = jnp.ones((ONES_ROWS, TKD), vt_sc.dtype)
        for hh in range(HEADS_PER_STEP):
            for t in range(vt_sc.shape[1]):
                blk = v_ref[0, t * TKD:(t + 1) * TKD, hh * LANES:(hh + 1) * LANES]
                vt_sc[hh, t, :DIFF_V_DIM, :] = blk.astype(jnp.float32).T.astype(vt_sc.dtype)
                vt_sc[hh, t, DIFF_V_DIM:, :] = ones

    lane = lax.broadcasted_iota(jnp.int32, (TQD, LANES), 1)
    zero = jnp.zeros((TQD, LANES), q_ref.dtype)
    q_sub = []
    for hh in range(HEADS_PER_STEP):
        q = q_ref[0, :, hh * LANES:(hh + 1) * LANES]
        q_sub.append(jnp.where(lane < DIFF_QK_DIM, q, zero))
        q_sub.append(jnp.where(lane >= DIFF_QK_DIM, q, zero))

    n_maps = 2 * HEADS_PER_STEP

    def run_tiles(tiles):
        def scores(kt, idx):
            ks = pl.multiple_of(kt * TKD, TKD)
            k = k_ref[0, pl.ds(ks, TKD), (idx // 2) * LANES:(idx // 2 + 1) * LANES]
            return lax.dot_general(k, q_sub[idx], NT_DIMS, preferred_element_type=jnp.float32)

        def weights(s, idx, bias_ref, cfar, first):
            hh = idx // 2
            if first:
                s = s + bias_ref[0, hh]
                m_new = jnp.max(s, axis=0, keepdims=True)
                m_sc[idx] = jnp.broadcast_to(m_new, (8, TQD))
                return jnp.exp2(s - m_new).astype(jnp.bfloat16), None
            m_old = m_sc[idx][0:1, :]
            if bias_ref is None:
                m_new = jnp.maximum(m_old, jnp.max(s, axis=0, keepdims=True) + cfar[hh])
                shift = m_new - cfar[hh]
            else:
                s = s + bias_ref[0, hh]
                m_new = jnp.maximum(m_old, jnp.max(s, axis=0, keepdims=True))
                shift = m_new
            m_sc[idx] = jnp.broadcast_to(m_new, (8, TQD))
            return jnp.exp2(s - shift).astype(jnp.bfloat16), jnp.exp2(m_old - m_new)

        def accumulate(kt, idx, p, alpha):
            pv = jnp.dot(vt_sc[idx // 2, kt], p, preferred_element_type=jnp.float32)
            acc_sc[idx] = pv if alpha is None else alpha * acc_sc[idx] + pv

        work = [(kt, idx, bias_ref, cfar, first)
                for (kt, bias_ref, cfar, first) in tiles for idx in range(n_maps)]
        s_next = [scores(w[0], w[1]) for w in work[:SCORE_LOOKAHEAD]]
        pending = None
        for n, (kt, idx, bias_ref, cfar, first) in enumerate(work):
            p, alpha = weights(s_next.pop(0), idx, bias_ref, cfar, first)
            if n + SCORE_LOOKAHEAD < len(work):
                nxt = work[n + SCORE_LOOKAHEAD]
                s_next.append(scores(nxt[0], nxt[1]))
            if pending is not None:
                accumulate(*pending)
            pending = (kt, idx, p, alpha)
        accumulate(*pending)

    cfar = [cfar_ref[HEADS_PER_STEP * hp + hh] for hh in range(HEADS_PER_STEP)]

    @pl.when(qi == 0)
    def _first_query_tile():
        run_tiles([(qi, bdiag_ref, None, True)])

    @pl.when(qi >= 1)
    def _near_tiles():
        run_tiles([(qi, bdiag_ref, None, True), (qi - 1, bsub_ref, None, False)])

    n_far = jnp.maximum(qi - 1, 0)

    def far_pair(i, carry):
        run_tiles([(2 * i, None, cfar, False), (2 * i + 1, None, cfar, False)])
        return carry

    lax.fori_loop(0, n_far // 2, far_pair, 0)

    @pl.when(n_far % 2 == 1)
    def _last_far_tile():
        run_tiles([(n_far - 1, None, cfar, False)])

    for hh in range(HEADS_PER_STEP):
        acc0 = acc_sc[2 * hh]
        acc1 = acc_sc[2 * hh + 1]
        o_t = (acc0[:DIFF_V_DIM] / acc0[DIFF_V_DIM:DIFF_V_DIM + 1]
               - lam_ref[0] * (acc1[:DIFF_V_DIM] / acc1[DIFF_V_DIM:DIFF_V_DIM + 1]))
        o = o_t.T
        ms = jnp.mean(o * o, axis=-1, keepdims=True)
        y = ((o * lax.rsqrt(ms + EPS)) * w_ref[...]) * (1.0 - LAMBDA_INIT)
        o_ref[0, :, hh * LANES:(hh + 1) * LANES] = y.astype(o_ref.dtype)


def _diff_attn(proj, bias_tiles, cfar, lam, w_out_norm):
    b, s, _ = proj.shape
    nq = s // TQD
    hps = HEADS_PER_STEP
    wide = hps * LANES
    smem = pl.BlockSpec(memory_space=pltpu.SMEM)
    return pl.pallas_call(
        _diff_attn_kernel,
        grid=(b, DIFF_HEADS // hps, nq),
        in_specs=[
            smem, smem,
            pl.BlockSpec((1, TQD, wide), lambda bi, hp, qi: (bi, qi, QD_BLK // hps + hp)),
            pl.BlockSpec((1, s, wide), lambda bi, hp, qi: (bi, 0, KD_BLK // hps + hp)),
            pl.BlockSpec((1, s, wide), lambda bi, hp, qi: (bi, 0, VD_BLK // hps + hp)),
            pl.BlockSpec((1, hps, TQD, TKD), lambda bi, hp, qi: (0, hp, 0, 0)),
            pl.BlockSpec((1, hps, TQD, TKD), lambda bi, hp, qi: (1, hp, 0, 0)),
            pl.BlockSpec((1, DIFF_V_DIM), lambda bi, hp, qi: (0, 0)),
        ],
        out_specs=pl.BlockSpec((1, TQD, wide), lambda bi, hp, qi: (bi, qi, hp)),
        out_shape=jax.ShapeDtypeStruct((b, s, DIFF_HEADS * DIFF_V_DIM), jnp.bfloat16),
        scratch_shapes=[
            pltpu.VMEM((hps, s // TKD, DIFF_V_DIM + ONES_ROWS, TKD), jnp.bfloat16),
            pltpu.VMEM((2 * hps, 8, TQD), jnp.float32),
            pltpu.VMEM((2 * hps, DIFF_V_DIM + ONES_ROWS, TQD), jnp.float32),
        ],
        compiler_params=pltpu.CompilerParams(
            dimension_semantics=("arbitrary", "arbitrary", "arbitrary"),
            vmem_limit_bytes=VMEM_LIMIT),
        name="diff_attn",
    )(cfar, lam, proj, proj, proj, bias_tiles, bias_tiles, w_out_norm)


def _softplus(z):
    return jnp.maximum(z, 0.0) + jnp.log(1.0 + jnp.exp(-jnp.abs(z)))


def _sb_attn_kernel(q_ref, k_ref, v_ref, tri_ref, mask_ref, w_ref, o_ref, vt_sc, r_sc, acc_sc):
    qi = pl.program_id(1)
    tri_t = tri_ref[...]

    @pl.when(qi == 0)
    def _build_vt():
        for j in range(SB_PAIRS):
            for t in range(vt_sc.shape[1]):
                blk = v_ref[0, t * TK:(t + 1) * TK, j * LANES:(j + 1) * LANES]
                vt_sc[j, t] = blk.astype(jnp.float32).T.astype(vt_sc.dtype)

    qq = [_stack_subheads(q_ref[0, :, j * LANES:(j + 1) * LANES]) for j in range(SB_PAIRS)]

    def run_tiles(tiles):
        work = [(kt, diagonal, j) for (kt, diagonal) in tiles for j in range(SB_PAIRS)]
        z = []
        for kt, _, j in work:
            ks = pl.multiple_of(kt * TK, TK)
            z.append(lax.dot_general(k_ref[0, pl.ds(ks, TK), j * LANES:(j + 1) * LANES], qq[j],
                                     NT_DIMS, preferred_element_type=jnp.float32))
        csum = []
        for n, (kt, diagonal, j) in enumerate(work):
            sp = _softplus(z[n])
            c = sp * mask_ref[...] if diagonal else sp
            hi, lo = _split_hi_lo(c)
            csum.append(jnp.dot(tri_t, hi, preferred_element_type=jnp.float32)
                        + jnp.dot(tri_t, lo, preferred_element_type=jnp.float32))
        r_min = None
        for n, (kt, diagonal, j) in enumerate(work):
            if diagonal:
                a = jnp.exp(z[n] - csum[n]) * mask_ref[...]
                r_new = csum[n][0:1, :]
            else:
                r = r_sc[j][0:1, :]
                a = jnp.exp(z[n] - csum[n] - r)
                r_new = r + csum[n][0:1, :]
            av = jnp.dot(vt_sc[j, kt], a.astype(jnp.bfloat16), preferred_element_type=jnp.float32)
            acc_sc[j] = av if diagonal else acc_sc[j] + av
            r_sc[j] = jnp.broadcast_to(r_new, (8, 2 * TQ))
            if n >= len(work) - SB_PAIRS:
                r_min = jnp.min(r_new) if r_min is None else jnp.minimum(r_min, jnp.min(r_new))
        return r_min

    @pl.when(qi == 0)
    def _first_query_tile():
        run_tiles([(qi, True)])

    def _diagonal_and_previous():
        return run_tiles([(qi, True), (qi - 1, False)])

    r_min0 = lax.cond(qi >= 1, _diagonal_and_previous, lambda: jnp.float32(0.0))

    def more(carry):
        i, r_min = carry
        return jnp.logical_and(i < qi, r_min < SB_STOP_COST)

    def step(carry):
        i, _ = carry
        return i + 1, run_tiles([(qi - 1 - i, False)])

    lax.while_loop(more, step, (jnp.int32(1), r_min0))

    lane = lax.broadcasted_iota(jnp.int32, (TQ, LANES), 1)
    first = lane < SB_DIM
    for j in range(SB_PAIRS):
        acc = acc_sc[j].T
        o = jnp.where(first, acc[:TQ], acc[TQ:])
        o2 = o * o
        ss_a = jnp.sum(jnp.where(first, o2, 0.0), axis=-1, keepdims=True)
        ss_b = jnp.sum(jnp.where(first, 0.0, o2), axis=-1, keepdims=True)
        ms = jnp.where(first, ss_a, ss_b) * (1.0 / SB_DIM)
        o_ref[0, :, j * LANES:(j + 1) * LANES] = (
            (o * lax.rsqrt(ms + EPS)) * w_ref[...]).astype(o_ref.dtype)


def _sb_attn(proj, tri_t, mask_t, w_out_norm):
    b, s, _ = proj.shape
    nq = s // TQ
    wide = SB_PAIRS * LANES
    return pl.pallas_call(
        _sb_attn_kernel,
        grid=(b, nq),
        in_specs=[
            pl.BlockSpec((1, TQ, wide), lambda bi, qi: (bi, qi, QS_BLK // SB_PAIRS)),
            pl.BlockSpec((1, s, wide), lambda bi, qi: (bi, 0, KS_BLK // SB_PAIRS)),
            pl.BlockSpec((1, s, wide), lambda bi, qi: (bi, 0, VS_BLK // SB_PAIRS)),
            pl.BlockSpec((TK, TK), lambda bi, qi: (0, 0)),
            pl.BlockSpec((TK, 2 * TQ), lambda bi, qi: (0, 0)),
            pl.BlockSpec((1, LANES), lambda bi, qi: (0, 0)),
        ],
        out_specs=pl.BlockSpec((1, TQ, wide), lambda bi, qi: (bi, qi, 0)),
        out_shape=jax.ShapeDtypeStruct((b, s, SB_HEADS * SB_DIM), jnp.bfloat16),
        scratch_shapes=[
            pltpu.VMEM((SB_PAIRS, s // TK, LANES, TK), jnp.bfloat16),
            pltpu.VMEM((SB_PAIRS, 8, 2 * TQ), jnp.float32),
            pltpu.VMEM((SB_PAIRS, LANES, 2 * TQ), jnp.float32),
        ],
        compiler_params=pltpu.CompilerParams(
            dimension_semantics=("arbitrary", "arbitrary"),
            vmem_limit_bytes=VMEM_LIMIT),
        name="sb_attn",
    )(proj, proj, proj, tri_t, mask_t, w_out_norm)


def _out_ffn_kernel(x_ref, yd_ref, ys_ref, woa_ref, wob_ref, w2_ref, wg_ref, wu_ref, wd_ref,
                    o_ref):
    h1 = (x_ref[...]
          + jnp.dot(yd_ref[...], woa_ref[...], preferred_element_type=jnp.float32)
          + jnp.dot(ys_ref[...], wob_ref[...], preferred_element_type=jnp.float32))
    ms = jnp.mean(h1 * h1, axis=-1, keepdims=True)
    u2 = ((h1 * lax.rsqrt(ms + EPS)) * w2_ref[...]).astype(jnp.bfloat16)
    ffn = None
    for c in range(D_FF // FF_CHUNK):
        sl = slice(c * FF_CHUNK, (c + 1) * FF_CHUNK)
        gate = jnp.dot(u2, wg_ref[:, sl], preferred_element_type=jnp.float32)
        up = jnp.dot(u2, wu_ref[:, sl], preferred_element_type=jnp.float32)
        act = (gate * (1.0 / (1.0 + jnp.exp(-gate))) * up).astype(jnp.bfloat16)
        part = jnp.dot(act, wd_ref[sl, :], preferred_element_type=jnp.float32)
        ffn = part if ffn is None else ffn + part
    o_ref[...] = h1 + ffn


def _out_ffn(x2d, yd, ys, woa, wob, norm2_w, wg, wu, wd):
    n = x2d.shape[0]
    const = lambda i: (0, 0)
    half = DIFF_HEADS * DIFF_V_DIM
    resident = partial(pl.BlockSpec, index_map=const, pipeline_mode=pl.Buffered(1))
    return pl.pallas_call(
        _out_ffn_kernel,
        grid=(n // TM_PROJ,),
        in_specs=[
            pl.BlockSpec((TM_PROJ, D_MODEL), lambda i: (i, 0)),
            pl.BlockSpec((TM_PROJ, half), lambda i: (i, 0)),
            pl.BlockSpec((TM_PROJ, half), lambda i: (i, 0)),
            resident((half, D_MODEL)),
            resident((half, D_MODEL)),
            pl.BlockSpec((1, D_MODEL), const),
            resident((D_MODEL, D_FF)),
            resident((D_MODEL, D_FF)),
            resident((D_FF, D_MODEL)),
        ],
        out_specs=pl.BlockSpec((TM_PROJ, D_MODEL), lambda i: (i, 0)),
        out_shape=jax.ShapeDtypeStruct((n, D_MODEL), jnp.float32),
        compiler_params=pltpu.CompilerParams(
            dimension_semantics=("arbitrary",), vmem_limit_bytes=VMEM_LIMIT),
        name="out_ffn",
    )(x2d, yd, ys, woa, wob, norm2_w, wg, wu, wd)


def _diff_bias_tiles(rel_bias):
    qq = jnp.arange(TQD, dtype=jnp.int32)[:, None]
    kk = jnp.arange(TKD, dtype=jnp.int32)[None, :]
    width = TQD + TKD + 1
    tiles = []
    for koff in (0, -TKD):
        kpos = kk + koff
        table = rel_bias.astype(jnp.float32)[
            _t5_bucket(jnp.arange(width, dtype=jnp.int32) - TQD + koff)].T
        toep = jnp.tile(table, (1, TQD))[:, :TQD * (width - 1)].reshape(DIFF_HEADS, TQD, width - 1)
        bias = toep[:, :, TQD:TQD + TKD]
        allowed = jnp.floor_divide(kpos, CHUNK) <= jnp.floor_divide(qq, CHUNK)
        tiles.append(jnp.where(allowed[None], bias, MASK_VALUE))
    return jnp.swapaxes(jnp.stack(tiles, axis=0), 2, 3)


def kernel(x, norm1_w, w_in, q_norm_w, k_norm_w, lambda_q1, lambda_k1, lambda_q2, lambda_k2,
           diff_out_norm_w, sb_out_norm_w, w_out, norm2_w, w_gate, w_up, w_down, rel_bias):
    b, s, d = x.shape
    assert (b, s, d) == (8, 4096, D_MODEL) and w_in.shape[0] == 1
    f32, bf16 = jnp.float32, jnp.bfloat16
    x2d = x.reshape(b * s, d)

    qkw = jnp.concatenate([jnp.tile(q_norm_w[0].astype(f32), 2 * DIFF_HEADS)
                           * (DIFF_QK_DIM ** -0.5 * LOG2E),
                           jnp.tile(k_norm_w[0].astype(f32), 2 * DIFF_HEADS)])[None, :]
    grp = jnp.arange(GROUP_SLAB) // DIFF_QK_DIM
    gmat = jnp.where(grp[:, None] == grp[None, :], 1.0 / DIFF_QK_DIM, 0.0).astype(bf16)
    lam = (jnp.exp(jnp.sum(lambda_q1[0].astype(f32) * lambda_k1[0].astype(f32)))
           - jnp.exp(jnp.sum(lambda_q2[0].astype(f32) * lambda_k2[0].astype(f32)))
           + LAMBDA_INIT).reshape(1)
    bias_tiles = _diff_bias_tiles(rel_bias) * LOG2E
    cfar = rel_bias.astype(f32)[_t5_bucket(jnp.int32(-(TKD + 1)))] * LOG2E
    jj = jnp.arange(TK)
    tri_t = (jj[None, :] >= jj[:, None]).astype(bf16)
    causal_t = (jj[:, None] < jnp.arange(TQ)[None, :]).astype(f32)
    sb_mask_t = jnp.concatenate([causal_t, causal_t], axis=1)

    proj = _in_proj(x2d, norm1_w[0].astype(f32)[None, :], w_in[0].astype(bf16), qkw, gmat)
    proj = proj.reshape(b, s, IN_COLS)
    y_diff = _diff_attn(proj, bias_tiles, cfar, lam, diff_out_norm_w[0].astype(f32)[None, :])
    y_sb = _sb_attn(proj, tri_t, sb_mask_t, jnp.tile(sb_out_norm_w[0].astype(f32), 2)[None, :])

    half = DIFF_HEADS * DIFF_V_DIM
    wo = w_out[0].astype(bf16)
    out = _out_ffn(x2d, y_diff.reshape(b * s, half), y_sb.reshape(b * s, half),
                   wo[:half], wo[half:], norm2_w[0].astype(f32)[None, :],
                   w_gate[0].astype(bf16), w_up[0].astype(bf16), w_down[0].astype(bf16))
    return out.reshape(b, s, d)
```

```python
import math
from functools import partial

import jax
import jax.numpy as jnp
from jax import lax
from jax.experimental import pallas as pl
from jax.experimental.pallas import tpu as pltpu

D_MODEL = 1024
CHUNK = 64
DIFF_QK_DIM = 64
DIFF_V_DIM = 128
DIFF_HEADS = 4
SB_DIM = 64
SB_HEADS = 8
NUM_BUCKETS = 32
MAX_DISTANCE = 128
D_FF = 2816
EPS = 1e-6
IN_COLS = 3072
LAMBDA_INIT = 0.8 - 0.6 * math.exp(-0.3 * 0)
LANES = 128

QD_BLK, KD_BLK, VD_BLK = 0, 4, 8
QS_BLK, KS_BLK, VS_BLK = 12, 16, 20

TM_PROJ = 512
COL_CHUNK = 512
GROUP_SLAB = 256
TQD = 512
TKD = 512
HEADS_PER_STEP = 2
ONES_ROWS = 16
SCORE_LOOKAHEAD = 2
LOG2E = math.log2(math.e)
TQ = 256
TK = 256
SB_PAIRS = SB_HEADS // 2
FF_CHUNK = 256
MASK_VALUE = -1e30
SB_STOP_COST = 104.0
VMEM_LIMIT = 56 * 1024 * 1024

NT_DIMS = (((1,), (1,)), ((), ()))


def _t5_bucket(rel):
    nb = NUM_BUCKETS // 2
    max_exact = nb // 2
    ret = (rel > 0).astype(jnp.int32) * nb
    n = jnp.abs(rel)
    nf = jnp.maximum(n, 1).astype(jnp.float32)
    large = max_exact + (jnp.log(nf / max_exact) / math.log(MAX_DISTANCE / max_exact)
                         * (nb - max_exact)).astype(jnp.int32)
    large = jnp.minimum(large, nb - 1)
    return ret + jnp.where(n < max_exact, n, large)


def _split_hi_lo(x):
    hi = x.astype(jnp.bfloat16)
    lo = (x - hi.astype(jnp.float32)).astype(jnp.bfloat16)
    return hi, lo


def _in_proj_kernel(x_ref, w1_ref, win_ref, qkw_ref, g_ref, o_ref):
    x = x_ref[...]
    ms = jnp.mean(x * x, axis=-1, keepdims=True)
    u = ((x * lax.rsqrt(ms + EPS)) * w1_ref[...]).astype(jnp.bfloat16)
    g = g_ref[...]
    for c in range(IN_COLS // COL_CHUNK):
        lo_c, hi_c = c * COL_CHUNK, (c + 1) * COL_CHUNK
        p = jnp.dot(u, win_ref[:, lo_c:hi_c], preferred_element_type=jnp.float32)
        if hi_c <= 2 * DIFF_HEADS * 2 * DIFF_QK_DIM:
            for j in range(COL_CHUNK // GROUP_SLAB):
                sl = slice(j * GROUP_SLAB, (j + 1) * GROUP_SLAB)
                pj = p[:, sl]
                msq = jnp.dot((pj * pj).astype(jnp.bfloat16), g, preferred_element_type=jnp.float32)
                w = qkw_ref[:, lo_c + j * GROUP_SLAB: lo_c + (j + 1) * GROUP_SLAB]
                o_ref[:, lo_c + j * GROUP_SLAB: lo_c + (j + 1) * GROUP_SLAB] = (
                    (pj * lax.rsqrt(msq + EPS)) * w).astype(jnp.bfloat16)
        else:
            if lo_c == QS_BLK * LANES:
                p = p * (SB_DIM ** -0.5)
            o_ref[:, lo_c:hi_c] = p.astype(jnp.bfloat16)


def _in_proj(x2d, norm1_w, w_in_bf, qkw, gmat):
    n = x2d.shape[0]
    const = lambda i: (0, 0)
    return pl.pallas_call(
        _in_proj_kernel,
        grid=(n // TM_PROJ,),
        in_specs=[
            pl.BlockSpec((TM_PROJ, D_MODEL), lambda i: (i, 0)),
            pl.BlockSpec((1, D_MODEL), const),
            pl.BlockSpec((D_MODEL, IN_COLS), const, pipeline_mode=pl.Buffered(1)),
            pl.BlockSpec((1, 2 * DIFF_HEADS * 2 * DIFF_QK_DIM), const),
            pl.BlockSpec((GROUP_SLAB, GROUP_SLAB), const),
        ],
        out_specs=pl.BlockSpec((TM_PROJ, IN_COLS), lambda i: (i, 0)),
        out_shape=jax.ShapeDtypeStruct((n, IN_COLS), jnp.bfloat16),
        compiler_params=pltpu.CompilerParams(
            dimension_semantics=("arbitrary",), vmem_limit_bytes=VMEM_LIMIT),
        name="in_proj",
    )(x2d, norm1_w, w_in_bf, qkw, gmat)


def _stack_subheads(q):
    lane = lax.broadcasted_iota(jnp.int32, q.shape, 1)
    zero = jnp.zeros_like(q)
    return jnp.concatenate([jnp.where(lane < SB_DIM, q, zero),
                            jnp.where(lane >= SB_DIM, q, zero)], axis=0)


def _diff_attn_kernel(cfar_ref, lam_ref, q_ref, k_ref, v_ref, bdiag_ref, bsub_ref, w_ref, o_ref,
                      vt_sc, m_sc, acc_sc):
    hp = pl.program_id(1)
    qi = pl.program_id(2)

    @pl.when(qi == 0)
    def _build_vt():
        ones = jnp.ones((ONES_ROWS, TKD), vt_sc.dtype)
        for hh in range(HEADS_PER_STEP):
            for t in range(vt_sc.shape[1]):
                blk = v_ref[0, t * TKD:(t + 1) * TKD, hh * LANES:(hh + 1) * LANES]
                vt_sc[hh, t, :DIFF_V_DIM, :] = blk.astype(jnp.float32).T.astype(vt_sc.dtype)
                vt_sc[hh, t, DIFF_V_DIM:, :] = ones

    lane = lax.broadcasted_iota(jnp.int32, (TQD, LANES), 1)
    zero = jnp.zeros((TQD, LANES), q_ref.dtype)
    q_sub = []
    for hh in range(HEADS_PER_STEP):
        q = q_ref[0, :, hh * LANES:(hh + 1) * LANES]
        q_sub.append(jnp.where(lane < DIFF_QK_DIM, q, zero))
        q_sub.append(jnp.where(lane >= DIFF_QK_DIM, q, zero))

    n_maps = 2 * HEADS_PER_STEP

    def run_tiles(tiles):
        def scores(kt, idx):
            ks = pl.multiple_of(kt * TKD, TKD)
            k = k_ref[0, pl.ds(ks, TKD), (idx // 2) * LANES:(idx // 2 + 1) * LANES]
            return lax.dot_general(k, q_sub[idx], NT_DIMS, preferred_element_type=jnp.float32)

        def weights(s, idx, bias_ref, cfar, first):
            hh = idx // 2
            if first:
                s = s + bias_ref[0, hh]
                m_new = jnp.max(s, axis=0, keepdims=True)
                m_sc[idx] = jnp.broadcast_to(m_new, (8, TQD))
                return jnp.exp2(s - m_new).astype(jnp.bfloat16), None
            m_old = m_sc[idx][0:1, :]
            if bias_ref is None:
                m_new = jnp.maximum(m_old, jnp.max(s, axis=0, keepdims=True) + cfar[hh])
                shift = m_new - cfar[hh]
            else:
                s = s + bias_ref[0, hh]
                m_new = jnp.maximum(m_old, jnp.max(s, axis=0, keepdims=True))
                shift = m_new
            m_sc[idx] = jnp.broadcast_to(m_new, (8, TQD))
            return jnp.exp2(s - shift).astype(jnp.bfloat16), jnp.exp2(m_old - m_new)

        def accumulate(kt, idx, p, alpha):
            pv = jnp.dot(vt_sc[idx // 2, kt], p, preferred_element_type=jnp.float32)
            acc_sc[idx] = pv if alpha is None else alpha * acc_sc[idx] + pv

        work = [(kt, idx, bias_ref, cfar, first)
                for (kt, bias_ref, cfar, first) in tiles for idx in range(n_maps)]
        s_next = [scores(w[0], w[1]) for w in work[:SCORE_LOOKAHEAD]]
        pending = None
        for n, (kt, idx, bias_ref, cfar, first) in enumerate(work):
            p, alpha = weights(s_next.pop(0), idx, bias_ref, cfar, first)
            if n + SCORE_LOOKAHEAD < len(work):
                nxt = work[n + SCORE_LOOKAHEAD]
                s_next.append(scores(nxt[0], nxt[1]))
            if pending is not None:
                accumulate(*pending)
            pending = (kt, idx, p, alpha)
        accumulate(*pending)

    cfar = [cfar_ref[HEADS_PER_STEP * hp + hh] for hh in range(HEADS_PER_STEP)]

    @pl.when(qi == 0)
    def _first_query_tile():
        run_tiles([(qi, bdiag_ref, None, True)])

    @pl.when(qi >= 1)
    def _near_tiles():
        run_tiles([(qi, bdiag_ref, None, True), (qi - 1, bsub_ref, None, False)])

    n_far = jnp.maximum(qi - 1, 0)

    def far_pair(i, carry):
        run_tiles([(2 * i, None, cfar, False), (2 * i + 1, None, cfar, False)])
        return carry

    lax.fori_loop(0, n_far // 2, far_pair, 0)

    @pl.when(n_far % 2 == 1)
    def _last_far_tile():
        run_tiles([(n_far - 1, None, cfar, False)])

    for hh in range(HEADS_PER_STEP):
        acc0 = acc_sc[2 * hh]
        acc1 = acc_sc[2 * hh + 1]
        o_t = (acc0[:DIFF_V_DIM] / acc0[DIFF_V_DIM:DIFF_V_DIM + 1]
               - lam_ref[0] * (acc1[:DIFF_V_DIM] / acc1[DIFF_V_DIM:DIFF_V_DIM + 1]))
        o = o_t.T
        ms = jnp.mean(o * o, axis=-1, keepdims=True)
        y = ((o * lax.rsqrt(ms + EPS)) * w_ref[...]) * (1.0 - LAMBDA_INIT)
        o_ref[0, :, hh * LANES:(hh + 1) * LANES] = y.astype(o_ref.dtype)


def _diff_attn(proj, bias_tiles, cfar, lam, w_out_norm):
    b, s, _ = proj.shape
    nq = s // TQD
    hps = HEADS_PER_STEP
    wide = hps * LANES
    smem = pl.BlockSpec(memory_space=pltpu.SMEM)
    return pl.pallas_call(
        _diff_attn_kernel,
        grid=(b, DIFF_HEADS // hps, nq),
        in_specs=[
            smem, smem,
            pl.BlockSpec((1, TQD, wide), lambda bi, hp, qi: (bi, qi, QD_BLK // hps + hp)),
            pl.BlockSpec((1, s, wide), lambda bi, hp, qi: (bi, 0, KD_BLK // hps + hp)),
            pl.BlockSpec((1, s, wide), lambda bi, hp, qi: (bi, 0, VD_BLK // hps + hp)),
            pl.BlockSpec((1, hps, TQD, TKD), lambda bi, hp, qi: (0, hp, 0, 0)),
            pl.BlockSpec((1, hps, TQD, TKD), lambda bi, hp, qi: (1, hp, 0, 0)),
            pl.BlockSpec((1, DIFF_V_DIM), lambda bi, hp, qi: (0, 0)),
        ],
        out_specs=pl.BlockSpec((1, TQD, wide), lambda bi, hp, qi: (bi, qi, hp)),
        out_shape=jax.ShapeDtypeStruct((b, s, DIFF_HEADS * DIFF_V_DIM), jnp.bfloat16),
        scratch_shapes=[
            pltpu.VMEM((hps, s // TKD, DIFF_V_DIM + ONES_ROWS, TKD), jnp.bfloat16),
            pltpu.VMEM((2 * hps, 8, TQD), jnp.float32),
            pltpu.VMEM((2 * hps, DIFF_V_DIM + ONES_ROWS, TQD), jnp.float32),
        ],
        compiler_params=pltpu.CompilerParams(
            dimension_semantics=("arbitrary", "arbitrary", "arbitrary"),
            vmem_limit_bytes=VMEM_LIMIT),
        name="diff_attn",
    )(cfar, lam, proj, proj, proj, bias_tiles, bias_tiles, w_out_norm)


def _softplus(z):
    return jnp.maximum(z, 0.0) + jnp.log(1.0 + jnp.exp(-jnp.abs(z)))


def _sb_attn_kernel(q_ref, k_ref, v_ref, tri_ref, mask_ref, w_ref, o_ref, vt_sc, r_sc, acc_sc):
    qi = pl.program_id(1)
    tri_t = tri_ref[...]

    @pl.when(qi == 0)
    def _build_vt():
        for j in range(SB_PAIRS):
            for t in range(vt_sc.shape[1]):
                blk = v_ref[0, t * TK:(t + 1) * TK, j * LANES:(j + 1) * LANES]
                vt_sc[j, t] = blk.astype(jnp.float32).T.astype(vt_sc.dtype)

    qq = [_stack_subheads(q_ref[0, :, j * LANES:(j + 1) * LANES]) for j in range(SB_PAIRS)]

    def run_tiles(tiles):
        work = [(kt, diagonal, j) for (kt, diagonal) in tiles for j in range(SB_PAIRS)]
        z = []
        for kt, _, j in work:
            ks = pl.multiple_of(kt * TK, TK)
            z.append(lax.dot_general(k_ref[0, pl.ds(ks, TK), j * LANES:(j + 1) * LANES], qq[j],
                                     NT_DIMS, preferred_element_type=jnp.float32))
        csum = []
        for n, (kt, diagonal, j) in enumerate(work):
            sp = _softplus(z[n])
            c = sp * mask_ref[...] if diagonal else sp
            hi, lo = _split_hi_lo(c)
            csum.append(jnp.dot(tri_t, hi, preferred_element_type=jnp.float32)
                        + jnp.dot(tri_t, lo, preferred_element_type=jnp.float32))
        r_min = None
        for n, (kt, diagonal, j) in enumerate(work):
            if diagonal:
                a = jnp.exp(z[n] - csum[n]) * mask_ref[...]
                r_new = csum[n][0:1, :]
            else:
                r = r_sc[j][0:1, :]
                a = jnp.exp(z[n] - csum[n] - r)
                r_new = r + csum[n][0:1, :]
            av = jnp.dot(vt_sc[j, kt], a.astype(jnp.bfloat16), preferred_element_type=jnp.float32)
            acc_sc[j] = av if diagonal else acc_sc[j] + av
            r_sc[j] = jnp.broadcast_to(r_new, (8, 2 * TQ))
            if n >= len(work) - SB_PAIRS:
                r_min = jnp.min(r_new) if r_min is None else jnp.minimum(r_min, jnp.min(r_new))
        return r_min

    @pl.when(qi == 0)
    def _first_query_tile():
        run_tiles([(qi, True)])

    def _diagonal_and_previous():
        return run_tiles([(qi, True), (qi - 1, False)])

    r_min0 = lax.cond(qi >= 1, _diagonal_and_previous, lambda: jnp.float32(0.0))

    def more(carry):
        i, r_min = carry
        return jnp.logical_and(i < qi, r_min < SB_STOP_COST)

    def step(carry):
        i, _ = carry
        return i + 1, run_tiles([(qi - 1 - i, False)])

    lax.while_loop(more, step, (jnp.int32(1), r_min0))

    lane = lax.broadcasted_iota(jnp.int32, (TQ, LANES), 1)
    first = lane < SB_DIM
    for j in range(SB_PAIRS):
        acc = acc_sc[j].T
        o = jnp.where(first, acc[:TQ], acc[TQ:])
        o2 = o * o
        ss_a = jnp.sum(jnp.where(first, o2, 0.0), axis=-1, keepdims=True)
        ss_b = jnp.sum(jnp.where(first, 0.0, o2), axis=-1, keepdims=True)
        ms = jnp.where(first, ss_a, ss_b) * (1.0 / SB_DIM)
        o_ref[0, :, j * LANES:(j + 1) * LANES] = (
            (o * lax.rsqrt(ms + EPS)) * w_ref[...]).astype(o_ref.dtype)


def _sb_attn(proj, tri_t, mask_t, w_out_norm):
    b, s, _ = proj.shape
    nq = s // TQ
    wide = SB_PAIRS * LANES
    return pl.pallas_call(
        _sb_attn_kernel,
        grid=(b, nq),
        in_specs=[
            pl.BlockSpec((1, TQ, wide), lambda bi, qi: (bi, qi, QS_BLK // SB_PAIRS)),
            pl.BlockSpec((1, s, wide), lambda bi, qi: (bi, 0, KS_BLK // SB_PAIRS)),
            pl.BlockSpec((1, s, wide), lambda bi, qi: (bi, 0, VS_BLK // SB_PAIRS)),
            pl.BlockSpec((TK, TK), lambda bi, qi: (0, 0)),
            pl.BlockSpec((TK, 2 * TQ), lambda bi, qi: (0, 0)),
            pl.BlockSpec((1, LANES), lambda bi, qi: (0, 0)),
        ],
        out_specs=pl.BlockSpec((1, TQ, wide), lambda bi, qi: (bi, qi, 0)),
        out_shape=jax.ShapeDtypeStruct((b, s, SB_HEADS * SB_DIM), jnp.bfloat16),
        scratch_shapes=[
            pltpu.VMEM((SB_PAIRS, s // TK, LANES, TK), jnp.bfloat16),
            pltpu.VMEM((SB_PAIRS, 8, 2 * TQ), jnp.float32),
            pltpu.VMEM((SB_PAIRS, LANES, 2 * TQ), jnp.float32),
        ],
        compiler_params=pltpu.CompilerParams(
            dimension_semantics=("arbitrary", "arbitrary"),
            vmem_limit_bytes=VMEM_LIMIT),
        name="sb_attn",
    )(proj, proj, proj, tri_t, mask_t, w_out_norm)


def _out_ffn_kernel(x_ref, yd_ref, ys_ref, woa_ref, wob_ref, w2_ref, wg_ref, wu_ref, wd_ref,
                    o_ref):
    h1 = (x_ref[...]
          + jnp.dot(yd_ref[...], woa_ref[...], preferred_element_type=jnp.float32)
          + jnp.dot(ys_ref[...], wob_ref[...], preferred_element_type=jnp.float32))
    ms = jnp.mean(h1 * h1, axis=-1, keepdims=True)
    u2 = ((h1 * lax.rsqrt(ms + EPS)) * w2_ref[...]).astype(jnp.bfloat16)
    ffn = None
    for c in range(D_FF // FF_CHUNK):
        sl = slice(c * FF_CHUNK, (c + 1) * FF_CHUNK)
        gate = jnp.dot(u2, wg_ref[:, sl], preferred_element_type=jnp.float32)
        up = jnp.dot(u2, wu_ref[:, sl], preferred_element_type=jnp.float32)
        act = (gate * (1.0 / (1.0 + jnp.exp(-gate))) * up).astype(jnp.bfloat16)
        part = jnp.dot(act, wd_ref[sl, :], preferred_element_type=jnp.float32)
        ffn = part if ffn is None else ffn + part
    o_ref[...] = h1 + ffn


def _out_ffn(x2d, yd, ys, wo, norm2_w, wg, wu, wd):
    n = x2d.shape[0]
    const = lambda i: (0, 0)
    half = DIFF_HEADS * DIFF_V_DIM
    resident = partial(pl.BlockSpec, index_map=const, pipeline_mode=pl.Buffered(1))
    return pl.pallas_call(
        _out_ffn_kernel,
        grid=(n // TM_PROJ,),
        in_specs=[
            pl.BlockSpec((TM_PROJ, D_MODEL), lambda i: (i, 0)),
            pl.BlockSpec((TM_PROJ, half), lambda i: (i, 0)),
            pl.BlockSpec((TM_PROJ, half), lambda i: (i, 0)),
            resident((half, D_MODEL)),
            pl.BlockSpec((half, D_MODEL), lambda i: (1, 0), pipeline_mode=pl.Buffered(1)),
            pl.BlockSpec((1, D_MODEL), const),
            resident((D_MODEL, D_FF)),
            resident((D_MODEL, D_FF)),
            resident((D_FF, D_MODEL)),
        ],
        out_specs=pl.BlockSpec((TM_PROJ, D_MODEL), lambda i: (i, 0)),
        out_shape=jax.ShapeDtypeStruct((n, D_MODEL), jnp.float32),
        compiler_params=pltpu.CompilerParams(
            dimension_semantics=("arbitrary",), vmem_limit_bytes=VMEM_LIMIT),
        name="out_ffn",
    )(x2d, yd, ys, wo, wo, norm2_w, wg, wu, wd)


def _bias_tiles_kernel(thr_ref, vals_ref, o_ref):
    t = pl.program_id(0)
    h = pl.program_id(1)
    kpos = lax.broadcasted_iota(jnp.int32, (TKD, TQD), 0) - t * TKD
    qpos = lax.broadcasted_iota(jnp.int32, (TKD, TQD), 1)
    rel = kpos - qpos
    dist = jnp.abs(rel)
    half = NUM_BUCKETS // 2
    behind = jnp.full((TKD, TQD), vals_ref[h, 0], jnp.float32)
    ahead = jnp.full((TKD, TQD), vals_ref[h, half], jnp.float32)
    for i in range(1, half):
        reached = dist >= thr_ref[i]
        behind = jnp.where(reached, vals_ref[h, i], behind)
        ahead = jnp.where(reached, vals_ref[h, half + i], ahead)
    bias = jnp.where(rel > 0, ahead, behind)
    shift = CHUNK.bit_length() - 1
    allowed = (kpos >> shift) <= (qpos >> shift)
    o_ref[0, 0] = jnp.where(allowed, bias, MASK_VALUE)


def _diff_bias_tiles(rel_bias):
    half = NUM_BUCKETS // 2
    idx = _t5_bucket(-jnp.arange(TQD + TKD, dtype=jnp.int32))
    thr = jnp.argmax(idx[None, :] >= jnp.arange(half, dtype=jnp.int32)[:, None], axis=1)
    vals = rel_bias.astype(jnp.float32).T * LOG2E
    smem = pl.BlockSpec(memory_space=pltpu.SMEM)
    return pl.pallas_call(
        _bias_tiles_kernel,
        grid=(2, DIFF_HEADS),
        in_specs=[smem, smem],
        out_specs=pl.BlockSpec((1, 1, TKD, TQD), lambda t, h: (t, h, 0, 0)),
        out_shape=jax.ShapeDtypeStruct((2, DIFF_HEADS, TKD, TQD), jnp.float32),
        compiler_params=pltpu.CompilerParams(dimension_semantics=("arbitrary", "arbitrary")),
        name="bias_tiles",
    )(thr.astype(jnp.int32), vals)


def kernel(x, norm1_w, w_in, q_norm_w, k_norm_w, lambda_q1, lambda_k1, lambda_q2, lambda_k2,
           diff_out_norm_w, sb_out_norm_w, w_out, norm2_w, w_gate, w_up, w_down, rel_bias):
    b, s, d = x.shape
    assert (b, s, d) == (8, 4096, D_MODEL) and w_in.shape[0] == 1
    f32, bf16 = jnp.float32, jnp.bfloat16
    x2d = x.reshape(b * s, d)

    qkw = jnp.concatenate([jnp.tile(q_norm_w[0].astype(f32), 2 * DIFF_HEADS)
                           * (DIFF_QK_DIM ** -0.5 * LOG2E),
                           jnp.tile(k_norm_w[0].astype(f32), 2 * DIFF_HEADS)])[None, :]
    grp = jnp.arange(GROUP_SLAB) // DIFF_QK_DIM
    gmat = jnp.where(grp[:, None] == grp[None, :], 1.0 / DIFF_QK_DIM, 0.0).astype(bf16)
    lam = (jnp.exp(jnp.sum(lambda_q1[0].astype(f32) * lambda_k1[0].astype(f32)))
           - jnp.exp(jnp.sum(lambda_q2[0].astype(f32) * lambda_k2[0].astype(f32)))
           + LAMBDA_INIT).reshape(1)
    bias_tiles = _diff_bias_tiles(rel_bias)
    cfar = rel_bias.astype(f32)[_t5_bucket(jnp.int32(-(TKD + 1)))] * LOG2E
    jj = jnp.arange(TK)
    tri_t = (jj[None, :] >= jj[:, None]).astype(bf16)
    causal_t = (jj[:, None] < jnp.arange(TQ)[None, :]).astype(f32)
    sb_mask_t = jnp.concatenate([causal_t, causal_t], axis=1)

    proj = _in_proj(x2d, norm1_w[0].astype(f32)[None, :], w_in[0].astype(bf16), qkw, gmat)
    proj = proj.reshape(b, s, IN_COLS)
    y_diff = _diff_attn(proj, bias_tiles, cfar, lam, diff_out_norm_w[0].astype(f32)[None, :])
    y_sb = _sb_attn(proj, tri_t, sb_mask_t, jnp.tile(sb_out_norm_w[0].astype(f32), 2)[None, :])

    half = DIFF_HEADS * DIFF_V_DIM
    out = _out_ffn(x2d, y_diff.reshape(b * s, half), y_sb.reshape(b * s, half),
                   w_out[0].astype(bf16), norm2_w[0].astype(f32)[None, :],
                   w_gate[0].astype(bf16), w_up[0].astype(bf16), w_down[0].astype(bf16))
    return out.reshape(b, s, d)
```

```python
import math
from functools import partial

import jax
import jax.numpy as jnp
from jax import lax
from jax.experimental import pallas as pl
from jax.experimental.pallas import tpu as pltpu

D_MODEL = 1024
CHUNK = 64
DIFF_QK_DIM = 64
DIFF_V_DIM = 128
DIFF_HEADS = 4
SB_DIM = 64
SB_HEADS = 8
NUM_BUCKETS = 32
MAX_DISTANCE = 128
D_FF = 2816
EPS = 1e-6
IN_COLS = 3072
LAMBDA_INIT = 0.8 - 0.6 * math.exp(-0.3 * 0)
LANES = 128

QD_BLK, KD_BLK, VD_BLK = 0, 4, 8
QS_BLK, KS_BLK, VS_BLK = 12, 16, 20

TM_PROJ = 512
COL_CHUNK = 512
GROUP_SLAB = 256
TQD = 512
TKD = 512
HEADS_PER_STEP = 4
ONES_ROWS = 16
SCORE_LOOKAHEAD = 2
LOG2E = math.log2(math.e)
TQ = 256
TK = 256
SB_PAIRS = SB_HEADS // 2
FF_CHUNK = 256
MASK_VALUE = -1e30
SB_STOP_COST = 104.0
VMEM_LIMIT = 56 * 1024 * 1024

NT_DIMS = (((1,), (1,)), ((), ()))


def _t5_bucket(rel):
    nb = NUM_BUCKETS // 2
    max_exact = nb // 2
    ret = (rel > 0).astype(jnp.int32) * nb
    n = jnp.abs(rel)
    nf = jnp.maximum(n, 1).astype(jnp.float32)
    large = max_exact + (jnp.log(nf / max_exact) / jnp.log(jnp.float32(MAX_DISTANCE / max_exact))
                         * (nb - max_exact)).astype(jnp.int32)
    large = jnp.minimum(large, nb - 1)
    return ret + jnp.where(n < max_exact, n, large)


def _split_hi_lo(x):
    hi = x.astype(jnp.bfloat16)
    lo = (x - hi.astype(jnp.float32)).astype(jnp.bfloat16)
    return hi, lo


def _in_proj_kernel(x_ref, w1_ref, win_ref, qkw_ref, g_ref, o_ref):
    x = x_ref[...]
    ms = jnp.mean(x * x, axis=-1, keepdims=True)
    u = ((x * lax.rsqrt(ms + EPS)) * w1_ref[...]).astype(jnp.bfloat16)
    g = g_ref[...]
    for c in range(IN_COLS // COL_CHUNK):
        lo_c, hi_c = c * COL_CHUNK, (c + 1) * COL_CHUNK
        p = jnp.dot(u, win_ref[:, lo_c:hi_c], preferred_element_type=jnp.float32)
        if hi_c <= 2 * DIFF_HEADS * 2 * DIFF_QK_DIM:
            for j in range(COL_CHUNK // GROUP_SLAB):
                sl = slice(j * GROUP_SLAB, (j + 1) * GROUP_SLAB)
                pj = p[:, sl]
                msq = jnp.dot((pj * pj).astype(jnp.bfloat16), g, preferred_element_type=jnp.float32)
                w = qkw_ref[:, lo_c + j * GROUP_SLAB: lo_c + (j + 1) * GROUP_SLAB]
                o_ref[:, lo_c + j * GROUP_SLAB: lo_c + (j + 1) * GROUP_SLAB] = (
                    (pj * lax.rsqrt(msq + EPS)) * w).astype(jnp.bfloat16)
        else:
            if lo_c == QS_BLK * LANES:
                p = p * (SB_DIM ** -0.5)
            o_ref[:, lo_c:hi_c] = p.astype(jnp.bfloat16)


def _in_proj(x2d, norm1_w, w_in_bf, qkw, gmat):
    n = x2d.shape[0]
    const = lambda i: (0, 0)
    return pl.pallas_call(
        _in_proj_kernel,
        grid=(n // TM_PROJ,),
        in_specs=[
            pl.BlockSpec((TM_PROJ, D_MODEL), lambda i: (i, 0)),
            pl.BlockSpec((1, D_MODEL), const),
            pl.BlockSpec((D_MODEL, IN_COLS), const, pipeline_mode=pl.Buffered(1)),
            pl.BlockSpec((1, 2 * DIFF_HEADS * 2 * DIFF_QK_DIM), const),
            pl.BlockSpec((GROUP_SLAB, GROUP_SLAB), const),
        ],
        out_specs=pl.BlockSpec((TM_PROJ, IN_COLS), lambda i: (i, 0)),
        out_shape=jax.ShapeDtypeStruct((n, IN_COLS), jnp.bfloat16),
        compiler_params=pltpu.CompilerParams(
            dimension_semantics=("arbitrary",), vmem_limit_bytes=VMEM_LIMIT),
        name="in_proj",
    )(x2d, norm1_w, w_in_bf, qkw, gmat)


def _stack_subheads(q):
    lane = lax.broadcasted_iota(jnp.int32, q.shape, 1)
    zero = jnp.zeros_like(q)
    return jnp.concatenate([jnp.where(lane < SB_DIM, q, zero),
                            jnp.where(lane >= SB_DIM, q, zero)], axis=0)


def _diff_attn_kernel(cfar_ref, lam_ref, q_ref, k_ref, v_ref, bdiag_ref, bsub_ref, w_ref, o_ref,
                      vt_sc, m_sc, acc_sc):
    hp = pl.program_id(1)
    qi = pl.program_id(2)

    @pl.when(qi == 0)
    def _build_vt():
        ones = jnp.ones((ONES_ROWS, TKD), vt_sc.dtype)
        for hh in range(HEADS_PER_STEP):
            for t in range(vt_sc.shape[1]):
                blk = v_ref[0, t * TKD:(t + 1) * TKD, hh * LANES:(hh + 1) * LANES]
                vt_sc[hh, t, :DIFF_V_DIM, :] = blk.astype(jnp.float32).T.astype(vt_sc.dtype)
                vt_sc[hh, t, DIFF_V_DIM:, :] = ones

    lane = lax.broadcasted_iota(jnp.int32, (TQD, LANES), 1)
    zero = jnp.zeros((TQD, LANES), q_ref.dtype)
    q_sub = []
    for hh in range(HEADS_PER_STEP):
        q = q_ref[0, :, hh * LANES:(hh + 1) * LANES]
        q_sub.append(jnp.where(lane < DIFF_QK_DIM, q, zero))
        q_sub.append(jnp.where(lane >= DIFF_QK_DIM, q, zero))

    n_maps = 2 * HEADS_PER_STEP

    def run_tiles(tiles):
        def scores(kt, idx):
            ks = pl.multiple_of(kt * TKD, TKD)
            k = k_ref[0, pl.ds(ks, TKD), (idx // 2) * LANES:(idx // 2 + 1) * LANES]
            return lax.dot_general(k, q_sub[idx], NT_DIMS, preferred_element_type=jnp.float32)

        def weights(s, idx, bias_ref, cfar, first):
            hh = idx // 2
            if first:
                s = s + bias_ref[0, hh]
                m_new = jnp.max(s, axis=0, keepdims=True)
                m_sc[idx] = jnp.broadcast_to(m_new, (8, TQD))
                return jnp.exp2(s - m_new).astype(jnp.bfloat16), None
            m_old = m_sc[idx][0:1, :]
            if bias_ref is None:
                m_new = jnp.maximum(m_old, jnp.max(s, axis=0, keepdims=True) + cfar[hh])
                shift = m_new - cfar[hh]
            else:
                s = s + bias_ref[0, hh]
                m_new = jnp.maximum(m_old, jnp.max(s, axis=0, keepdims=True))
                shift = m_new
            m_sc[idx] = jnp.broadcast_to(m_new, (8, TQD))
            return jnp.exp2(s - shift).astype(jnp.bfloat16), jnp.exp2(m_old - m_new)

        def accumulate(kt, idx, p, alpha):
            pv = jnp.dot(vt_sc[idx // 2, kt], p, preferred_element_type=jnp.float32)
            acc_sc[idx] = pv if alpha is None else alpha * acc_sc[idx] + pv

        work = [(kt, idx, bias_ref, cfar, first)
                for (kt, bias_ref, cfar, first) in tiles for idx in range(n_maps)]
        s_next = [scores(w[0], w[1]) for w in work[:SCORE_LOOKAHEAD]]
        pending = None
        for n, (kt, idx, bias_ref, cfar, first) in enumerate(work):
            p, alpha = weights(s_next.pop(0), idx, bias_ref, cfar, first)
            if n + SCORE_LOOKAHEAD < len(work):
                nxt = work[n + SCORE_LOOKAHEAD]
                s_next.append(scores(nxt[0], nxt[1]))
            if pending is not None:
                accumulate(*pending)
            pending = (kt, idx, p, alpha)
        accumulate(*pending)

    cfar = [cfar_ref[HEADS_PER_STEP * hp + hh] for hh in range(HEADS_PER_STEP)]

    @pl.when(qi == 0)
    def _first_query_tile():
        run_tiles([(qi, bdiag_ref, None, True)])

    @pl.when(qi >= 1)
    def _near_tiles():
        run_tiles([(qi, bdiag_ref, None, True), (qi - 1, bsub_ref, None, False)])

    n_far = jnp.maximum(qi - 1, 0)

    def far_pair(i, carry):
        run_tiles([(2 * i, None, cfar, False), (2 * i + 1, None, cfar, False)])
        return carry

    lax.fori_loop(0, n_far // 2, far_pair, 0)

    @pl.when(n_far % 2 == 1)
    def _last_far_tile():
        run_tiles([(n_far - 1, None, cfar, False)])

    for hh in range(HEADS_PER_STEP):
        acc0 = acc_sc[2 * hh]
        acc1 = acc_sc[2 * hh + 1]
        o_t = (acc0[:DIFF_V_DIM] / acc0[DIFF_V_DIM:DIFF_V_DIM + 1]
               - lam_ref[0] * (acc1[:DIFF_V_DIM] / acc1[DIFF_V_DIM:DIFF_V_DIM + 1]))
        o = o_t.T
        ms = jnp.mean(o * o, axis=-1, keepdims=True)
        y = ((o * lax.rsqrt(ms + EPS)) * w_ref[...]) * (1.0 - LAMBDA_INIT)
        o_ref[0, :, hh * LANES:(hh + 1) * LANES] = y.astype(o_ref.dtype)


def _diff_attn(proj, bias_tiles, cfar, lam, w_out_norm):
    b, s, _ = proj.shape
    nq = s // TQD
    hps = HEADS_PER_STEP
    wide = hps * LANES
    smem = pl.BlockSpec(memory_space=pltpu.SMEM)
    return pl.pallas_call(
        _diff_attn_kernel,
        grid=(b, DIFF_HEADS // hps, nq),
        in_specs=[
            smem, smem,
            pl.BlockSpec((1, TQD, wide), lambda bi, hp, qi: (bi, qi, QD_BLK // hps + hp)),
            pl.BlockSpec((1, s, wide), lambda bi, hp, qi: (bi, 0, KD_BLK // hps + hp)),
            pl.BlockSpec((1, s, wide), lambda bi, hp, qi: (bi, 0, VD_BLK // hps + hp)),
            pl.BlockSpec((1, hps, TKD, TQD), lambda bi, hp, qi: (0, hp, 0, 0)),
            pl.BlockSpec((1, hps, TKD, TQD), lambda bi, hp, qi: (1, hp, 0, 0)),
            pl.BlockSpec((1, DIFF_V_DIM), lambda bi, hp, qi: (0, 0)),
        ],
        out_specs=pl.BlockSpec((1, TQD, wide), lambda bi, hp, qi: (bi, qi, hp)),
        out_shape=jax.ShapeDtypeStruct((b, s, DIFF_HEADS * DIFF_V_DIM), jnp.bfloat16),
        scratch_shapes=[
            pltpu.VMEM((hps, s // TKD, DIFF_V_DIM + ONES_ROWS, TKD), jnp.bfloat16),
            pltpu.VMEM((2 * hps, 8, TQD), jnp.float32),
            pltpu.VMEM((2 * hps, DIFF_V_DIM + ONES_ROWS, TQD), jnp.float32),
        ],
        compiler_params=pltpu.CompilerParams(
            dimension_semantics=("arbitrary", "arbitrary", "arbitrary"),
            vmem_limit_bytes=VMEM_LIMIT),
        name="diff_attn",
    )(cfar, lam, proj, proj, proj, bias_tiles, bias_tiles, w_out_norm)


def _softplus(z):
    return jnp.maximum(z, 0.0) + jnp.log(1.0 + jnp.exp(-jnp.abs(z)))


def _sb_attn_kernel(q_ref, k_ref, v_ref, tri_ref, mask_ref, w_ref, o_ref, vt_sc, r_sc, acc_sc,
                    rmin_sc):
    qi = pl.program_id(1)
    tri_t = tri_ref[...]

    @pl.when(qi == 0)
    def _build_vt():
        for j in range(SB_PAIRS):
            for t in range(vt_sc.shape[1]):
                blk = v_ref[0, t * TK:(t + 1) * TK, j * LANES:(j + 1) * LANES]
                vt_sc[j, t] = blk.astype(jnp.float32).T.astype(vt_sc.dtype)

    qq = [_stack_subheads(q_ref[0, :, j * LANES:(j + 1) * LANES]) for j in range(SB_PAIRS)]

    def run_tiles(tiles):
        work = [(kt, diagonal, j) for (kt, diagonal) in tiles for j in range(SB_PAIRS)]
        z = []
        for kt, _, j in work:
            ks = pl.multiple_of(kt * TK, TK)
            z.append(lax.dot_general(k_ref[0, pl.ds(ks, TK), j * LANES:(j + 1) * LANES], qq[j],
                                     NT_DIMS, preferred_element_type=jnp.float32))
        csum = []
        for n, (kt, diagonal, j) in enumerate(work):
            sp = _softplus(z[n])
            c = sp * mask_ref[...] if diagonal else sp
            hi, lo = _split_hi_lo(c)
            csum.append(jnp.dot(tri_t, hi, preferred_element_type=jnp.float32)
                        + jnp.dot(tri_t, lo, preferred_element_type=jnp.float32))
        r_min = None
        for n, (kt, diagonal, j) in enumerate(work):
            if diagonal:
                a = jnp.exp(z[n] - csum[n]) * mask_ref[...]
                r_new = csum[n][0:1, :]
            else:
                r = r_sc[j][0:1, :]
                a = jnp.exp(z[n] - csum[n] - r)
                r_new = r + csum[n][0:1, :]
            av = jnp.dot(vt_sc[j, kt], a.astype(jnp.bfloat16), preferred_element_type=jnp.float32)
            acc_sc[j] = av if diagonal else acc_sc[j] + av
            r_sc[j] = jnp.broadcast_to(r_new, (8, 2 * TQ))
            if n >= len(work) - SB_PAIRS:
                r_min = jnp.min(r_new) if r_min is None else jnp.minimum(r_min, jnp.min(r_new))
        return r_min

    @pl.when(qi == 0)
    def _first_query_tile():
        run_tiles([(qi, True)])
        rmin_sc[0] = jnp.float32(0.0)

    @pl.when(qi >= 1)
    def _diagonal_and_previous():
        rmin_sc[0] = run_tiles([(qi, True), (qi - 1, False)])

    r_min0 = rmin_sc[0]

    def more(carry):
        i, r_min = carry
        return jnp.logical_and(i < qi, r_min < SB_STOP_COST)

    def step(carry):
        i, _ = carry
        return i + 1, run_tiles([(qi - 1 - i, False)])

    lax.while_loop(more, step, (jnp.int32(1), r_min0))

    lane = lax.broadcasted_iota(jnp.int32, (TQ, LANES), 1)
    first = lane < SB_DIM
    for j in range(SB_PAIRS):
        acc = acc_sc[j].T
        o = jnp.where(first, acc[:TQ], acc[TQ:])
        o2 = o * o
        ss_a = jnp.sum(jnp.where(first, o2, 0.0), axis=-1, keepdims=True)
        ss_b = jnp.sum(jnp.where(first, 0.0, o2), axis=-1, keepdims=True)
        ms = jnp.where(first, ss_a, ss_b) * (1.0 / SB_DIM)
        o_ref[0, :, j * LANES:(j + 1) * LANES] = (
            (o * lax.rsqrt(ms + EPS)) * w_ref[...]).astype(o_ref.dtype)


def _sb_attn(proj, tri_t, mask_t, w_out_norm):
    b, s, _ = proj.shape
    nq = s // TQ
    wide = SB_PAIRS * LANES
    return pl.pallas_call(
        _sb_attn_kernel,
        grid=(b, nq),
        in_specs=[
            pl.BlockSpec((1, TQ, wide), lambda bi, qi: (bi, qi, QS_BLK // SB_PAIRS)),
            pl.BlockSpec((1, s, wide), lambda bi, qi: (bi, 0, KS_BLK // SB_PAIRS)),
            pl.BlockSpec((1, s, wide), lambda bi, qi: (bi, 0, VS_BLK // SB_PAIRS)),
            pl.BlockSpec((TK, TK), lambda bi, qi: (0, 0)),
            pl.BlockSpec((TK, 2 * TQ), lambda bi, qi: (0, 0)),
            pl.BlockSpec((1, LANES), lambda bi, qi: (0, 0)),
        ],
        out_specs=pl.BlockSpec((1, TQ, wide), lambda bi, qi: (bi, qi, 0)),
        out_shape=jax.ShapeDtypeStruct((b, s, SB_HEADS * SB_DIM), jnp.bfloat16),
        scratch_shapes=[
            pltpu.VMEM((SB_PAIRS, s // TK, LANES, TK), jnp.bfloat16),
            pltpu.VMEM((SB_PAIRS, 8, 2 * TQ), jnp.float32),
            pltpu.VMEM((SB_PAIRS, LANES, 2 * TQ), jnp.float32),
            pltpu.SMEM((1,), jnp.float32),
        ],
        compiler_params=pltpu.CompilerParams(
            dimension_semantics=("arbitrary", "arbitrary"),
            vmem_limit_bytes=VMEM_LIMIT),
        name="sb_attn",
    )(proj, proj, proj, tri_t, mask_t, w_out_norm)


def _out_ffn_kernel(x_ref, yd_ref, ys_ref, woa_ref, wob_ref, w2_ref, wg_ref, wu_ref, wd_ref,
                    o_ref):
    h1 = (x_ref[...]
          + jnp.dot(yd_ref[...], woa_ref[...], preferred_element_type=jnp.float32)
          + jnp.dot(ys_ref[...], wob_ref[...], preferred_element_type=jnp.float32))
    ms = jnp.mean(h1 * h1, axis=-1, keepdims=True)
    u2 = ((h1 * lax.rsqrt(ms + EPS)) * w2_ref[...]).astype(jnp.bfloat16)
    ffn = None
    for c in range(D_FF // FF_CHUNK):
        sl = slice(c * FF_CHUNK, (c + 1) * FF_CHUNK)
        gate = jnp.dot(u2, wg_ref[:, sl], preferred_element_type=jnp.float32)
        up = jnp.dot(u2, wu_ref[:, sl], preferred_element_type=jnp.float32)
        act = (gate * (1.0 / (1.0 + jnp.exp(-gate))) * up).astype(jnp.bfloat16)
        part = jnp.dot(act, wd_ref[sl, :], preferred_element_type=jnp.float32)
        ffn = part if ffn is None else ffn + part
    o_ref[...] = h1 + ffn


def _out_ffn(x2d, yd, ys, wo, norm2_w, wg, wu, wd):
    n = x2d.shape[0]
    const = lambda i: (0, 0)
    half = DIFF_HEADS * DIFF_V_DIM
    resident = partial(pl.BlockSpec, index_map=const, pipeline_mode=pl.Buffered(1))
    return pl.pallas_call(
        _out_ffn_kernel,
        grid=(n // TM_PROJ,),
        in_specs=[
            pl.BlockSpec((TM_PROJ, D_MODEL), lambda i: (i, 0)),
            pl.BlockSpec((TM_PROJ, half), lambda i: (i, 0)),
            pl.BlockSpec((TM_PROJ, half), lambda i: (i, 0)),
            resident((half, D_MODEL)),
            pl.BlockSpec((half, D_MODEL), lambda i: (1, 0), pipeline_mode=pl.Buffered(1)),
            pl.BlockSpec((1, D_MODEL), const),
            resident((D_MODEL, D_FF)),
            resident((D_MODEL, D_FF)),
            resident((D_FF, D_MODEL)),
        ],
        out_specs=pl.BlockSpec((TM_PROJ, D_MODEL), lambda i: (i, 0)),
        out_shape=jax.ShapeDtypeStruct((n, D_MODEL), jnp.float32),
        compiler_params=pltpu.CompilerParams(
            dimension_semantics=("arbitrary",), vmem_limit_bytes=VMEM_LIMIT),
        name="out_ffn",
    )(x2d, yd, ys, wo, wo, norm2_w, wg, wu, wd)


def _bias_tiles_kernel(thr_ref, vals_ref, o_ref):
    t = pl.program_id(0)
    h = pl.program_id(1)
    kpos = lax.broadcasted_iota(jnp.int32, (TKD, TQD), 0) - t * TKD
    qpos = lax.broadcasted_iota(jnp.int32, (TKD, TQD), 1)
    rel = kpos - qpos
    dist = jnp.abs(rel)
    half = NUM_BUCKETS // 2
    behind = jnp.full((TKD, TQD), vals_ref[h, 0], jnp.float32)
    ahead = jnp.full((TKD, TQD), vals_ref[h, half], jnp.float32)
    for i in range(1, half):
        reached = dist >= thr_ref[i]
        behind = jnp.where(reached, vals_ref[h, i], behind)
        ahead = jnp.where(reached, vals_ref[h, half + i], ahead)
    bias = jnp.where(rel > 0, ahead, behind)
    shift = CHUNK.bit_length() - 1
    allowed = (kpos >> shift) <= (qpos >> shift)
    o_ref[0, 0] = jnp.where(allowed, bias, MASK_VALUE)


def _diff_bias_tiles(rel_bias):
    half = NUM_BUCKETS // 2
    idx = _t5_bucket(-jnp.arange(TQD + TKD, dtype=jnp.int32))
    thr = jnp.sum((idx[None, :] < jnp.arange(half, dtype=jnp.int32)[:, None]).astype(jnp.int32), axis=1)
    vals = rel_bias.astype(jnp.float32).T * LOG2E
    smem = pl.BlockSpec(memory_space=pltpu.SMEM)
    return pl.pallas_call(
        _bias_tiles_kernel,
        grid=(2, DIFF_HEADS),
        in_specs=[smem, smem],
        out_specs=pl.BlockSpec((1, 1, TKD, TQD), lambda t, h: (t, h, 0, 0)),
        out_shape=jax.ShapeDtypeStruct((2, DIFF_HEADS, TKD, TQD), jnp.float32),
        compiler_params=pltpu.CompilerParams(dimension_semantics=("arbitrary", "arbitrary")),
        name="bias_tiles",
    )(thr.astype(jnp.int32), vals)


def kernel(x, norm1_w, w_in, q_norm_w, k_norm_w, lambda_q1, lambda_k1, lambda_q2, lambda_k2,
           diff_out_norm_w, sb_out_norm_w, w_out, norm2_w, w_gate, w_up, w_down, rel_bias):
    b, s, d = x.shape
    assert (b, s, d) == (8, 4096, D_MODEL) and w_in.shape[0] == 1
    f32, bf16 = jnp.float32, jnp.bfloat16
    x2d = x.reshape(b * s, d)

    qkw = jnp.concatenate([jnp.tile(q_norm_w[0].astype(f32), 2 * DIFF_HEADS)
                           * (DIFF_QK_DIM ** -0.5 * LOG2E),
                           jnp.tile(k_norm_w[0].astype(f32), 2 * DIFF_HEADS)])[None, :]
    grp = jnp.arange(GROUP_SLAB) // DIFF_QK_DIM
    gmat = jnp.where(grp[:, None] == grp[None, :], 1.0 / DIFF_QK_DIM, 0.0).astype(bf16)
    lam = (jnp.exp(jnp.sum(lambda_q1[0].astype(f32) * lambda_k1[0].astype(f32)))
           - jnp.exp(jnp.sum(lambda_q2[0].astype(f32) * lambda_k2[0].astype(f32)))
           + LAMBDA_INIT).reshape(1)
    bias_tiles = _diff_bias_tiles(rel_bias)
    cfar = rel_bias.astype(f32)[_t5_bucket(jnp.int32(-(TKD + 1)))] * LOG2E
    jj = jnp.arange(TK)
    tri_t = (jj[None, :] >= jj[:, None]).astype(bf16)
    causal_t = (jj[:, None] < jnp.arange(TQ)[None, :]).astype(f32)
    sb_mask_t = jnp.concatenate([causal_t, causal_t], axis=1)

    proj = _in_proj(x2d, norm1_w[0].astype(f32)[None, :], w_in[0].astype(bf16), qkw, gmat)
    proj = proj.reshape(b, s, IN_COLS)
    y_diff = _diff_attn(proj, bias_tiles, cfar, lam, diff_out_norm_w[0].astype(f32)[None, :])
    y_sb = _sb_attn(proj, tri_t, sb_mask_t, jnp.tile(sb_out_norm_w[0].astype(f32), 2)[None, :])

    half = DIFF_HEADS * DIFF_V_DIM
    out = _out_ffn(x2d, y_diff.reshape(b * s, half), y_sb.reshape(b * s, half),
                   w_out[0].astype(bf16), norm2_w[0].astype(f32)[None, :],
                   w_gate[0].astype(bf16), w_up[0].astype(bf16), w_down[0].astype(bf16))
    return out.reshape(b, s, d)
```

```python
import math
from functools import partial

import jax
import jax.numpy as jnp
from jax import lax
from jax.experimental import pallas as pl
from jax.experimental.pallas import tpu as pltpu

D_MODEL = 1024
CHUNK = 64
DIFF_QK_DIM = 64
DIFF_V_DIM = 128
DIFF_HEADS = 4
SB_DIM = 64
SB_HEADS = 8
NUM_BUCKETS = 32
MAX_DISTANCE = 128
D_FF = 2816
EPS = 1e-6
IN_COLS = 3072
LAMBDA_INIT = 0.8 - 0.6 * math.exp(-0.3 * 0)
LANES = 128

QD_BLK, KD_BLK, VD_BLK = 0, 4, 8
QS_BLK, KS_BLK, VS_BLK = 12, 16, 20

TM_PROJ = 512
COL_CHUNK = 512
GROUP_SLAB = 256
TQD = 512
TKD = 512
HEADS_PER_STEP = 4
ONES_ROWS = 16
SCORE_LOOKAHEAD = 2
LOG2E = math.log2(math.e)
TQ = 256
TK = 256
SB_PAIRS = SB_HEADS // 2
FF_CHUNK = 256
SAFE_SHIFT_GAP = 90.0
BOUND_SLACK = 1.01
BOUND_MARGIN = 0.5
MASK_VALUE = -1e30
SB_STOP_COST = 104.0
VMEM_LIMIT = 56 * 1024 * 1024

NT_DIMS = (((1,), (1,)), ((), ()))


def _t5_bucket(rel):
    nb = NUM_BUCKETS // 2
    max_exact = nb // 2
    ret = (rel > 0).astype(jnp.int32) * nb
    n = jnp.abs(rel)
    nf = jnp.maximum(n, 1).astype(jnp.float32)
    large = max_exact + (jnp.log(nf / max_exact) / jnp.log(jnp.float32(MAX_DISTANCE / max_exact))
                         * (nb - max_exact)).astype(jnp.int32)
    large = jnp.minimum(large, nb - 1)
    return ret + jnp.where(n < max_exact, n, large)


def _split_hi_lo(x):
    hi = x.astype(jnp.bfloat16)
    lo = (x - hi.astype(jnp.float32)).astype(jnp.bfloat16)
    return hi, lo


def _in_proj_kernel(x_ref, w1_ref, win_ref, qkw_ref, g_ref, o_ref):
    x = x_ref[...]
    ms = jnp.mean(x * x, axis=-1, keepdims=True)
    u = ((x * lax.rsqrt(ms + EPS)) * w1_ref[...]).astype(jnp.bfloat16)
    g = g_ref[...]
    for c in range(IN_COLS // COL_CHUNK):
        lo_c, hi_c = c * COL_CHUNK, (c + 1) * COL_CHUNK
        p = jnp.dot(u, win_ref[:, lo_c:hi_c], preferred_element_type=jnp.float32)
        if hi_c <= 2 * DIFF_HEADS * 2 * DIFF_QK_DIM:
            for j in range(COL_CHUNK // GROUP_SLAB):
                sl = slice(j * GROUP_SLAB, (j + 1) * GROUP_SLAB)
                pj = p[:, sl]
                msq = jnp.dot((pj * pj).astype(jnp.bfloat16), g, preferred_element_type=jnp.float32)
                w = qkw_ref[:, lo_c + j * GROUP_SLAB: lo_c + (j + 1) * GROUP_SLAB]
                o_ref[:, lo_c + j * GROUP_SLAB: lo_c + (j + 1) * GROUP_SLAB] = (
                    (pj * lax.rsqrt(msq + EPS)) * w).astype(jnp.bfloat16)
        else:
            if lo_c == QS_BLK * LANES:
                p = p * (SB_DIM ** -0.5)
            o_ref[:, lo_c:hi_c] = p.astype(jnp.bfloat16)


def _in_proj(x2d, norm1_w, w_in_bf, qkw, gmat):
    n = x2d.shape[0]
    const = lambda i: (0, 0)
    return pl.pallas_call(
        _in_proj_kernel,
        grid=(n // TM_PROJ,),
        in_specs=[
            pl.BlockSpec((TM_PROJ, D_MODEL), lambda i: (i, 0)),
            pl.BlockSpec((1, D_MODEL), const),
            pl.BlockSpec((D_MODEL, IN_COLS), const, pipeline_mode=pl.Buffered(1)),
            pl.BlockSpec((1, 2 * DIFF_HEADS * 2 * DIFF_QK_DIM), const),
            pl.BlockSpec((GROUP_SLAB, GROUP_SLAB), const),
        ],
        out_specs=pl.BlockSpec((TM_PROJ, IN_COLS), lambda i: (i, 0)),
        out_shape=jax.ShapeDtypeStruct((n, IN_COLS), jnp.bfloat16),
        compiler_params=pltpu.CompilerParams(
            dimension_semantics=("arbitrary",), vmem_limit_bytes=VMEM_LIMIT),
        name="in_proj",
    )(x2d, norm1_w, w_in_bf, qkw, gmat)


def _stack_subheads(q):
    lane = lax.broadcasted_iota(jnp.int32, q.shape, 1)
    zero = jnp.zeros_like(q)
    return jnp.concatenate([jnp.where(lane < SB_DIM, q, zero),
                            jnp.where(lane >= SB_DIM, q, zero)], axis=0)


def _diff_attn_kernel(bsc_ref, lam_ref, q_ref, k_ref, v_ref, bdiag_ref, bsub_ref, w_ref, o_ref,
                      vt_sc, m_sc, acc_sc, kmax_sc):
    hp = pl.program_id(1)
    qi = pl.program_id(2)
    n_maps = 2 * HEADS_PER_STEP
    lane = lax.broadcasted_iota(jnp.int32, (TQD, LANES), 1)
    sub_lanes = (lane < DIFF_QK_DIM, lane >= DIFF_QK_DIM)
    ones8 = jnp.ones((8, LANES), jnp.bfloat16)

    def lane_sums(x):
        return lax.dot_general(ones8, x.astype(jnp.bfloat16), NT_DIMS,
                               preferred_element_type=jnp.float32)[0:1, :]

    @pl.when(qi == 0)
    def _per_sequence_setup():
        ones = jnp.ones((ONES_ROWS, TKD), vt_sc.dtype)
        for hh in range(HEADS_PER_STEP):
            kmax = [None, None]
            for t in range(vt_sc.shape[1]):
                blk = v_ref[0, t * TKD:(t + 1) * TKD, hh * LANES:(hh + 1) * LANES]
                vt_sc[hh, t, :DIFF_V_DIM, :] = blk.astype(jnp.float32).T.astype(vt_sc.dtype)
                vt_sc[hh, t, DIFF_V_DIM:, :] = ones
                kf = k_ref[0, t * TKD:(t + 1) * TKD, hh * LANES:(hh + 1) * LANES].astype(jnp.float32)
                for c in range(2):
                    n2 = jnp.max(lane_sums(jnp.where(sub_lanes[c], kf * kf, 0.0)))
                    kmax[c] = n2 if kmax[c] is None else jnp.maximum(kmax[c], n2)
            for c in range(2):
                kmax_sc[2 * hh + c] = kmax[c]

    zero = jnp.zeros((TQD, LANES), q_ref.dtype)
    q_sub = []
    for hh in range(HEADS_PER_STEP):
        q = q_ref[0, :, hh * LANES:(hh + 1) * LANES]
        q_sub.append(jnp.where(sub_lanes[0], q, zero))
        q_sub.append(jnp.where(sub_lanes[1], q, zero))

    q0 = pl.multiple_of(qi * TQD, TQD)
    gap = None
    for idx in range(n_maps):
        hh = idx // 2
        head = HEADS_PER_STEP * hp + hh
        qf = q_sub[idx].astype(jnp.float32)
        k_own = k_ref[0, pl.ds(q0, TQD), hh * LANES:(hh + 1) * LANES].astype(jnp.float32)
        own = lane_sums(qf * k_own) + bsc_ref[1, head]
        bound = (jnp.sqrt(lane_sums(qf * qf) * kmax_sc[idx]) * BOUND_SLACK
                 + (bsc_ref[2, head] + BOUND_MARGIN))
        m_sc[idx] = jnp.broadcast_to(bound, (8, TQD))
        g = jnp.max(bound - own)
        gap = g if gap is None else jnp.maximum(gap, g)
    fixed_shift_ok = gap <= SAFE_SHIFT_GAP

    def run_tiles(tiles, fixed):
        def scores(kt, idx):
            ks = pl.multiple_of(kt * TKD, TKD)
            k = k_ref[0, pl.ds(ks, TKD), (idx // 2) * LANES:(idx // 2 + 1) * LANES]
            return lax.dot_general(k, q_sub[idx], NT_DIMS, preferred_element_type=jnp.float32)

        def weights(s, idx, bias_ref, cfar, first):
            hh = idx // 2
            if fixed:
                shift = m_sc[idx][0:1, :]
                if bias_ref is None:
                    shift = shift - cfar[hh]
                else:
                    s = s + bias_ref[0, hh]
                return jnp.exp2(s - shift).astype(jnp.bfloat16), None
            if first:
                s = s + bias_ref[0, hh]
                m_new = jnp.max(s, axis=0, keepdims=True)
                m_sc[idx] = jnp.broadcast_to(m_new, (8, TQD))
                return jnp.exp2(s - m_new).astype(jnp.bfloat16), None
            m_old = m_sc[idx][0:1, :]
            if bias_ref is None:
                m_new = jnp.maximum(m_old, jnp.max(s, axis=0, keepdims=True) + cfar[hh])
                shift = m_new - cfar[hh]
            else:
                s = s + bias_ref[0, hh]
                m_new = jnp.maximum(m_old, jnp.max(s, axis=0, keepdims=True))
                shift = m_new
            m_sc[idx] = jnp.broadcast_to(m_new, (8, TQD))
            return jnp.exp2(s - shift).astype(jnp.bfloat16), jnp.exp2(m_old - m_new)

        def accumulate(kt, idx, p, alpha, first):
            pv = jnp.dot(vt_sc[idx // 2, kt], p, preferred_element_type=jnp.float32)
            if first:
                acc_sc[idx] = pv
            elif alpha is None:
                acc_sc[idx] = acc_sc[idx] + pv
            else:
                acc_sc[idx] = alpha * acc_sc[idx] + pv

        work = [(kt, idx, bias_ref, cfar, first)
                for (kt, bias_ref, cfar, first) in tiles for idx in range(n_maps)]
        s_next = [scores(w[0], w[1]) for w in work[:SCORE_LOOKAHEAD]]
        pending = None
        for n, (kt, idx, bias_ref, cfar, first) in enumerate(work):
            p, alpha = weights(s_next.pop(0), idx, bias_ref, cfar, first)
            if n + SCORE_LOOKAHEAD < len(work):
                nxt = work[n + SCORE_LOOKAHEAD]
                s_next.append(scores(nxt[0], nxt[1]))
            if pending is not None:
                accumulate(*pending)
            pending = (kt, idx, p, alpha, first)
        accumulate(*pending)

    cfar = [bsc_ref[0, HEADS_PER_STEP * hp + hh] for hh in range(HEADS_PER_STEP)]
    n_far = jnp.maximum(qi - 1, 0)

    def all_tiles(fixed):
        @pl.when(qi == 0)
        def _first_query_tile():
            run_tiles([(qi, bdiag_ref, None, True)], fixed)

        @pl.when(qi >= 1)
        def _near_tiles():
            run_tiles([(qi, bdiag_ref, None, True), (qi - 1, bsub_ref, None, False)], fixed)

        def far_pair(i, carry):
            run_tiles([(2 * i, None, cfar, False), (2 * i + 1, None, cfar, False)], fixed)
            return carry

        lax.fori_loop(0, n_far // 2, far_pair, 0)

        @pl.when(n_far % 2 == 1)
        def _last_far_tile():
            run_tiles([(n_far - 1, None, cfar, False)], fixed)

    @pl.when(fixed_shift_ok)
    def _fixed_shift_softmax():
        all_tiles(True)

    @pl.when(jnp.logical_not(fixed_shift_ok))
    def _online_softmax():
        all_tiles(False)

    for hh in range(HEADS_PER_STEP):
        acc0 = acc_sc[2 * hh]
        acc1 = acc_sc[2 * hh + 1]
        o_t = (acc0[:DIFF_V_DIM] / acc0[DIFF_V_DIM:DIFF_V_DIM + 1]
               - lam_ref[0] * (acc1[:DIFF_V_DIM] / acc1[DIFF_V_DIM:DIFF_V_DIM + 1]))
        o = o_t.T
        ms = jnp.mean(o * o, axis=-1, keepdims=True)
        y = ((o * lax.rsqrt(ms + EPS)) * w_ref[...]) * (1.0 - LAMBDA_INIT)
        o_ref[0, :, hh * LANES:(hh + 1) * LANES] = y.astype(o_ref.dtype)


def _diff_attn(proj, bias_tiles, bias_scalars, lam, w_out_norm):
    b, s, _ = proj.shape
    nq = s // TQD
    hps = HEADS_PER_STEP
    wide = hps * LANES
    smem = pl.BlockSpec(memory_space=pltpu.SMEM)
    return pl.pallas_call(
        _diff_attn_kernel,
        grid=(b, DIFF_HEADS // hps, nq),
        in_specs=[
            smem, smem,
            pl.BlockSpec((1, TQD, wide), lambda bi, hp, qi: (bi, qi, QD_BLK // hps + hp)),
            pl.BlockSpec((1, s, wide), lambda bi, hp, qi: (bi, 0, KD_BLK // hps + hp)),
            pl.BlockSpec((1, s, wide), lambda bi, hp, qi: (bi, 0, VD_BLK // hps + hp)),
            pl.BlockSpec((1, hps, TKD, TQD), lambda bi, hp, qi: (0, hp, 0, 0)),
            pl.BlockSpec((1, hps, TKD, TQD), lambda bi, hp, qi: (1, hp, 0, 0)),
            pl.BlockSpec((1, DIFF_V_DIM), lambda bi, hp, qi: (0, 0)),
        ],
        out_specs=pl.BlockSpec((1, TQD, wide), lambda bi, hp, qi: (bi, qi, hp)),
        out_shape=jax.ShapeDtypeStruct((b, s, DIFF_HEADS * DIFF_V_DIM), jnp.bfloat16),
        scratch_shapes=[
            pltpu.VMEM((hps, s // TKD, DIFF_V_DIM + ONES_ROWS, TKD), jnp.bfloat16),
            pltpu.VMEM((2 * hps, 8, TQD), jnp.float32),
            pltpu.VMEM((2 * hps, DIFF_V_DIM + ONES_ROWS, TQD), jnp.float32),
            pltpu.SMEM((2 * hps,), jnp.float32),
        ],
        compiler_params=pltpu.CompilerParams(
            dimension_semantics=("arbitrary", "arbitrary", "arbitrary"),
            vmem_limit_bytes=VMEM_LIMIT),
        name="diff_attn",
    )(bias_scalars, lam, proj, proj, proj, bias_tiles, bias_tiles, w_out_norm)


def _softplus(z):
    return jnp.maximum(z, 0.0) + jnp.log(1.0 + jnp.exp(-jnp.abs(z)))


def _sb_attn_kernel(q_ref, k_ref, v_ref, tri_ref, mask_ref, w_ref, o_ref, vt_sc, r_sc, acc_sc,
                    rmin_sc):
    qi = pl.program_id(1)
    tri_t = tri_ref[...]

    @pl.when(qi == 0)
    def _build_vt():
        for j in range(SB_PAIRS):
            for t in range(vt_sc.shape[1]):
                blk = v_ref[0, t * TK:(t + 1) * TK, j * LANES:(j + 1) * LANES]
                vt_sc[j, t] = blk.astype(jnp.float32).T.astype(vt_sc.dtype)

    qq = [_stack_subheads(q_ref[0, :, j * LANES:(j + 1) * LANES]) for j in range(SB_PAIRS)]

    def run_tiles(tiles):
        work = [(kt, diagonal, j) for (kt, diagonal) in tiles for j in range(SB_PAIRS)]
        z = []
        for kt, _, j in work:
            ks = pl.multiple_of(kt * TK, TK)
            z.append(lax.dot_general(k_ref[0, pl.ds(ks, TK), j * LANES:(j + 1) * LANES], qq[j],
                                     NT_DIMS, preferred_element_type=jnp.float32))
        csum = []
        for n, (kt, diagonal, j) in enumerate(work):
            sp = _softplus(z[n])
            c = sp * mask_ref[...] if diagonal else sp
            hi, lo = _split_hi_lo(c)
            csum.append(jnp.dot(tri_t, hi, preferred_element_type=jnp.float32)
                        + jnp.dot(tri_t, lo, preferred_element_type=jnp.float32))
        r_min = None
        for n, (kt, diagonal, j) in enumerate(work):
            if diagonal:
                a = jnp.exp(z[n] - csum[n]) * mask_ref[...]
                r_new = csum[n][0:1, :]
            else:
                r = r_sc[j][0:1, :]
                a = jnp.exp(z[n] - csum[n] - r)
                r_new = r + csum[n][0:1, :]
            av = jnp.dot(vt_sc[j, kt], a.astype(jnp.bfloat16), preferred_element_type=jnp.float32)
            acc_sc[j] = av if diagonal else acc_sc[j] + av
            r_sc[j] = jnp.broadcast_to(r_new, (8, 2 * TQ))
            if n >= len(work) - SB_PAIRS:
                r_min = jnp.min(r_new) if r_min is None else jnp.minimum(r_min, jnp.min(r_new))
        return r_min

    @pl.when(qi == 0)
    def _first_query_tile():
        run_tiles([(qi, True)])
        rmin_sc[0] = jnp.float32(0.0)

    @pl.when(qi >= 1)
    def _diagonal_and_previous():
        rmin_sc[0] = run_tiles([(qi, True), (qi - 1, False)])

    r_min0 = rmin_sc[0]

    def more(carry):
        i, r_min = carry
        return jnp.logical_and(i < qi, r_min < SB_STOP_COST)

    def step(carry):
        i, _ = carry
        return i + 1, run_tiles([(qi - 1 - i, False)])

    lax.while_loop(more, step, (jnp.int32(1), r_min0))

    lane = lax.broadcasted_iota(jnp.int32, (TQ, LANES), 1)
    first = lane < SB_DIM
    for j in range(SB_PAIRS):
        acc = acc_sc[j].T
        o = jnp.where(first, acc[:TQ], acc[TQ:])
        o2 = o * o
        ss_a = jnp.sum(jnp.where(first, o2, 0.0), axis=-1, keepdims=True)
        ss_b = jnp.sum(jnp.where(first, 0.0, o2), axis=-1, keepdims=True)
        ms = jnp.where(first, ss_a, ss_b) * (1.0 / SB_DIM)
        o_ref[0, :, j * LANES:(j + 1) * LANES] = (
            (o * lax.rsqrt(ms + EPS)) * w_ref[...]).astype(o_ref.dtype)


def _sb_attn(proj, tri_t, mask_t, w_out_norm):
    b, s, _ = proj.shape
    nq = s // TQ
    wide = SB_PAIRS * LANES
    return pl.pallas_call(
        _sb_attn_kernel,
        grid=(b, nq),
        in_specs=[
            pl.BlockSpec((1, TQ, wide), lambda bi, qi: (bi, qi, QS_BLK // SB_PAIRS)),
            pl.BlockSpec((1, s, wide), lambda bi, qi: (bi, 0, KS_BLK // SB_PAIRS)),
            pl.BlockSpec((1, s, wide), lambda bi, qi: (bi, 0, VS_BLK // SB_PAIRS)),
            pl.BlockSpec((TK, TK), lambda bi, qi: (0, 0)),
            pl.BlockSpec((TK, 2 * TQ), lambda bi, qi: (0, 0)),
            pl.BlockSpec((1, LANES), lambda bi, qi: (0, 0)),
        ],
        out_specs=pl.BlockSpec((1, TQ, wide), lambda bi, qi: (bi, qi, 0)),
        out_shape=jax.ShapeDtypeStruct((b, s, SB_HEADS * SB_DIM), jnp.bfloat16),
        scratch_shapes=[
            pltpu.VMEM((SB_PAIRS, s // TK, LANES, TK), jnp.bfloat16),
            pltpu.VMEM((SB_PAIRS, 8, 2 * TQ), jnp.float32),
            pltpu.VMEM((SB_PAIRS, LANES, 2 * TQ), jnp.float32),
            pltpu.SMEM((1,), jnp.float32),
        ],
        compiler_params=pltpu.CompilerParams(
            dimension_semantics=("arbitrary", "arbitrary"),
            vmem_limit_bytes=VMEM_LIMIT),
        name="sb_attn",
    )(proj, proj, proj, tri_t, mask_t, w_out_norm)


def _out_ffn_kernel(x_ref, yd_ref, ys_ref, woa_ref, wob_ref, w2_ref, wg_ref, wu_ref, wd_ref,
                    o_ref):
    h1 = (x_ref[...]
          + jnp.dot(yd_ref[...], woa_ref[...], preferred_element_type=jnp.float32)
          + jnp.dot(ys_ref[...], wob_ref[...], preferred_element_type=jnp.float32))
    ms = jnp.mean(h1 * h1, axis=-1, keepdims=True)
    u2 = ((h1 * lax.rsqrt(ms + EPS)) * w2_ref[...]).astype(jnp.bfloat16)
    ffn = None
    for c in range(D_FF // FF_CHUNK):
        sl = slice(c * FF_CHUNK, (c + 1) * FF_CHUNK)
        gate = jnp.dot(u2, wg_ref[:, sl], preferred_element_type=jnp.float32)
        up = jnp.dot(u2, wu_ref[:, sl], preferred_element_type=jnp.float32)
        act = (gate * (1.0 / (1.0 + jnp.exp(-gate))) * up).astype(jnp.bfloat16)
        part = jnp.dot(act, wd_ref[sl, :], preferred_element_type=jnp.float32)
        ffn = part if ffn is None else ffn + part
    o_ref[...] = h1 + ffn


def _out_ffn(x2d, yd, ys, wo, norm2_w, wg, wu, wd):
    n = x2d.shape[0]
    const = lambda i: (0, 0)
    half = DIFF_HEADS * DIFF_V_DIM
    resident = partial(pl.BlockSpec, index_map=const, pipeline_mode=pl.Buffered(1))
    return pl.pallas_call(
        _out_ffn_kernel,
        grid=(n // TM_PROJ,),
        in_specs=[
            pl.BlockSpec((TM_PROJ, D_MODEL), lambda i: (i, 0)),
            pl.BlockSpec((TM_PROJ, half), lambda i: (i, 0)),
            pl.BlockSpec((TM_PROJ, half), lambda i: (i, 0)),
            resident((half, D_MODEL)),
            pl.BlockSpec((half, D_MODEL), lambda i: (1, 0), pipeline_mode=pl.Buffered(1)),
            pl.BlockSpec((1, D_MODEL), const),
            resident((D_MODEL, D_FF)),
            resident((D_MODEL, D_FF)),
            resident((D_FF, D_MODEL)),
        ],
        out_specs=pl.BlockSpec((TM_PROJ, D_MODEL), lambda i: (i, 0)),
        out_shape=jax.ShapeDtypeStruct((n, D_MODEL), jnp.float32),
        compiler_params=pltpu.CompilerParams(
            dimension_semantics=("arbitrary",), vmem_limit_bytes=VMEM_LIMIT),
        name="out_ffn",
    )(x2d, yd, ys, wo, wo, norm2_w, wg, wu, wd)


def _bias_tiles_kernel(thr_ref, vals_ref, o_ref):
    t = pl.program_id(0)
    h = pl.program_id(1)
    kpos = lax.broadcasted_iota(jnp.int32, (TKD, TQD), 0) - t * TKD
    qpos = lax.broadcasted_iota(jnp.int32, (TKD, TQD), 1)
    rel = kpos - qpos
    dist = jnp.abs(rel)
    half = NUM_BUCKETS // 2
    behind = jnp.full((TKD, TQD), vals_ref[h, 0], jnp.float32)
    ahead = jnp.full((TKD, TQD), vals_ref[h, half], jnp.float32)
    for i in range(1, half):
        reached = dist >= thr_ref[i]
        behind = jnp.where(reached, vals_ref[h, i], behind)
        ahead = jnp.where(reached, vals_ref[h, half + i], ahead)
    bias = jnp.where(rel > 0, ahead, behind)
    shift = CHUNK.bit_length() - 1
    allowed = (kpos >> shift) <= (qpos >> shift)
    o_ref[0, 0] = jnp.where(allowed, bias, MASK_VALUE)


def _diff_bias_tiles(rel_bias):
    half = NUM_BUCKETS // 2
    idx = _t5_bucket(-jnp.arange(TQD + TKD, dtype=jnp.int32))
    thr = jnp.sum((idx[None, :] < jnp.arange(half, dtype=jnp.int32)[:, None]).astype(jnp.int32), axis=1)
    vals = rel_bias.astype(jnp.float32).T * LOG2E
    smem = pl.BlockSpec(memory_space=pltpu.SMEM)
    return pl.pallas_call(
        _bias_tiles_kernel,
        grid=(2, DIFF_HEADS),
        in_specs=[smem, smem],
        out_specs=pl.BlockSpec((1, 1, TKD, TQD), lambda t, h: (t, h, 0, 0)),
        out_shape=jax.ShapeDtypeStruct((2, DIFF_HEADS, TKD, TQD), jnp.float32),
        compiler_params=pltpu.CompilerParams(dimension_semantics=("arbitrary", "arbitrary")),
        name="bias_tiles",
    )(thr.astype(jnp.int32), vals)


def kernel(x, norm1_w, w_in, q_norm_w, k_norm_w, lambda_q1, lambda_k1, lambda_q2, lambda_k2,
           diff_out_norm_w, sb_out_norm_w, w_out, norm2_w, w_gate, w_up, w_down, rel_bias):
    b, s, d = x.shape
    assert (b, s, d) == (8, 4096, D_MODEL) and w_in.shape[0] == 1
    f32, bf16 = jnp.float32, jnp.bfloat16
    x2d = x.reshape(b * s, d)

    qkw = jnp.concatenate([jnp.tile(q_norm_w[0].astype(f32), 2 * DIFF_HEADS)
                           * (DIFF_QK_DIM ** -0.5 * LOG2E),
                           jnp.tile(k_norm_w[0].astype(f32), 2 * DIFF_HEADS)])[None, :]
    grp = jnp.arange(GROUP_SLAB) // DIFF_QK_DIM
    gmat = jnp.where(grp[:, None] == grp[None, :], 1.0 / DIFF_QK_DIM, 0.0).astype(bf16)
    lam = (jnp.exp(jnp.sum(lambda_q1[0].astype(f32) * lambda_k1[0].astype(f32)))
           - jnp.exp(jnp.sum(lambda_q2[0].astype(f32) * lambda_k2[0].astype(f32)))
           + LAMBDA_INIT).reshape(1)
    bias_tiles = _diff_bias_tiles(rel_bias)
    vals = rel_bias.astype(f32) * LOG2E
    bias_scalars = jnp.stack([vals[_t5_bucket(jnp.int32(-(TKD + 1)))],
                              vals[_t5_bucket(jnp.int32(0))],
                              jnp.max(vals, axis=0)])
    jj = jnp.arange(TK)
    tri_t = (jj[None, :] >= jj[:, None]).astype(bf16)
    causal_t = (jj[:, None] < jnp.arange(TQ)[None, :]).astype(f32)
    sb_mask_t = jnp.concatenate([causal_t, causal_t], axis=1)

    proj = _in_proj(x2d, norm1_w[0].astype(f32)[None, :], w_in[0].astype(bf16), qkw, gmat)
    proj = proj.reshape(b, s, IN_COLS)
    y_diff = _diff_attn(proj, bias_tiles, bias_scalars, lam,
                        diff_out_norm_w[0].astype(f32)[None, :])
    y_sb = _sb_attn(proj, tri_t, sb_mask_t, jnp.tile(sb_out_norm_w[0].astype(f32), 2)[None, :])

    half = DIFF_HEADS * DIFF_V_DIM
    out = _out_ffn(x2d, y_diff.reshape(b * s, half), y_sb.reshape(b * s, half),
                   w_out[0].astype(bf16), norm2_w[0].astype(f32)[None, :],
                   w_gate[0].astype(bf16), w_up[0].astype(bf16), w_down[0].astype(bf16))
    return out.reshape(b, s, d)
```

```python
import math
from functools import partial

import jax
import jax.numpy as jnp
from jax import lax
from jax.experimental import pallas as pl
from jax.experimental.pallas import tpu as pltpu

D_MODEL = 1024
CHUNK = 64
DIFF_QK_DIM = 64
DIFF_V_DIM = 128
DIFF_HEADS = 4
SB_DIM = 64
SB_HEADS = 8
NUM_BUCKETS = 32
MAX_DISTANCE = 128
D_FF = 2816
EPS = 1e-6
IN_COLS = 3072
LAMBDA_INIT = 0.8 - 0.6 * math.exp(-0.3 * 0)
LANES = 128

QD_BLK, KD_BLK, VD_BLK = 0, 4, 8
QS_BLK, KS_BLK, VS_BLK = 12, 16, 20

TM_PROJ = 512
COL_CHUNK = 512
GROUP_SLAB = 256
TQD = 512
TKD = 512
HEADS_PER_STEP = 4
ONES_ROWS = 16
SCORE_LOOKAHEAD = 2
LOG2E = math.log2(math.e)
TQ = 256
TK = 256
SB_PAIRS = SB_HEADS // 2
FF_CHUNK = 256
SAFE_SHIFT_GAP = 90.0
BOUND_SLACK = 1.01
BOUND_MARGIN = 0.5
MASK_VALUE = -1e30
SB_STOP_COST = 150.1
SOFTPLUS_LINEAR_FROM = 60.0
VMEM_LIMIT = 56 * 1024 * 1024

NT_DIMS = (((1,), (1,)), ((), ()))


def _t5_bucket(rel):
    nb = NUM_BUCKETS // 2
    max_exact = nb // 2
    ret = (rel > 0).astype(jnp.int32) * nb
    n = jnp.abs(rel)
    nf = jnp.maximum(n, 1).astype(jnp.float32)
    large = max_exact + (jnp.log(nf / max_exact) / jnp.log(jnp.float32(MAX_DISTANCE / max_exact))
                         * (nb - max_exact)).astype(jnp.int32)
    large = jnp.minimum(large, nb - 1)
    return ret + jnp.where(n < max_exact, n, large)


def _split_hi_lo(x):
    hi = x.astype(jnp.bfloat16)
    lo = (x - hi.astype(jnp.float32)).astype(jnp.bfloat16)
    return hi, lo


def _in_proj_kernel(x_ref, w1_ref, win_ref, qkw_ref, g_ref, o_ref):
    x = x_ref[...]
    ms = jnp.mean(x * x, axis=-1, keepdims=True)
    u = ((x * lax.rsqrt(ms + EPS)) * w1_ref[...]).astype(jnp.bfloat16)
    g = g_ref[...]
    for c in range(IN_COLS // COL_CHUNK):
        lo_c, hi_c = c * COL_CHUNK, (c + 1) * COL_CHUNK
        p = jnp.dot(u, win_ref[:, lo_c:hi_c], preferred_element_type=jnp.float32)
        if hi_c <= 2 * DIFF_HEADS * 2 * DIFF_QK_DIM:
            for j in range(COL_CHUNK // GROUP_SLAB):
                sl = slice(j * GROUP_SLAB, (j + 1) * GROUP_SLAB)
                pj = p[:, sl]
                msq = jnp.dot((pj * pj).astype(jnp.bfloat16), g, preferred_element_type=jnp.float32)
                w = qkw_ref[:, lo_c + j * GROUP_SLAB: lo_c + (j + 1) * GROUP_SLAB]
                o_ref[:, lo_c + j * GROUP_SLAB: lo_c + (j + 1) * GROUP_SLAB] = (
                    (pj * lax.rsqrt(msq + EPS)) * w).astype(jnp.bfloat16)
        else:
            if lo_c == QS_BLK * LANES:
                p = p * (SB_DIM ** -0.5 * LOG2E)
            o_ref[:, lo_c:hi_c] = p.astype(jnp.bfloat16)


def _in_proj(x2d, norm1_w, w_in_bf, qkw, gmat):
    n = x2d.shape[0]
    const = lambda i: (0, 0)
    return pl.pallas_call(
        _in_proj_kernel,
        grid=(n // TM_PROJ,),
        in_specs=[
            pl.BlockSpec((TM_PROJ, D_MODEL), lambda i: (i, 0)),
            pl.BlockSpec((1, D_MODEL), const),
            pl.BlockSpec((D_MODEL, IN_COLS), const, pipeline_mode=pl.Buffered(1)),
            pl.BlockSpec((1, 2 * DIFF_HEADS * 2 * DIFF_QK_DIM), const),
            pl.BlockSpec((GROUP_SLAB, GROUP_SLAB), const),
        ],
        out_specs=pl.BlockSpec((TM_PROJ, IN_COLS), lambda i: (i, 0)),
        out_shape=jax.ShapeDtypeStruct((n, IN_COLS), jnp.bfloat16),
        compiler_params=pltpu.CompilerParams(
            dimension_semantics=("arbitrary",), vmem_limit_bytes=VMEM_LIMIT),
        name="in_proj",
    )(x2d, norm1_w, w_in_bf, qkw, gmat)


def _stack_subheads(q):
    lane = lax.broadcasted_iota(jnp.int32, q.shape, 1)
    zero = jnp.zeros_like(q)
    return jnp.concatenate([jnp.where(lane < SB_DIM, q, zero),
                            jnp.where(lane >= SB_DIM, q, zero)], axis=0)


def _diff_attn_kernel(bsc_ref, lam_ref, q_ref, k_ref, v_ref, bdiag_ref, bsub_ref, w_ref, o_ref,
                      vt_sc, m_sc, acc_sc, kmax_sc):
    hp = pl.program_id(1)
    qi = pl.program_id(2)
    n_maps = 2 * HEADS_PER_STEP
    lane = lax.broadcasted_iota(jnp.int32, (TQD, LANES), 1)
    sub_lanes = (lane < DIFF_QK_DIM, lane >= DIFF_QK_DIM)
    ones8 = jnp.ones((8, LANES), jnp.bfloat16)

    def lane_sums(x):
        return lax.dot_general(ones8, x.astype(jnp.bfloat16), NT_DIMS,
                               preferred_element_type=jnp.float32)[0:1, :]

    @pl.when(qi == 0)
    def _per_sequence_setup():
        ones = jnp.ones((ONES_ROWS, TKD), vt_sc.dtype)
        for hh in range(HEADS_PER_STEP):
            kmax = [None, None]
            for t in range(vt_sc.shape[1]):
                blk = v_ref[0, t * TKD:(t + 1) * TKD, hh * LANES:(hh + 1) * LANES]
                vt_sc[hh, t, :DIFF_V_DIM, :] = blk.astype(jnp.float32).T.astype(vt_sc.dtype)
                vt_sc[hh, t, DIFF_V_DIM:, :] = ones
                kf = k_ref[0, t * TKD:(t + 1) * TKD, hh * LANES:(hh + 1) * LANES].astype(jnp.float32)
                for c in range(2):
                    n2 = jnp.max(lane_sums(jnp.where(sub_lanes[c], kf * kf, 0.0)))
                    kmax[c] = n2 if kmax[c] is None else jnp.maximum(kmax[c], n2)
            for c in range(2):
                kmax_sc[2 * hh + c] = kmax[c]

    zero = jnp.zeros((TQD, LANES), q_ref.dtype)
    q_sub = []
    for hh in range(HEADS_PER_STEP):
        q = q_ref[0, :, hh * LANES:(hh + 1) * LANES]
        q_sub.append(jnp.where(sub_lanes[0], q, zero))
        q_sub.append(jnp.where(sub_lanes[1], q, zero))

    q0 = pl.multiple_of(qi * TQD, TQD)
    gap = None
    for idx in range(n_maps):
        hh = idx // 2
        head = HEADS_PER_STEP * hp + hh
        qf = q_sub[idx].astype(jnp.float32)
        k_own = k_ref[0, pl.ds(q0, TQD), hh * LANES:(hh + 1) * LANES].astype(jnp.float32)
        own = lane_sums(qf * k_own) + bsc_ref[1, head]
        bound = (jnp.sqrt(lane_sums(qf * qf) * kmax_sc[idx]) * BOUND_SLACK
                 + (bsc_ref[2, head] + BOUND_MARGIN))
        m_sc[idx] = jnp.broadcast_to(bound, (8, TQD))
        g = jnp.max(bound - own)
        gap = g if gap is None else jnp.maximum(gap, g)
    fixed_shift_ok = gap <= SAFE_SHIFT_GAP

    def run_tiles(tiles, fixed):
        def scores(kt, idx):
            ks = pl.multiple_of(kt * TKD, TKD)
            k = k_ref[0, pl.ds(ks, TKD), (idx // 2) * LANES:(idx // 2 + 1) * LANES]
            return lax.dot_general(k, q_sub[idx], NT_DIMS, preferred_element_type=jnp.float32)

        def weights(s, idx, bias_ref, cfar, first):
            hh = idx // 2
            if fixed:
                shift = m_sc[idx][0:1, :]
                if bias_ref is None:
                    shift = shift - cfar[hh]
                else:
                    s = s + bias_ref[0, hh]
                return jnp.exp2(s - shift).astype(jnp.bfloat16), None
            if first:
                s = s + bias_ref[0, hh]
                m_new = jnp.max(s, axis=0, keepdims=True)
                m_sc[idx] = jnp.broadcast_to(m_new, (8, TQD))
                return jnp.exp2(s - m_new).astype(jnp.bfloat16), None
            m_old = m_sc[idx][0:1, :]
            if bias_ref is None:
                m_new = jnp.maximum(m_old, jnp.max(s, axis=0, keepdims=True) + cfar[hh])
                shift = m_new - cfar[hh]
            else:
                s = s + bias_ref[0, hh]
                m_new = jnp.maximum(m_old, jnp.max(s, axis=0, keepdims=True))
                shift = m_new
            m_sc[idx] = jnp.broadcast_to(m_new, (8, TQD))
            return jnp.exp2(s - shift).astype(jnp.bfloat16), jnp.exp2(m_old - m_new)

        def accumulate(kt, idx, p, alpha, first):
            pv = jnp.dot(vt_sc[idx // 2, kt], p, preferred_element_type=jnp.float32)
            if first:
                acc_sc[idx] = pv
            elif alpha is None:
                acc_sc[idx] = acc_sc[idx] + pv
            else:
                acc_sc[idx] = alpha * acc_sc[idx] + pv

        work = [(kt, idx, bias_ref, cfar, first)
                for (kt, bias_ref, cfar, first) in tiles for idx in range(n_maps)]
        s_next = [scores(w[0], w[1]) for w in work[:SCORE_LOOKAHEAD]]
        pending = None
        for n, (kt, idx, bias_ref, cfar, first) in enumerate(work):
            p, alpha = weights(s_next.pop(0), idx, bias_ref, cfar, first)
            if n + SCORE_LOOKAHEAD < len(work):
                nxt = work[n + SCORE_LOOKAHEAD]
                s_next.append(scores(nxt[0], nxt[1]))
            if pending is not None:
                accumulate(*pending)
            pending = (kt, idx, p, alpha, first)
        accumulate(*pending)

    cfar = [bsc_ref[0, HEADS_PER_STEP * hp + hh] for hh in range(HEADS_PER_STEP)]
    n_far = jnp.maximum(qi - 1, 0)

    def all_tiles(fixed):
        @pl.when(qi == 0)
        def _first_query_tile():
            run_tiles([(qi, bdiag_ref, None, True)], fixed)

        @pl.when(qi >= 1)
        def _near_tiles():
            run_tiles([(qi, bdiag_ref, None, True), (qi - 1, bsub_ref, None, False)], fixed)

        def far_pair(i, carry):
            run_tiles([(2 * i, None, cfar, False), (2 * i + 1, None, cfar, False)], fixed)
            return carry

        lax.fori_loop(0, n_far // 2, far_pair, 0)

        @pl.when(n_far % 2 == 1)
        def _last_far_tile():
            run_tiles([(n_far - 1, None, cfar, False)], fixed)

    @pl.when(fixed_shift_ok)
    def _fixed_shift_softmax():
        all_tiles(True)

    @pl.when(jnp.logical_not(fixed_shift_ok))
    def _online_softmax():
        all_tiles(False)

    for hh in range(HEADS_PER_STEP):
        acc0 = acc_sc[2 * hh]
        acc1 = acc_sc[2 * hh + 1]
        o_t = (acc0[:DIFF_V_DIM] / acc0[DIFF_V_DIM:DIFF_V_DIM + 1]
               - lam_ref[0] * (acc1[:DIFF_V_DIM] / acc1[DIFF_V_DIM:DIFF_V_DIM + 1]))
        o = o_t.T
        ms = jnp.mean(o * o, axis=-1, keepdims=True)
        y = ((o * lax.rsqrt(ms + EPS)) * w_ref[...]) * (1.0 - LAMBDA_INIT)
        o_ref[0, :, hh * LANES:(hh + 1) * LANES] = y.astype(o_ref.dtype)


def _diff_attn(proj, bias_tiles, bias_scalars, lam, w_out_norm):
    b, s, _ = proj.shape
    nq = s // TQD
    hps = HEADS_PER_STEP
    wide = hps * LANES
    smem = pl.BlockSpec(memory_space=pltpu.SMEM)
    return pl.pallas_call(
        _diff_attn_kernel,
        grid=(b, DIFF_HEADS // hps, nq),
        in_specs=[
            smem, smem,
            pl.BlockSpec((1, TQD, wide), lambda bi, hp, qi: (bi, qi, QD_BLK // hps + hp)),
            pl.BlockSpec((1, s, wide), lambda bi, hp, qi: (bi, 0, KD_BLK // hps + hp)),
            pl.BlockSpec((1, s, wide), lambda bi, hp, qi: (bi, 0, VD_BLK // hps + hp)),
            pl.BlockSpec((1, hps, TKD, TQD), lambda bi, hp, qi: (0, hp, 0, 0)),
            pl.BlockSpec((1, hps, TKD, TQD), lambda bi, hp, qi: (1, hp, 0, 0)),
            pl.BlockSpec((1, DIFF_V_DIM), lambda bi, hp, qi: (0, 0)),
        ],
        out_specs=pl.BlockSpec((1, TQD, wide), lambda bi, hp, qi: (bi, qi, hp)),
        out_shape=jax.ShapeDtypeStruct((b, s, DIFF_HEADS * DIFF_V_DIM), jnp.bfloat16),
        scratch_shapes=[
            pltpu.VMEM((hps, s // TKD, DIFF_V_DIM + ONES_ROWS, TKD), jnp.bfloat16),
            pltpu.VMEM((2 * hps, 8, TQD), jnp.float32),
            pltpu.VMEM((2 * hps, DIFF_V_DIM + ONES_ROWS, TQD), jnp.float32),
            pltpu.SMEM((2 * hps,), jnp.float32),
        ],
        compiler_params=pltpu.CompilerParams(
            dimension_semantics=("arbitrary", "arbitrary", "arbitrary"),
            vmem_limit_bytes=VMEM_LIMIT),
        name="diff_attn",
    )(bias_scalars, lam, proj, proj, proj, bias_tiles, bias_tiles, w_out_norm)


def _softplus2(y):
    return jnp.where(y > SOFTPLUS_LINEAR_FROM, y, jnp.log(1.0 + jnp.exp2(y)) * LOG2E)


def _sb_attn_kernel(q_ref, k_ref, v_ref, tri_ref, mask_ref, w_ref, o_ref, vt_sc, r_sc, acc_sc,
                    rmin_sc):
    qi = pl.program_id(1)
    tri_t = tri_ref[...]

    @pl.when(qi == 0)
    def _build_vt():
        for j in range(SB_PAIRS):
            for t in range(vt_sc.shape[1]):
                blk = v_ref[0, t * TK:(t + 1) * TK, j * LANES:(j + 1) * LANES]
                vt_sc[j, t] = blk.astype(jnp.float32).T.astype(vt_sc.dtype)

    qq = [_stack_subheads(q_ref[0, :, j * LANES:(j + 1) * LANES]) for j in range(SB_PAIRS)]

    def run_tiles(tiles):
        work = [(kt, diagonal, j) for (kt, diagonal) in tiles for j in range(SB_PAIRS)]
        z = []
        for kt, _, j in work:
            ks = pl.multiple_of(kt * TK, TK)
            z.append(lax.dot_general(k_ref[0, pl.ds(ks, TK), j * LANES:(j + 1) * LANES], qq[j],
                                     NT_DIMS, preferred_element_type=jnp.float32))
        csum = []
        for n, (kt, diagonal, j) in enumerate(work):
            sp = _softplus2(z[n])
            c = sp * mask_ref[...] if diagonal else sp
            hi, lo = _split_hi_lo(c)
            csum.append(jnp.dot(tri_t, hi, preferred_element_type=jnp.float32)
                        + jnp.dot(tri_t, lo, preferred_element_type=jnp.float32))
        r_min = None
        for n, (kt, diagonal, j) in enumerate(work):
            if diagonal:
                a = jnp.exp2(z[n] - csum[n]) * mask_ref[...]
                r_new = csum[n][0:1, :]
            else:
                r = r_sc[j][0:1, :]
                a = jnp.exp2(z[n] - csum[n] - r)
                r_new = r + csum[n][0:1, :]
            av = jnp.dot(vt_sc[j, kt], a.astype(jnp.bfloat16), preferred_element_type=jnp.float32)
            acc_sc[j] = av if diagonal else acc_sc[j] + av
            r_sc[j] = jnp.broadcast_to(r_new, (8, 2 * TQ))
            if n >= len(work) - SB_PAIRS:
                r_min = jnp.min(r_new) if r_min is None else jnp.minimum(r_min, jnp.min(r_new))
        return r_min

    @pl.when(qi == 0)
    def _first_query_tile():
        run_tiles([(qi, True)])
        rmin_sc[0] = jnp.float32(0.0)

    @pl.when(qi >= 1)
    def _diagonal_and_previous():
        rmin_sc[0] = run_tiles([(qi, True), (qi - 1, False)])

    r_min0 = rmin_sc[0]

    def more(carry):
        i, r_min = carry
        return jnp.logical_and(i < qi, r_min < SB_STOP_COST)

    def step(carry):
        i, _ = carry
        return i + 1, run_tiles([(qi - 1 - i, False)])

    lax.while_loop(more, step, (jnp.int32(1), r_min0))

    lane = lax.broadcasted_iota(jnp.int32, (TQ, LANES), 1)
    first = lane < SB_DIM
    for j in range(SB_PAIRS):
        acc = acc_sc[j].T
        o = jnp.where(first, acc[:TQ], acc[TQ:])
        o2 = o * o
        ss_a = jnp.sum(jnp.where(first, o2, 0.0), axis=-1, keepdims=True)
        ss_b = jnp.sum(jnp.where(first, 0.0, o2), axis=-1, keepdims=True)
        ms = jnp.where(first, ss_a, ss_b) * (1.0 / SB_DIM)
        o_ref[0, :, j * LANES:(j + 1) * LANES] = (
            (o * lax.rsqrt(ms + EPS)) * w_ref[...]).astype(o_ref.dtype)


def _sb_attn(proj, tri_t, mask_t, w_out_norm):
    b, s, _ = proj.shape
    nq = s // TQ
    wide = SB_PAIRS * LANES
    return pl.pallas_call(
        _sb_attn_kernel,
        grid=(b, nq),
        in_specs=[
            pl.BlockSpec((1, TQ, wide), lambda bi, qi: (bi, qi, QS_BLK // SB_PAIRS)),
            pl.BlockSpec((1, s, wide), lambda bi, qi: (bi, 0, KS_BLK // SB_PAIRS)),
            pl.BlockSpec((1, s, wide), lambda bi, qi: (bi, 0, VS_BLK // SB_PAIRS)),
            pl.BlockSpec((TK, TK), lambda bi, qi: (0, 0)),
            pl.BlockSpec((TK, 2 * TQ), lambda bi, qi: (0, 0)),
            pl.BlockSpec((1, LANES), lambda bi, qi: (0, 0)),
        ],
        out_specs=pl.BlockSpec((1, TQ, wide), lambda bi, qi: (bi, qi, 0)),
        out_shape=jax.ShapeDtypeStruct((b, s, SB_HEADS * SB_DIM), jnp.bfloat16),
        scratch_shapes=[
            pltpu.VMEM((SB_PAIRS, s // TK, LANES, TK), jnp.bfloat16),
            pltpu.VMEM((SB_PAIRS, 8, 2 * TQ), jnp.float32),
            pltpu.VMEM((SB_PAIRS, LANES, 2 * TQ), jnp.float32),
            pltpu.SMEM((1,), jnp.float32),
        ],
        compiler_params=pltpu.CompilerParams(
            dimension_semantics=("arbitrary", "arbitrary"),
            vmem_limit_bytes=VMEM_LIMIT),
        name="sb_attn",
    )(proj, proj, proj, tri_t, mask_t, w_out_norm)


def _out_ffn_kernel(x_ref, yd_ref, ys_ref, woa_ref, wob_ref, w2_ref, wg_ref, wu_ref, wd_ref,
                    o_ref):
    h1 = (x_ref[...]
          + jnp.dot(yd_ref[...], woa_ref[...], preferred_element_type=jnp.float32)
          + jnp.dot(ys_ref[...], wob_ref[...], preferred_element_type=jnp.float32))
    ms = jnp.mean(h1 * h1, axis=-1, keepdims=True)
    u2 = ((h1 * lax.rsqrt(ms + EPS)) * w2_ref[...]).astype(jnp.bfloat16)
    ffn = None
    for c in range(D_FF // FF_CHUNK):
        sl = slice(c * FF_CHUNK, (c + 1) * FF_CHUNK)
        gate = jnp.dot(u2, wg_ref[:, sl], preferred_element_type=jnp.float32)
        up = jnp.dot(u2, wu_ref[:, sl], preferred_element_type=jnp.float32)
        act = (gate * (1.0 / (1.0 + jnp.exp(-gate))) * up).astype(jnp.bfloat16)
        part = jnp.dot(act, wd_ref[sl, :], preferred_element_type=jnp.float32)
        ffn = part if ffn is None else ffn + part
    o_ref[...] = h1 + ffn


def _out_ffn(x2d, yd, ys, wo, norm2_w, wg, wu, wd):
    n = x2d.shape[0]
    const = lambda i: (0, 0)
    half = DIFF_HEADS * DIFF_V_DIM
    resident = partial(pl.BlockSpec, index_map=const, pipeline_mode=pl.Buffered(1))
    return pl.pallas_call(
        _out_ffn_kernel,
        grid=(n // TM_PROJ,),
        in_specs=[
            pl.BlockSpec((TM_PROJ, D_MODEL), lambda i: (i, 0)),
            pl.BlockSpec((TM_PROJ, half), lambda i: (i, 0)),
            pl.BlockSpec((TM_PROJ, half), lambda i: (i, 0)),
            resident((half, D_MODEL)),
            pl.BlockSpec((half, D_MODEL), lambda i: (1, 0), pipeline_mode=pl.Buffered(1)),
            pl.BlockSpec((1, D_MODEL), const),
            resident((D_MODEL, D_FF)),
            resident((D_MODEL, D_FF)),
            resident((D_FF, D_MODEL)),
        ],
        out_specs=pl.BlockSpec((TM_PROJ, D_MODEL), lambda i: (i, 0)),
        out_shape=jax.ShapeDtypeStruct((n, D_MODEL), jnp.float32),
        compiler_params=pltpu.CompilerParams(
            dimension_semantics=("arbitrary",), vmem_limit_bytes=VMEM_LIMIT),
        name="out_ffn",
    )(x2d, yd, ys, wo, wo, norm2_w, wg, wu, wd)


def _bias_tiles_kernel(thr_ref, vals_ref, o_ref):
    t = pl.program_id(0)
    h = pl.program_id(1)
    kpos = lax.broadcasted_iota(jnp.int32, (TKD, TQD), 0) - t * TKD
    qpos = lax.broadcasted_iota(jnp.int32, (TKD, TQD), 1)
    rel = kpos - qpos
    dist = jnp.abs(rel)
    half = NUM_BUCKETS // 2
    behind = jnp.full((TKD, TQD), vals_ref[h, 0], jnp.float32)
    ahead = jnp.full((TKD, TQD), vals_ref[h, half], jnp.float32)
    for i in range(1, half):
        reached = dist >= thr_ref[i]
        behind = jnp.where(reached, vals_ref[h, i], behind)
        ahead = jnp.where(reached, vals_ref[h, half + i], ahead)
    bias = jnp.where(rel > 0, ahead, behind)
    shift = CHUNK.bit_length() - 1
    allowed = (kpos >> shift) <= (qpos >> shift)
    o_ref[0, 0] = jnp.where(allowed, bias, MASK_VALUE)


def _diff_bias_tiles(rel_bias):
    half = NUM_BUCKETS // 2
    idx = _t5_bucket(-jnp.arange(TQD + TKD, dtype=jnp.int32))
    thr = jnp.sum((idx[None, :] < jnp.arange(half, dtype=jnp.int32)[:, None]).astype(jnp.int32), axis=1)
    vals = rel_bias.astype(jnp.float32).T * LOG2E
    smem = pl.BlockSpec(memory_space=pltpu.SMEM)
    return pl.pallas_call(
        _bias_tiles_kernel,
        grid=(2, DIFF_HEADS),
        in_specs=[smem, smem],
        out_specs=pl.BlockSpec((1, 1, TKD, TQD), lambda t, h: (t, h, 0, 0)),
        out_shape=jax.ShapeDtypeStruct((2, DIFF_HEADS, TKD, TQD), jnp.float32),
        compiler_params=pltpu.CompilerParams(dimension_semantics=("arbitrary", "arbitrary")),
        name="bias_tiles",
    )(thr.astype(jnp.int32), vals)


def kernel(x, norm1_w, w_in, q_norm_w, k_norm_w, lambda_q1, lambda_k1, lambda_q2, lambda_k2,
           diff_out_norm_w, sb_out_norm_w, w_out, norm2_w, w_gate, w_up, w_down, rel_bias):
    b, s, d = x.shape
    assert (b, s, d) == (8, 4096, D_MODEL) and w_in.shape[0] == 1
    f32, bf16 = jnp.float32, jnp.bfloat16
    x2d = x.reshape(b * s, d)

    qkw = jnp.concatenate([jnp.tile(q_norm_w[0].astype(f32), 2 * DIFF_HEADS)
                           * (DIFF_QK_DIM ** -0.5 * LOG2E),
                           jnp.tile(k_norm_w[0].astype(f32), 2 * DIFF_HEADS)])[None, :]
    grp = jnp.arange(GROUP_SLAB) // DIFF_QK_DIM
    gmat = jnp.where(grp[:, None] == grp[None, :], 1.0 / DIFF_QK_DIM, 0.0).astype(bf16)
    lam = (jnp.exp(jnp.sum(lambda_q1[0].astype(f32) * lambda_k1[0].astype(f32)))
           - jnp.exp(jnp.sum(lambda_q2[0].astype(f32) * lambda_k2[0].astype(f32)))
           + LAMBDA_INIT).reshape(1)
    bias_tiles = _diff_bias_tiles(rel_bias)
    vals = rel_bias.astype(f32) * LOG2E
    bias_scalars = jnp.stack([vals[_t5_bucket(jnp.int32(-(TKD + 1)))],
                              vals[_t5_bucket(jnp.int32(0))],
                              jnp.max(vals, axis=0)])
    jj = jnp.arange(TK)
    tri_t = (jj[None, :] >= jj[:, None]).astype(bf16)
    causal_t = (jj[:, None] < jnp.arange(TQ)[None, :]).astype(f32)
    sb_mask_t = jnp.concatenate([causal_t, causal_t], axis=1)

    proj = _in_proj(x2d, norm1_w[0].astype(f32)[None, :], w_in[0].astype(bf16), qkw, gmat)
    proj = proj.reshape(b, s, IN_COLS)
    y_diff = _diff_attn(proj, bias_tiles, bias_scalars, lam,
                        diff_out_norm_w[0].astype(f32)[None, :])
    y_sb = _sb_attn(proj, tri_t, sb_mask_t, jnp.tile(sb_out_norm_w[0].astype(f32), 2)[None, :])

    half = DIFF_HEADS * DIFF_V_DIM
    out = _out_ffn(x2d, y_diff.reshape(b * s, half), y_sb.reshape(b * s, half),
                   w_out[0].astype(bf16), norm2_w[0].astype(f32)[None, :],
                   w_gate[0].astype(bf16), w_up[0].astype(bf16), w_down[0].astype(bf16))
    return out.reshape(b, s, d)
```

```python
import math
from functools import partial

import jax
import jax.numpy as jnp
from jax import lax
from jax.experimental import pallas as pl
from jax.experimental.pallas import tpu as pltpu

D_MODEL = 1024
CHUNK = 64
DIFF_QK_DIM = 64
DIFF_V_DIM = 128
DIFF_HEADS = 4
SB_DIM = 64
SB_HEADS = 8
NUM_BUCKETS = 32
MAX_DISTANCE = 128
D_FF = 2816
EPS = 1e-6
IN_COLS = 3072
LAMBDA_INIT = 0.8 - 0.6 * math.exp(-0.3 * 0)
LANES = 128

QD_BLK, KD_BLK, VD_BLK = 0, 4, 8
QS_BLK, KS_BLK, VS_BLK = 12, 16, 20

TM_PROJ = 512
COL_CHUNK = 512
GROUP_SLAB = 256
TQD = 512
TKD = 512
HEADS_PER_STEP = 4
ONES_ROWS = 16
SCORE_LOOKAHEAD = 2
LOG2E = math.log2(math.e)
TQ = 256
TK = 256
SB_PAIRS = SB_HEADS // 2
SB_LOOKAHEAD = 2
FF_CHUNK = 256
SAFE_SHIFT_GAP = 90.0
BOUND_SLACK = 1.01
BOUND_MARGIN = 0.5
MASK_VALUE = -1e30
SB_STOP_COST = 150.1
SOFTPLUS_LINEAR_FROM = 60.0
VMEM_LIMIT = 56 * 1024 * 1024

NT_DIMS = (((1,), (1,)), ((), ()))


def _t5_bucket(rel):
    nb = NUM_BUCKETS // 2
    max_exact = nb // 2
    ret = (rel > 0).astype(jnp.int32) * nb
    n = jnp.abs(rel)
    nf = jnp.maximum(n, 1).astype(jnp.float32)
    large = max_exact + (jnp.log(nf / max_exact) / jnp.log(jnp.float32(MAX_DISTANCE / max_exact))
                         * (nb - max_exact)).astype(jnp.int32)
    large = jnp.minimum(large, nb - 1)
    return ret + jnp.where(n < max_exact, n, large)


def _split_hi_lo(x):
    hi = x.astype(jnp.bfloat16)
    lo = (x - hi.astype(jnp.float32)).astype(jnp.bfloat16)
    return hi, lo


def _in_proj_kernel(x_ref, w1_ref, win_ref, qkw_ref, g_ref, o_ref):
    x = x_ref[...]
    ms = jnp.mean(x * x, axis=-1, keepdims=True)
    u = ((x * lax.rsqrt(ms + EPS)) * w1_ref[...]).astype(jnp.bfloat16)
    g = g_ref[...]
    for c in range(IN_COLS // COL_CHUNK):
        lo_c, hi_c = c * COL_CHUNK, (c + 1) * COL_CHUNK
        p = jnp.dot(u, win_ref[:, lo_c:hi_c], preferred_element_type=jnp.float32)
        if hi_c <= 2 * DIFF_HEADS * 2 * DIFF_QK_DIM:
            for j in range(COL_CHUNK // GROUP_SLAB):
                sl = slice(j * GROUP_SLAB, (j + 1) * GROUP_SLAB)
                pj = p[:, sl]
                msq = jnp.dot((pj * pj).astype(jnp.bfloat16), g, preferred_element_type=jnp.float32)
                w = qkw_ref[:, lo_c + j * GROUP_SLAB: lo_c + (j + 1) * GROUP_SLAB]
                o_ref[:, lo_c + j * GROUP_SLAB: lo_c + (j + 1) * GROUP_SLAB] = (
                    (pj * lax.rsqrt(msq + EPS)) * w).astype(jnp.bfloat16)
        else:
            if lo_c == QS_BLK * LANES:
                p = p * (SB_DIM ** -0.5 * LOG2E)
            o_ref[:, lo_c:hi_c] = p.astype(jnp.bfloat16)


def _in_proj(x2d, norm1_w, w_in_bf, qkw, gmat):
    n = x2d.shape[0]
    const = lambda i: (0, 0)
    return pl.pallas_call(
        _in_proj_kernel,
        grid=(n // TM_PROJ,),
        in_specs=[
            pl.BlockSpec((TM_PROJ, D_MODEL), lambda i: (i, 0)),
            pl.BlockSpec((1, D_MODEL), const),
            pl.BlockSpec((D_MODEL, IN_COLS), const, pipeline_mode=pl.Buffered(1)),
            pl.BlockSpec((1, 2 * DIFF_HEADS * 2 * DIFF_QK_DIM), const),
            pl.BlockSpec((GROUP_SLAB, GROUP_SLAB), const),
        ],
        out_specs=pl.BlockSpec((TM_PROJ, IN_COLS), lambda i: (i, 0)),
        out_shape=jax.ShapeDtypeStruct((n, IN_COLS), jnp.bfloat16),
        compiler_params=pltpu.CompilerParams(
            dimension_semantics=("arbitrary",), vmem_limit_bytes=VMEM_LIMIT),
        name="in_proj",
    )(x2d, norm1_w, w_in_bf, qkw, gmat)


def _stack_subheads(q):
    lane = lax.broadcasted_iota(jnp.int32, q.shape, 1)
    zero = jnp.zeros_like(q)
    return jnp.concatenate([jnp.where(lane < SB_DIM, q, zero),
                            jnp.where(lane >= SB_DIM, q, zero)], axis=0)


def _diff_attn_kernel(bsc_ref, lam_ref, q_ref, k_ref, v_ref, bdiag_ref, bsub_ref, w_ref, o_ref,
                      vt_sc, m_sc, acc_sc, kmax_sc):
    hp = pl.program_id(1)
    qi = pl.program_id(2)
    n_maps = 2 * HEADS_PER_STEP
    lane = lax.broadcasted_iota(jnp.int32, (TQD, LANES), 1)
    sub_lanes = (lane < DIFF_QK_DIM, lane >= DIFF_QK_DIM)
    ones8 = jnp.ones((8, LANES), jnp.bfloat16)

    def lane_sums(x):
        return lax.dot_general(ones8, x.astype(jnp.bfloat16), NT_DIMS,
                               preferred_element_type=jnp.float32)[0:1, :]

    @pl.when(qi == 0)
    def _per_sequence_setup():
        ones = jnp.ones((ONES_ROWS, TKD), vt_sc.dtype)
        for hh in range(HEADS_PER_STEP):
            kmax = [None, None]
            for t in range(vt_sc.shape[1]):
                blk = v_ref[0, t * TKD:(t + 1) * TKD, hh * LANES:(hh + 1) * LANES]
                vt_sc[hh, t, :DIFF_V_DIM, :] = blk.astype(jnp.float32).T.astype(vt_sc.dtype)
                vt_sc[hh, t, DIFF_V_DIM:, :] = ones
                kf = k_ref[0, t * TKD:(t + 1) * TKD, hh * LANES:(hh + 1) * LANES].astype(jnp.float32)
                for c in range(2):
                    n2 = jnp.max(lane_sums(jnp.where(sub_lanes[c], kf * kf, 0.0)))
                    kmax[c] = n2 if kmax[c] is None else jnp.maximum(kmax[c], n2)
            for c in range(2):
                kmax_sc[2 * hh + c] = kmax[c]

    zero = jnp.zeros((TQD, LANES), q_ref.dtype)
    q_sub = []
    for hh in range(HEADS_PER_STEP):
        q = q_ref[0, :, hh * LANES:(hh + 1) * LANES]
        q_sub.append(jnp.where(sub_lanes[0], q, zero))
        q_sub.append(jnp.where(sub_lanes[1], q, zero))

    q0 = pl.multiple_of(qi * TQD, TQD)
    gap = None
    for idx in range(n_maps):
        hh = idx // 2
        head = HEADS_PER_STEP * hp + hh
        qf = q_sub[idx].astype(jnp.float32)
        k_own = k_ref[0, pl.ds(q0, TQD), hh * LANES:(hh + 1) * LANES].astype(jnp.float32)
        own = lane_sums(qf * k_own) + bsc_ref[1, head]
        bound = (jnp.sqrt(lane_sums(qf * qf) * kmax_sc[idx]) * BOUND_SLACK
                 + (bsc_ref[2, head] + BOUND_MARGIN))
        m_sc[idx] = jnp.broadcast_to(bound, (8, TQD))
        g = jnp.max(bound - own)
        gap = g if gap is None else jnp.maximum(gap, g)
    fixed_shift_ok = gap <= SAFE_SHIFT_GAP

    def run_tiles(tiles, fixed):
        def scores(kt, idx):
            ks = pl.multiple_of(kt * TKD, TKD)
            k = k_ref[0, pl.ds(ks, TKD), (idx // 2) * LANES:(idx // 2 + 1) * LANES]
            return lax.dot_general(k, q_sub[idx], NT_DIMS, preferred_element_type=jnp.float32)

        def weights(s, idx, bias_ref, cfar, first):
            hh = idx // 2
            if fixed:
                shift = m_sc[idx][0:1, :]
                if bias_ref is None:
                    shift = shift - cfar[hh]
                else:
                    s = s + bias_ref[0, hh]
                return jnp.exp2(s - shift).astype(jnp.bfloat16), None
            if first:
                s = s + bias_ref[0, hh]
                m_new = jnp.max(s, axis=0, keepdims=True)
                m_sc[idx] = jnp.broadcast_to(m_new, (8, TQD))
                return jnp.exp2(s - m_new).astype(jnp.bfloat16), None
            m_old = m_sc[idx][0:1, :]
            if bias_ref is None:
                m_new = jnp.maximum(m_old, jnp.max(s, axis=0, keepdims=True) + cfar[hh])
                shift = m_new - cfar[hh]
            else:
                s = s + bias_ref[0, hh]
                m_new = jnp.maximum(m_old, jnp.max(s, axis=0, keepdims=True))
                shift = m_new
            m_sc[idx] = jnp.broadcast_to(m_new, (8, TQD))
            return jnp.exp2(s - shift).astype(jnp.bfloat16), jnp.exp2(m_old - m_new)

        def accumulate(kt, idx, p, alpha, first):
            pv = jnp.dot(vt_sc[idx // 2, kt], p, preferred_element_type=jnp.float32)
            if first:
                acc_sc[idx] = pv
            elif alpha is None:
                acc_sc[idx] = acc_sc[idx] + pv
            else:
                acc_sc[idx] = alpha * acc_sc[idx] + pv

        work = [(kt, idx, bias_ref, cfar, first)
                for (kt, bias_ref, cfar, first) in tiles for idx in range(n_maps)]
        s_next = [scores(w[0], w[1]) for w in work[:SCORE_LOOKAHEAD]]
        pending = None
        for n, (kt, idx, bias_ref, cfar, first) in enumerate(work):
            p, alpha = weights(s_next.pop(0), idx, bias_ref, cfar, first)
            if n + SCORE_LOOKAHEAD < len(work):
                nxt = work[n + SCORE_LOOKAHEAD]
                s_next.append(scores(nxt[0], nxt[1]))
            if pending is not None:
                accumulate(*pending)
            pending = (kt, idx, p, alpha, first)
        accumulate(*pending)

    cfar = [bsc_ref[0, HEADS_PER_STEP * hp + hh] for hh in range(HEADS_PER_STEP)]
    n_far = jnp.maximum(qi - 1, 0)

    def all_tiles(fixed):
        @pl.when(qi == 0)
        def _first_query_tile():
            run_tiles([(qi, bdiag_ref, None, True)], fixed)

        @pl.when(qi >= 1)
        def _near_tiles():
            run_tiles([(qi, bdiag_ref, None, True), (qi - 1, bsub_ref, None, False)], fixed)

        def far_pair(i, carry):
            run_tiles([(2 * i, None, cfar, False), (2 * i + 1, None, cfar, False)], fixed)
            return carry

        lax.fori_loop(0, n_far // 2, far_pair, 0)

        @pl.when(n_far % 2 == 1)
        def _last_far_tile():
            run_tiles([(n_far - 1, None, cfar, False)], fixed)

    @pl.when(fixed_shift_ok)
    def _fixed_shift_softmax():
        all_tiles(True)

    @pl.when(jnp.logical_not(fixed_shift_ok))
    def _online_softmax():
        all_tiles(False)

    for hh in range(HEADS_PER_STEP):
        acc0 = acc_sc[2 * hh]
        acc1 = acc_sc[2 * hh + 1]
        o_t = (acc0[:DIFF_V_DIM] / acc0[DIFF_V_DIM:DIFF_V_DIM + 1]
               - lam_ref[0] * (acc1[:DIFF_V_DIM] / acc1[DIFF_V_DIM:DIFF_V_DIM + 1]))
        o = o_t.T
        ms = jnp.mean(o * o, axis=-1, keepdims=True)
        y = ((o * lax.rsqrt(ms + EPS)) * w_ref[...]) * (1.0 - LAMBDA_INIT)
        o_ref[0, :, hh * LANES:(hh + 1) * LANES] = y.astype(o_ref.dtype)


def _diff_attn(proj, bias_tiles, bias_scalars, lam, w_out_norm):
    b, s, _ = proj.shape
    nq = s // TQD
    hps = HEADS_PER_STEP
    wide = hps * LANES
    smem = pl.BlockSpec(memory_space=pltpu.SMEM)
    return pl.pallas_call(
        _diff_attn_kernel,
        grid=(b, DIFF_HEADS // hps, nq),
        in_specs=[
            smem, smem,
            pl.BlockSpec((1, TQD, wide), lambda bi, hp, qi: (bi, qi, QD_BLK // hps + hp)),
            pl.BlockSpec((1, s, wide), lambda bi, hp, qi: (bi, 0, KD_BLK // hps + hp)),
            pl.BlockSpec((1, s, wide), lambda bi, hp, qi: (bi, 0, VD_BLK // hps + hp)),
            pl.BlockSpec((1, hps, TKD, TQD), lambda bi, hp, qi: (0, hp, 0, 0)),
            pl.BlockSpec((1, hps, TKD, TQD), lambda bi, hp, qi: (1, hp, 0, 0)),
            pl.BlockSpec((1, DIFF_V_DIM), lambda bi, hp, qi: (0, 0)),
        ],
        out_specs=pl.BlockSpec((1, TQD, wide), lambda bi, hp, qi: (bi, qi, hp)),
        out_shape=jax.ShapeDtypeStruct((b, s, DIFF_HEADS * DIFF_V_DIM), jnp.bfloat16),
        scratch_shapes=[
            pltpu.VMEM((hps, s // TKD, DIFF_V_DIM + ONES_ROWS, TKD), jnp.bfloat16),
            pltpu.VMEM((2 * hps, 8, TQD), jnp.float32),
            pltpu.VMEM((2 * hps, DIFF_V_DIM + ONES_ROWS, TQD), jnp.float32),
            pltpu.SMEM((2 * hps,), jnp.float32),
        ],
        compiler_params=pltpu.CompilerParams(
            dimension_semantics=("arbitrary", "arbitrary", "arbitrary"),
            vmem_limit_bytes=VMEM_LIMIT),
        name="diff_attn",
    )(bias_scalars, lam, proj, proj, proj, bias_tiles, bias_tiles, w_out_norm)


def _softplus2(y):
    return jnp.where(y > SOFTPLUS_LINEAR_FROM, y, jnp.log(1.0 + jnp.exp2(y)) * LOG2E)


def _sb_attn_kernel(q_ref, k_ref, v_ref, tri_ref, mask_ref, w_ref, o_ref, vt_sc, r_sc, acc_sc,
                    rmin_sc):
    qi = pl.program_id(1)
    tri_t = tri_ref[...]

    @pl.when(qi == 0)
    def _build_vt():
        for j in range(SB_PAIRS):
            for t in range(vt_sc.shape[1]):
                blk = v_ref[0, t * TK:(t + 1) * TK, j * LANES:(j + 1) * LANES]
                vt_sc[j, t] = blk.astype(jnp.float32).T.astype(vt_sc.dtype)

    qq = [_stack_subheads(q_ref[0, :, j * LANES:(j + 1) * LANES]) for j in range(SB_PAIRS)]

    def run_tiles(tiles):
        work = [(kt, diagonal, j) for (kt, diagonal) in tiles for j in range(SB_PAIRS)]

        def logits(n):
            kt, _, j = work[n]
            ks = pl.multiple_of(kt * TK, TK)
            return lax.dot_general(k_ref[0, pl.ds(ks, TK), j * LANES:(j + 1) * LANES], qq[j],
                                   NT_DIMS, preferred_element_type=jnp.float32)

        def cost_sums(n, z):
            sp = _softplus2(z)
            c = sp * mask_ref[...] if work[n][1] else sp
            hi, lo = _split_hi_lo(c)
            return (jnp.dot(tri_t, hi, preferred_element_type=jnp.float32)
                    + jnp.dot(tri_t, lo, preferred_element_type=jnp.float32))

        def attend(n, z, csum):
            kt, diagonal, j = work[n]
            if diagonal:
                a = jnp.exp2(z - csum) * mask_ref[...]
                r_new = csum[0:1, :]
            else:
                r = r_sc[j][0:1, :]
                a = jnp.exp2(z - csum - r)
                r_new = r + csum[0:1, :]
            av = jnp.dot(vt_sc[j, kt], a.astype(jnp.bfloat16), preferred_element_type=jnp.float32)
            acc_sc[j] = av if diagonal else acc_sc[j] + av
            r_sc[j] = jnp.broadcast_to(r_new, (8, 2 * TQ))
            return jnp.min(r_new)

        z_next = [logits(n) for n in range(min(SB_LOOKAHEAD, len(work)))]
        pending = None
        r_mins = []
        for n in range(len(work)):
            z = z_next.pop(0)
            csum = cost_sums(n, z)
            if n + SB_LOOKAHEAD < len(work):
                z_next.append(logits(n + SB_LOOKAHEAD))
            if pending is not None:
                r_mins.append(attend(*pending))
            pending = (n, z, csum)
        r_mins.append(attend(*pending))
        r_min = r_mins[-SB_PAIRS]
        for m in r_mins[-SB_PAIRS + 1:]:
            r_min = jnp.minimum(r_min, m)
        return r_min

    @pl.when(qi == 0)
    def _first_query_tile():
        run_tiles([(qi, True)])
        rmin_sc[0] = jnp.float32(0.0)

    @pl.when(qi >= 1)
    def _diagonal_and_previous():
        rmin_sc[0] = run_tiles([(qi, True), (qi - 1, False)])

    r_min0 = rmin_sc[0]

    def more(carry):
        i, r_min = carry
        return jnp.logical_and(i < qi, r_min < SB_STOP_COST)

    def step(carry):
        i, _ = carry
        return i + 1, run_tiles([(qi - 1 - i, False)])

    lax.while_loop(more, step, (jnp.int32(1), r_min0))

    lane = lax.broadcasted_iota(jnp.int32, (TQ, LANES), 1)
    first = lane < SB_DIM
    for j in range(SB_PAIRS):
        acc = acc_sc[j].T
        o = jnp.where(first, acc[:TQ], acc[TQ:])
        o2 = o * o
        ss_a = jnp.sum(jnp.where(first, o2, 0.0), axis=-1, keepdims=True)
        ss_b = jnp.sum(jnp.where(first, 0.0, o2), axis=-1, keepdims=True)
        ms = jnp.where(first, ss_a, ss_b) * (1.0 / SB_DIM)
        o_ref[0, :, j * LANES:(j + 1) * LANES] = (
            (o * lax.rsqrt(ms + EPS)) * w_ref[...]).astype(o_ref.dtype)


def _sb_attn(proj, tri_t, mask_t, w_out_norm):
    b, s, _ = proj.shape
    nq = s // TQ
    wide = SB_PAIRS * LANES
    return pl.pallas_call(
        _sb_attn_kernel,
        grid=(b, nq),
        in_specs=[
            pl.BlockSpec((1, TQ, wide), lambda bi, qi: (bi, qi, QS_BLK // SB_PAIRS)),
            pl.BlockSpec((1, s, wide), lambda bi, qi: (bi, 0, KS_BLK // SB_PAIRS)),
            pl.BlockSpec((1, s, wide), lambda bi, qi: (bi, 0, VS_BLK // SB_PAIRS)),
            pl.BlockSpec((TK, TK), lambda bi, qi: (0, 0)),
            pl.BlockSpec((TK, 2 * TQ), lambda bi, qi: (0, 0)),
            pl.BlockSpec((1, LANES), lambda bi, qi: (0, 0)),
        ],
        out_specs=pl.BlockSpec((1, TQ, wide), lambda bi, qi: (bi, qi, 0)),
        out_shape=jax.ShapeDtypeStruct((b, s, SB_HEADS * SB_DIM), jnp.bfloat16),
        scratch_shapes=[
            pltpu.VMEM((SB_PAIRS, s // TK, LANES, TK), jnp.bfloat16),
            pltpu.VMEM((SB_PAIRS, 8, 2 * TQ), jnp.float32),
            pltpu.VMEM((SB_PAIRS, LANES, 2 * TQ), jnp.float32),
            pltpu.SMEM((1,), jnp.float32),
        ],
        compiler_params=pltpu.CompilerParams(
            dimension_semantics=("arbitrary", "arbitrary"),
            vmem_limit_bytes=VMEM_LIMIT),
        name="sb_attn",
    )(proj, proj, proj, tri_t, mask_t, w_out_norm)


def _out_ffn_kernel(x_ref, yd_ref, ys_ref, woa_ref, wob_ref, w2_ref, wg_ref, wu_ref, wd_ref,
                    o_ref):
    h1 = (x_ref[...]
          + jnp.dot(yd_ref[...], woa_ref[...], preferred_element_type=jnp.float32)
          + jnp.dot(ys_ref[...], wob_ref[...], preferred_element_type=jnp.float32))
    ms = jnp.mean(h1 * h1, axis=-1, keepdims=True)
    u2 = ((h1 * lax.rsqrt(ms + EPS)) * w2_ref[...]).astype(jnp.bfloat16)
    ffn = None
    for c in range(D_FF // FF_CHUNK):
        sl = slice(c * FF_CHUNK, (c + 1) * FF_CHUNK)
        gate = jnp.dot(u2, wg_ref[:, sl], preferred_element_type=jnp.float32)
        up = jnp.dot(u2, wu_ref[:, sl], preferred_element_type=jnp.float32)
        act = (gate * (1.0 / (1.0 + jnp.exp(-gate))) * up).astype(jnp.bfloat16)
        part = jnp.dot(act, wd_ref[sl, :], preferred_element_type=jnp.float32)
        ffn = part if ffn is None else ffn + part
    o_ref[...] = h1 + ffn


def _out_ffn(x2d, yd, ys, wo, norm2_w, wg, wu, wd):
    n = x2d.shape[0]
    const = lambda i: (0, 0)
    half = DIFF_HEADS * DIFF_V_DIM
    resident = partial(pl.BlockSpec, index_map=const, pipeline_mode=pl.Buffered(1))
    return pl.pallas_call(
        _out_ffn_kernel,
        grid=(n // TM_PROJ,),
        in_specs=[
            pl.BlockSpec((TM_PROJ, D_MODEL), lambda i: (i, 0)),
            pl.BlockSpec((TM_PROJ, half), lambda i: (i, 0)),
            pl.BlockSpec((TM_PROJ, half), lambda i: (i, 0)),
            resident((half, D_MODEL)),
            pl.BlockSpec((half, D_MODEL), lambda i: (1, 0), pipeline_mode=pl.Buffered(1)),
            pl.BlockSpec((1, D_MODEL), const),
            resident((D_MODEL, D_FF)),
            resident((D_MODEL, D_FF)),
            resident((D_FF, D_MODEL)),
        ],
        out_specs=pl.BlockSpec((TM_PROJ, D_MODEL), lambda i: (i, 0)),
        out_shape=jax.ShapeDtypeStruct((n, D_MODEL), jnp.float32),
        compiler_params=pltpu.CompilerParams(
            dimension_semantics=("arbitrary",), vmem_limit_bytes=VMEM_LIMIT),
        name="out_ffn",
    )(x2d, yd, ys, wo, wo, norm2_w, wg, wu, wd)


def _bias_tiles_kernel(thr_ref, vals_ref, o_ref):
    t = pl.program_id(0)
    h = pl.program_id(1)
    kpos = lax.broadcasted_iota(jnp.int32, (TKD, TQD), 0) - t * TKD
    qpos = lax.broadcasted_iota(jnp.int32, (TKD, TQD), 1)
    rel = kpos - qpos
    dist = jnp.abs(rel)
    half = NUM_BUCKETS // 2
    behind = jnp.full((TKD, TQD), vals_ref[h, 0], jnp.float32)
    ahead = jnp.full((TKD, TQD), vals_ref[h, half], jnp.float32)
    for i in range(1, half):
        reached = dist >= thr_ref[i]
        behind = jnp.where(reached, vals_ref[h, i], behind)
        ahead = jnp.where(reached, vals_ref[h, half + i], ahead)
    bias = jnp.where(rel > 0, ahead, behind)
    shift = CHUNK.bit_length() - 1
    allowed = (kpos >> shift) <= (qpos >> shift)
    o_ref[0, 0] = jnp.where(allowed, bias, MASK_VALUE)


def _diff_bias_tiles(rel_bias):
    half = NUM_BUCKETS // 2
    idx = _t5_bucket(-jnp.arange(TQD + TKD, dtype=jnp.int32))
    thr = jnp.sum((idx[None, :] < jnp.arange(half, dtype=jnp.int32)[:, None]).astype(jnp.int32), axis=1)
    vals = rel_bias.astype(jnp.float32).T * LOG2E
    smem = pl.BlockSpec(memory_space=pltpu.SMEM)
    return pl.pallas_call(
        _bias_tiles_kernel,
        grid=(2, DIFF_HEADS),
        in_specs=[smem, smem],
        out_specs=pl.BlockSpec((1, 1, TKD, TQD), lambda t, h: (t, h, 0, 0)),
        out_shape=jax.ShapeDtypeStruct((2, DIFF_HEADS, TKD, TQD), jnp.float32),
        compiler_params=pltpu.CompilerParams(dimension_semantics=("arbitrary", "arbitrary")),
        name="bias_tiles",
    )(thr.astype(jnp.int32), vals)


def kernel(x, norm1_w, w_in, q_norm_w, k_norm_w, lambda_q1, lambda_k1, lambda_q2, lambda_k2,
           diff_out_norm_w, sb_out_norm_w, w_out, norm2_w, w_gate, w_up, w_down, rel_bias):
    b, s, d = x.shape
    assert (b, s, d) == (8, 4096, D_MODEL) and w_in.shape[0] == 1
    f32, bf16 = jnp.float32, jnp.bfloat16
    x2d = x.reshape(b * s, d)

    qkw = jnp.concatenate([jnp.tile(q_norm_w[0].astype(f32), 2 * DIFF_HEADS)
                           * (DIFF_QK_DIM ** -0.5 * LOG2E),
                           jnp.tile(k_norm_w[0].astype(f32), 2 * DIFF_HEADS)])[None, :]
    grp = jnp.arange(GROUP_SLAB) // DIFF_QK_DIM
    gmat = jnp.where(grp[:, None] == grp[None, :], 1.0 / DIFF_QK_DIM, 0.0).astype(bf16)
    lam = (jnp.exp(jnp.sum(lambda_q1[0].astype(f32) * lambda_k1[0].astype(f32)))
           - jnp.exp(jnp.sum(lambda_q2[0].astype(f32) * lambda_k2[0].astype(f32)))
           + LAMBDA_INIT).reshape(1)
    bias_tiles = _diff_bias_tiles(rel_bias)
    vals = rel_bias.astype(f32) * LOG2E
    bias_scalars = jnp.stack([vals[_t5_bucket(jnp.int32(-(TKD + 1)))],
                              vals[_t5_bucket(jnp.int32(0))],
                              jnp.max(vals, axis=0)])
    jj = jnp.arange(TK)
    tri_t = (jj[None, :] >= jj[:, None]).astype(bf16)
    causal_t = (jj[:, None] < jnp.arange(TQ)[None, :]).astype(f32)
    sb_mask_t = jnp.concatenate([causal_t, causal_t], axis=1)

    proj = _in_proj(x2d, norm1_w[0].astype(f32)[None, :], w_in[0].astype(bf16), qkw, gmat)
    proj = proj.reshape(b, s, IN_COLS)
    y_diff = _diff_attn(proj, bias_tiles, bias_scalars, lam,
                        diff_out_norm_w[0].astype(f32)[None, :])
    y_sb = _sb_attn(proj, tri_t, sb_mask_t, jnp.tile(sb_out_norm_w[0].astype(f32), 2)[None, :])

    half = DIFF_HEADS * DIFF_V_DIM
    out = _out_ffn(x2d, y_diff.reshape(b * s, half), y_sb.reshape(b * s, half),
                   w_out[0].astype(bf16), norm2_w[0].astype(f32)[None, :],
                   w_gate[0].astype(bf16), w_up[0].astype(bf16), w_down[0].astype(bf16))
    return out.reshape(b, s, d)
```

```python
import math
from functools import partial

import jax
import jax.numpy as jnp
from jax import lax
from jax.experimental import pallas as pl
from jax.experimental.pallas import tpu as pltpu

D_MODEL = 1024
CHUNK = 64
DIFF_QK_DIM = 64
DIFF_V_DIM = 128
DIFF_HEADS = 4
SB_DIM = 64
SB_HEADS = 8
NUM_BUCKETS = 32
MAX_DISTANCE = 128
D_FF = 2816
EPS = 1e-6
IN_COLS = 3072
LAMBDA_INIT = 0.8 - 0.6 * math.exp(-0.3 * 0)
LANES = 128

QD_BLK, KD_BLK, VD_BLK = 0, 4, 8
QS_BLK, KS_BLK, VS_BLK = 12, 16, 20

TM_PROJ = 512
COL_CHUNK = 512
GROUP_SLAB = 256
TQD = 512
TKD = 512
HEADS_PER_STEP = 4
ONES_ROWS = 16
SCORE_LOOKAHEAD = 2
LOG2E = math.log2(math.e)
TQ = 256
TK = 256
SB_PAIRS = SB_HEADS // 2
FF_CHUNK = 256
SAFE_SHIFT_GAP = 90.0
BOUND_SLACK = 1.01
BOUND_MARGIN = 0.5
MASK_VALUE = -1e30
SB_STOP_COST = 150.1
SOFTPLUS_LINEAR_FROM = 60.0
VMEM_LIMIT = 56 * 1024 * 1024

NT_DIMS = (((1,), (1,)), ((), ()))


def _t5_bucket(rel):
    nb = NUM_BUCKETS // 2
    max_exact = nb // 2
    ret = (rel > 0).astype(jnp.int32) * nb
    n = jnp.abs(rel)
    nf = jnp.maximum(n, 1).astype(jnp.float32)
    large = max_exact + (jnp.log(nf / max_exact) / jnp.log(jnp.float32(MAX_DISTANCE / max_exact))
                         * (nb - max_exact)).astype(jnp.int32)
    large = jnp.minimum(large, nb - 1)
    return ret + jnp.where(n < max_exact, n, large)


def _split_hi_lo(x):
    hi = x.astype(jnp.bfloat16)
    lo = (x - hi.astype(jnp.float32)).astype(jnp.bfloat16)
    return hi, lo


def _in_proj_kernel(x_ref, w1_ref, win_ref, qkw_ref, g_ref, o_ref):
    x = x_ref[...]
    ms = jnp.mean(x * x, axis=-1, keepdims=True)
    u = ((x * lax.rsqrt(ms + EPS)) * w1_ref[...]).astype(jnp.bfloat16)
    g = g_ref[...]
    for c in range(IN_COLS // COL_CHUNK):
        lo_c, hi_c = c * COL_CHUNK, (c + 1) * COL_CHUNK
        p = jnp.dot(u, win_ref[:, lo_c:hi_c], preferred_element_type=jnp.float32)
        if hi_c <= 2 * DIFF_HEADS * 2 * DIFF_QK_DIM:
            for j in range(COL_CHUNK // GROUP_SLAB):
                sl = slice(j * GROUP_SLAB, (j + 1) * GROUP_SLAB)
                pj = p[:, sl]
                msq = jnp.dot((pj * pj).astype(jnp.bfloat16), g, preferred_element_type=jnp.float32)
                w = qkw_ref[:, lo_c + j * GROUP_SLAB: lo_c + (j + 1) * GROUP_SLAB]
                o_ref[:, lo_c + j * GROUP_SLAB: lo_c + (j + 1) * GROUP_SLAB] = (
                    (pj * lax.rsqrt(msq + EPS)) * w).astype(jnp.bfloat16)
        else:
            if lo_c == QS_BLK * LANES:
                p = p * (SB_DIM ** -0.5 * LOG2E)
            o_ref[:, lo_c:hi_c] = p.astype(jnp.bfloat16)


def _in_proj(x2d, norm1_w, w_in_bf, qkw, gmat):
    n = x2d.shape[0]
    const = lambda i: (0, 0)
    return pl.pallas_call(
        _in_proj_kernel,
        grid=(n // TM_PROJ,),
        in_specs=[
            pl.BlockSpec((TM_PROJ, D_MODEL), lambda i: (i, 0)),
            pl.BlockSpec((1, D_MODEL), const),
            pl.BlockSpec((D_MODEL, IN_COLS), const, pipeline_mode=pl.Buffered(1)),
            pl.BlockSpec((1, 2 * DIFF_HEADS * 2 * DIFF_QK_DIM), const),
            pl.BlockSpec((GROUP_SLAB, GROUP_SLAB), const),
        ],
        out_specs=pl.BlockSpec((TM_PROJ, IN_COLS), lambda i: (i, 0)),
        out_shape=jax.ShapeDtypeStruct((n, IN_COLS), jnp.bfloat16),
        compiler_params=pltpu.CompilerParams(
            dimension_semantics=("arbitrary",), vmem_limit_bytes=VMEM_LIMIT),
        name="in_proj",
    )(x2d, norm1_w, w_in_bf, qkw, gmat)


def _stack_subheads(q):
    lane = lax.broadcasted_iota(jnp.int32, q.shape, 1)
    zero = jnp.zeros_like(q)
    return jnp.concatenate([jnp.where(lane < SB_DIM, q, zero),
                            jnp.where(lane >= SB_DIM, q, zero)], axis=0)


def _diff_attn_kernel(bsc_ref, lam_ref, q_ref, k_ref, v_ref, bdiag_ref, bsub_ref, w_ref, o_ref,
                      vt_sc, m_sc, acc_sc, kmax_sc):
    hp = pl.program_id(1)
    qi = pl.program_id(2)
    n_maps = 2 * HEADS_PER_STEP
    lane = lax.broadcasted_iota(jnp.int32, (TQD, LANES), 1)
    sub_lanes = (lane < DIFF_QK_DIM, lane >= DIFF_QK_DIM)
    ones8 = jnp.ones((8, LANES), jnp.bfloat16)

    def lane_sums(x):
        return lax.dot_general(ones8, x.astype(jnp.bfloat16), NT_DIMS,
                               preferred_element_type=jnp.float32)[0:1, :]

    @pl.when(qi == 0)
    def _per_sequence_setup():
        ones = jnp.ones((ONES_ROWS, TKD), vt_sc.dtype)
        for hh in range(HEADS_PER_STEP):
            kmax = [None, None]
            for t in range(vt_sc.shape[1]):
                blk = v_ref[0, t * TKD:(t + 1) * TKD, hh * LANES:(hh + 1) * LANES]
                vt_sc[hh, t, :DIFF_V_DIM, :] = blk.astype(jnp.float32).T.astype(vt_sc.dtype)
                vt_sc[hh, t, DIFF_V_DIM:, :] = ones
                kf = k_ref[0, t * TKD:(t + 1) * TKD, hh * LANES:(hh + 1) * LANES].astype(jnp.float32)
                for c in range(2):
                    n2 = jnp.max(lane_sums(jnp.where(sub_lanes[c], kf * kf, 0.0)))
                    kmax[c] = n2 if kmax[c] is None else jnp.maximum(kmax[c], n2)
            for c in range(2):
                kmax_sc[2 * hh + c] = kmax[c]

    zero = jnp.zeros((TQD, LANES), q_ref.dtype)
    q_sub = []
    for hh in range(HEADS_PER_STEP):
        q = q_ref[0, :, hh * LANES:(hh + 1) * LANES]
        q_sub.append(jnp.where(sub_lanes[0], q, zero))
        q_sub.append(jnp.where(sub_lanes[1], q, zero))

    gap = None
    for idx in range(n_maps):
        head = HEADS_PER_STEP * hp + idx // 2
        qf = q_sub[idx].astype(jnp.float32)
        reach = jnp.sqrt(lane_sums(qf * qf) * kmax_sc[idx]) * BOUND_SLACK
        m_sc[idx] = jnp.broadcast_to(reach + (bsc_ref[2, head] + BOUND_MARGIN), (8, TQD))
        g = 2.0 * jnp.max(reach) + (bsc_ref[2, head] - bsc_ref[1, head] + BOUND_MARGIN)
        gap = g if gap is None else jnp.maximum(gap, g)
    fixed_shift_ok = gap <= SAFE_SHIFT_GAP

    def run_tiles(tiles, fixed):
        def scores(kt, idx):
            ks = pl.multiple_of(kt * TKD, TKD)
            k = k_ref[0, pl.ds(ks, TKD), (idx // 2) * LANES:(idx // 2 + 1) * LANES]
            return lax.dot_general(k, q_sub[idx], NT_DIMS, preferred_element_type=jnp.float32)

        def weights(s, idx, bias_ref, cfar, first):
            hh = idx // 2
            if fixed:
                shift = m_sc[idx][0:1, :]
                if bias_ref is None:
                    shift = shift - cfar[hh]
                else:
                    s = s + bias_ref[0, hh]
                return jnp.exp2(s - shift).astype(jnp.bfloat16), None
            if first:
                s = s + bias_ref[0, hh]
                m_new = jnp.max(s, axis=0, keepdims=True)
                m_sc[idx] = jnp.broadcast_to(m_new, (8, TQD))
                return jnp.exp2(s - m_new).astype(jnp.bfloat16), None
            m_old = m_sc[idx][0:1, :]
            if bias_ref is None:
                m_new = jnp.maximum(m_old, jnp.max(s, axis=0, keepdims=True) + cfar[hh])
                shift = m_new - cfar[hh]
            else:
                s = s + bias_ref[0, hh]
                m_new = jnp.maximum(m_old, jnp.max(s, axis=0, keepdims=True))
                shift = m_new
            m_sc[idx] = jnp.broadcast_to(m_new, (8, TQD))
            return jnp.exp2(s - shift).astype(jnp.bfloat16), jnp.exp2(m_old - m_new)

        def accumulate(kt, idx, p, alpha, first):
            pv = jnp.dot(vt_sc[idx // 2, kt], p, preferred_element_type=jnp.float32)
            if first:
                acc_sc[idx] = pv
            elif alpha is None:
                acc_sc[idx] = acc_sc[idx] + pv
            else:
                acc_sc[idx] = alpha * acc_sc[idx] + pv

        work = [(kt, idx, bias_ref, cfar, first)
                for (kt, bias_ref, cfar, first) in tiles for idx in range(n_maps)]
        s_next = [scores(w[0], w[1]) for w in work[:SCORE_LOOKAHEAD]]
        pending = None
        for n, (kt, idx, bias_ref, cfar, first) in enumerate(work):
            p, alpha = weights(s_next.pop(0), idx, bias_ref, cfar, first)
            if n + SCORE_LOOKAHEAD < len(work):
                nxt = work[n + SCORE_LOOKAHEAD]
                s_next.append(scores(nxt[0], nxt[1]))
            if pending is not None:
                accumulate(*pending)
            pending = (kt, idx, p, alpha, first)
        accumulate(*pending)

    cfar = [bsc_ref[0, HEADS_PER_STEP * hp + hh] for hh in range(HEADS_PER_STEP)]
    n_far = jnp.maximum(qi - 1, 0)

    def all_tiles(fixed):
        @pl.when(qi == 0)
        def _first_query_tile():
            run_tiles([(qi, bdiag_ref, None, True)], fixed)

        @pl.when(qi >= 1)
        def _near_tiles():
            run_tiles([(qi, bdiag_ref, None, True), (qi - 1, bsub_ref, None, False)], fixed)

        def far_pair(i, carry):
            run_tiles([(2 * i, None, cfar, False), (2 * i + 1, None, cfar, False)], fixed)
            return carry

        lax.fori_loop(0, n_far // 2, far_pair, 0)

        @pl.when(n_far % 2 == 1)
        def _last_far_tile():
            run_tiles([(n_far - 1, None, cfar, False)], fixed)

    @pl.when(fixed_shift_ok)
    def _fixed_shift_softmax():
        all_tiles(True)

    @pl.when(jnp.logical_not(fixed_shift_ok))
    def _online_softmax():
        all_tiles(False)

    for hh in range(HEADS_PER_STEP):
        acc0 = acc_sc[2 * hh]
        acc1 = acc_sc[2 * hh + 1]
        o_t = (acc0[:DIFF_V_DIM] / acc0[DIFF_V_DIM:DIFF_V_DIM + 1]
               - lam_ref[0] * (acc1[:DIFF_V_DIM] / acc1[DIFF_V_DIM:DIFF_V_DIM + 1]))
        o = o_t.T
        ms = jnp.mean(o * o, axis=-1, keepdims=True)
        y = ((o * lax.rsqrt(ms + EPS)) * w_ref[...]) * (1.0 - LAMBDA_INIT)
        o_ref[0, :, hh * LANES:(hh + 1) * LANES] = y.astype(o_ref.dtype)


def _diff_attn(proj, bias_tiles, bias_scalars, lam, w_out_norm):
    b, s, _ = proj.shape
    nq = s // TQD
    hps = HEADS_PER_STEP
    wide = hps * LANES
    smem = pl.BlockSpec(memory_space=pltpu.SMEM)
    return pl.pallas_call(
        _diff_attn_kernel,
        grid=(b, DIFF_HEADS // hps, nq),
        in_specs=[
            smem, smem,
            pl.BlockSpec((1, TQD, wide), lambda bi, hp, qi: (bi, qi, QD_BLK // hps + hp)),
            pl.BlockSpec((1, s, wide), lambda bi, hp, qi: (bi, 0, KD_BLK // hps + hp)),
            pl.BlockSpec((1, s, wide), lambda bi, hp, qi: (bi, 0, VD_BLK // hps + hp)),
            pl.BlockSpec((1, hps, TKD, TQD), lambda bi, hp, qi: (0, hp, 0, 0)),
            pl.BlockSpec((1, hps, TKD, TQD), lambda bi, hp, qi: (1, hp, 0, 0)),
            pl.BlockSpec((1, DIFF_V_DIM), lambda bi, hp, qi: (0, 0)),
        ],
        out_specs=pl.BlockSpec((1, TQD, wide), lambda bi, hp, qi: (bi, qi, hp)),
        out_shape=jax.ShapeDtypeStruct((b, s, DIFF_HEADS * DIFF_V_DIM), jnp.bfloat16),
        scratch_shapes=[
            pltpu.VMEM((hps, s // TKD, DIFF_V_DIM + ONES_ROWS, TKD), jnp.bfloat16),
            pltpu.VMEM((2 * hps, 8, TQD), jnp.float32),
            pltpu.VMEM((2 * hps, DIFF_V_DIM + ONES_ROWS, TQD), jnp.float32),
            pltpu.SMEM((2 * hps,), jnp.float32),
        ],
        compiler_params=pltpu.CompilerParams(
            dimension_semantics=("arbitrary", "arbitrary", "arbitrary"),
            vmem_limit_bytes=VMEM_LIMIT),
        name="diff_attn",
    )(bias_scalars, lam, proj, proj, proj, bias_tiles, bias_tiles, w_out_norm)


def _softplus2(y):
    return jnp.where(y > SOFTPLUS_LINEAR_FROM, y, jnp.log(1.0 + jnp.exp2(y)) * LOG2E)


def _sb_attn_kernel(q_ref, k_ref, v_ref, tri_ref, mask_ref, w_ref, o_ref, vt_sc, r_sc, acc_sc,
                    rmin_sc):
    qi = pl.program_id(1)
    tri_t = tri_ref[...]

    @pl.when(qi == 0)
    def _build_vt():
        for j in range(SB_PAIRS):
            for t in range(vt_sc.shape[1]):
                blk = v_ref[0, t * TK:(t + 1) * TK, j * LANES:(j + 1) * LANES]
                vt_sc[j, t] = blk.astype(jnp.float32).T.astype(vt_sc.dtype)

    qq = [_stack_subheads(q_ref[0, :, j * LANES:(j + 1) * LANES]) for j in range(SB_PAIRS)]

    def run_tiles(tiles):
        work = [(kt, diagonal, j) for (kt, diagonal) in tiles for j in range(SB_PAIRS)]
        z = []
        for kt, _, j in work:
            ks = pl.multiple_of(kt * TK, TK)
            z.append(lax.dot_general(k_ref[0, pl.ds(ks, TK), j * LANES:(j + 1) * LANES], qq[j],
                                     NT_DIMS, preferred_element_type=jnp.float32))
        csum = []
        for n, (kt, diagonal, j) in enumerate(work):
            sp = _softplus2(z[n])
            c = sp * mask_ref[...] if diagonal else sp
            hi, lo = _split_hi_lo(c)
            csum.append(jnp.dot(tri_t, hi, preferred_element_type=jnp.float32)
                        + jnp.dot(tri_t, lo, preferred_element_type=jnp.float32))
        r_min = None
        for n, (kt, diagonal, j) in enumerate(work):
            if diagonal:
                a = jnp.exp2(z[n] - csum[n]) * mask_ref[...]
                r_new = csum[n][0:1, :]
            else:
                r = r_sc[j][0:1, :]
                a = jnp.exp2(z[n] - csum[n] - r)
                r_new = r + csum[n][0:1, :]
            av = jnp.dot(vt_sc[j, kt], a.astype(jnp.bfloat16), preferred_element_type=jnp.float32)
            acc_sc[j] = av if diagonal else acc_sc[j] + av
            r_sc[j] = jnp.broadcast_to(r_new, (8, 2 * TQ))
            if n >= len(work) - SB_PAIRS:
                r_min = jnp.min(r_new) if r_min is None else jnp.minimum(r_min, jnp.min(r_new))
        return r_min

    @pl.when(qi == 0)
    def _first_query_tile():
        run_tiles([(qi, True)])
        rmin_sc[0] = jnp.float32(0.0)

    @pl.when(qi >= 1)
    def _diagonal_and_previous():
        rmin_sc[0] = run_tiles([(qi, True), (qi - 1, False)])

    r_min0 = rmin_sc[0]

    def more(carry):
        i, r_min = carry
        return jnp.logical_and(i < qi, r_min < SB_STOP_COST)

    def step(carry):
        i, _ = carry
        return i + 1, run_tiles([(qi - 1 - i, False)])

    lax.while_loop(more, step, (jnp.int32(1), r_min0))

    lane = lax.broadcasted_iota(jnp.int32, (TQ, LANES), 1)
    first = lane < SB_DIM
    for j in range(SB_PAIRS):
        acc = acc_sc[j].T
        o = jnp.where(first, acc[:TQ], acc[TQ:])
        o2 = o * o
        ss_a = jnp.sum(jnp.where(first, o2, 0.0), axis=-1, keepdims=True)
        ss_b = jnp.sum(jnp.where(first, 0.0, o2), axis=-1, keepdims=True)
        ms = jnp.where(first, ss_a, ss_b) * (1.0 / SB_DIM)
        o_ref[0, :, j * LANES:(j + 1) * LANES] = (
            (o * lax.rsqrt(ms + EPS)) * w_ref[...]).astype(o_ref.dtype)


def _sb_attn(proj, tri_t, mask_t, w_out_norm):
    b, s, _ = proj.shape
    nq = s // TQ
    wide = SB_PAIRS * LANES
    return pl.pallas_call(
        _sb_attn_kernel,
        grid=(b, nq),
        in_specs=[
            pl.BlockSpec((1, TQ, wide), lambda bi, qi: (bi, qi, QS_BLK // SB_PAIRS)),
            pl.BlockSpec((1, s, wide), lambda bi, qi: (bi, 0, KS_BLK // SB_PAIRS)),
            pl.BlockSpec((1, s, wide), lambda bi, qi: (bi, 0, VS_BLK // SB_PAIRS)),
            pl.BlockSpec((TK, TK), lambda bi, qi: (0, 0)),
            pl.BlockSpec((TK, 2 * TQ), lambda bi, qi: (0, 0)),
            pl.BlockSpec((1, LANES), lambda bi, qi: (0, 0)),
        ],
        out_specs=pl.BlockSpec((1, TQ, wide), lambda bi, qi: (bi, qi, 0)),
        out_shape=jax.ShapeDtypeStruct((b, s, SB_HEADS * SB_DIM), jnp.bfloat16),
        scratch_shapes=[
            pltpu.VMEM((SB_PAIRS, s // TK, LANES, TK), jnp.bfloat16),
            pltpu.VMEM((SB_PAIRS, 8, 2 * TQ), jnp.float32),
            pltpu.VMEM((SB_PAIRS, LANES, 2 * TQ), jnp.float32),
            pltpu.SMEM((1,), jnp.float32),
        ],
        compiler_params=pltpu.CompilerParams(
            dimension_semantics=("arbitrary", "arbitrary"),
            vmem_limit_bytes=VMEM_LIMIT),
        name="sb_attn",
    )(proj, proj, proj, tri_t, mask_t, w_out_norm)


def _out_ffn_kernel(x_ref, yd_ref, ys_ref, woa_ref, wob_ref, w2_ref, wg_ref, wu_ref, wd_ref,
                    o_ref):
    h1 = (x_ref[...]
          + jnp.dot(yd_ref[...], woa_ref[...], preferred_element_type=jnp.float32)
          + jnp.dot(ys_ref[...], wob_ref[...], preferred_element_type=jnp.float32))
    ms = jnp.mean(h1 * h1, axis=-1, keepdims=True)
    u2 = ((h1 * lax.rsqrt(ms + EPS)) * w2_ref[...]).astype(jnp.bfloat16)
    ffn = None
    for c in range(D_FF // FF_CHUNK):
        sl = slice(c * FF_CHUNK, (c + 1) * FF_CHUNK)
        gate = jnp.dot(u2, wg_ref[:, sl], preferred_element_type=jnp.float32)
        up = jnp.dot(u2, wu_ref[:, sl], preferred_element_type=jnp.float32)
        act = (gate * (1.0 / (1.0 + jnp.exp(-gate))) * up).astype(jnp.bfloat16)
        part = jnp.dot(act, wd_ref[sl, :], preferred_element_type=jnp.float32)
        ffn = part if ffn is None else ffn + part
    o_ref[...] = h1 + ffn


def _out_ffn(x2d, yd, ys, wo, norm2_w, wg, wu, wd):
    n = x2d.shape[0]
    const = lambda i: (0, 0)
    half = DIFF_HEADS * DIFF_V_DIM
    resident = partial(pl.BlockSpec, index_map=const, pipeline_mode=pl.Buffered(1))
    return pl.pallas_call(
        _out_ffn_kernel,
        grid=(n // TM_PROJ,),
        in_specs=[
            pl.BlockSpec((TM_PROJ, D_MODEL), lambda i: (i, 0)),
            pl.BlockSpec((TM_PROJ, half), lambda i: (i, 0)),
            pl.BlockSpec((TM_PROJ, half), lambda i: (i, 0)),
            resident((half, D_MODEL)),
            pl.BlockSpec((half, D_MODEL), lambda i: (1, 0), pipeline_mode=pl.Buffered(1)),
            pl.BlockSpec((1, D_MODEL), const),
            resident((D_MODEL, D_FF)),
            resident((D_MODEL, D_FF)),
            resident((D_FF, D_MODEL)),
        ],
        out_specs=pl.BlockSpec((TM_PROJ, D_MODEL), lambda i: (i, 0)),
        out_shape=jax.ShapeDtypeStruct((n, D_MODEL), jnp.float32),
        compiler_params=pltpu.CompilerParams(
            dimension_semantics=("arbitrary",), vmem_limit_bytes=VMEM_LIMIT),
        name="out_ffn",
    )(x2d, yd, ys, wo, wo, norm2_w, wg, wu, wd)


def _bias_tiles_kernel(thr_ref, vals_ref, o_ref):
    t = pl.program_id(0)
    h = pl.program_id(1)
    kpos = lax.broadcasted_iota(jnp.int32, (TKD, TQD), 0) - t * TKD
    qpos = lax.broadcasted_iota(jnp.int32, (TKD, TQD), 1)
    rel = kpos - qpos
    dist = jnp.abs(rel)
    half = NUM_BUCKETS // 2
    behind = jnp.full((TKD, TQD), vals_ref[h, 0], jnp.float32)
    ahead = jnp.full((TKD, TQD), vals_ref[h, half], jnp.float32)
    for i in range(1, half):
        reached = dist >= thr_ref[i]
        behind = jnp.where(reached, vals_ref[h, i], behind)
        ahead = jnp.where(reached, vals_ref[h, half + i], ahead)
    bias = jnp.where(rel > 0, ahead, behind)
    shift = CHUNK.bit_length() - 1
    allowed = (kpos >> shift) <= (qpos >> shift)
    o_ref[0, 0] = jnp.where(allowed, bias, MASK_VALUE)


def _diff_bias_tiles(rel_bias):
    half = NUM_BUCKETS // 2
    idx = _t5_bucket(-jnp.arange(TQD + TKD, dtype=jnp.int32))
    thr = jnp.sum((idx[None, :] < jnp.arange(half, dtype=jnp.int32)[:, None]).astype(jnp.int32), axis=1)
    vals = rel_bias.astype(jnp.float32).T * LOG2E
    smem = pl.BlockSpec(memory_space=pltpu.SMEM)
    return pl.pallas_call(
        _bias_tiles_kernel,
        grid=(2, DIFF_HEADS),
        in_specs=[smem, smem],
        out_specs=pl.BlockSpec((1, 1, TKD, TQD), lambda t, h: (t, h, 0, 0)),
        out_shape=jax.ShapeDtypeStruct((2, DIFF_HEADS, TKD, TQD), jnp.float32),
        compiler_params=pltpu.CompilerParams(dimension_semantics=("arbitrary", "arbitrary")),
        name="bias_tiles",
    )(thr.astype(jnp.int32), vals)


def kernel(x, norm1_w, w_in, q_norm_w, k_norm_w, lambda_q1, lambda_k1, lambda_q2, lambda_k2,
           diff_out_norm_w, sb_out_norm_w, w_out, norm2_w, w_gate, w_up, w_down, rel_bias):
    b, s, d = x.shape
    assert (b, s, d) == (8, 4096, D_MODEL) and w_in.shape[0] == 1
    f32, bf16 = jnp.float32, jnp.bfloat16
    x2d = x.reshape(b * s, d)

    qkw = jnp.concatenate([jnp.tile(q_norm_w[0].astype(f32), 2 * DIFF_HEADS)
                           * (DIFF_QK_DIM ** -0.5 * LOG2E),
                           jnp.tile(k_norm_w[0].astype(f32), 2 * DIFF_HEADS)])[None, :]
    grp = jnp.arange(GROUP_SLAB) // DIFF_QK_DIM
    gmat = jnp.where(grp[:, None] == grp[None, :], 1.0 / DIFF_QK_DIM, 0.0).astype(bf16)
    lam = (jnp.exp(jnp.sum(lambda_q1[0].astype(f32) * lambda_k1[0].astype(f32)))
           - jnp.exp(jnp.sum(lambda_q2[0].astype(f32) * lambda_k2[0].astype(f32)))
           + LAMBDA_INIT).reshape(1)
    bias_tiles = _diff_bias_tiles(rel_bias)
    vals = rel_bias.astype(f32) * LOG2E
    bias_scalars = jnp.stack([vals[_t5_bucket(jnp.int32(-(TKD + 1)))],
                              vals[_t5_bucket(jnp.int32(0))],
                              jnp.max(vals, axis=0)])
    jj = jnp.arange(TK)
    tri_t = (jj[None, :] >= jj[:, None]).astype(bf16)
    causal_t = (jj[:, None] < jnp.arange(TQ)[None, :]).astype(f32)
    sb_mask_t = jnp.concatenate([causal_t, causal_t], axis=1)

    proj = _in_proj(x2d, norm1_w[0].astype(f32)[None, :], w_in[0].astype(bf16), qkw, gmat)
    proj = proj.reshape(b, s, IN_COLS)
    y_diff = _diff_attn(proj, bias_tiles, bias_scalars, lam,
                        diff_out_norm_w[0].astype(f32)[None, :])
    y_sb = _sb_attn(proj, tri_t, sb_mask_t, jnp.tile(sb_out_norm_w[0].astype(f32), 2)[None, :])

    half = DIFF_HEADS * DIFF_V_DIM
    out = _out_ffn(x2d, y_diff.reshape(b * s, half), y_sb.reshape(b * s, half),
                   w_out[0].astype(bf16), norm2_w[0].astype(f32)[None, :],
                   w_gate[0].astype(bf16), w_up[0].astype(bf16), w_down[0].astype(bf16))
    return out.reshape(b, s, d)
```

```python
import math
from functools import partial

import jax
import jax.numpy as jnp
from jax import lax
from jax.experimental import pallas as pl
from jax.experimental.pallas import tpu as pltpu

D_MODEL = 1024
CHUNK = 64
DIFF_QK_DIM = 64
DIFF_V_DIM = 128
DIFF_HEADS = 4
SB_DIM = 64
SB_HEADS = 8
NUM_BUCKETS = 32
MAX_DISTANCE = 128
D_FF = 2816
EPS = 1e-6
IN_COLS = 3072
LAMBDA_INIT = 0.8 - 0.6 * math.exp(-0.3 * 0)
LANES = 128

QD_BLK, KD_BLK, VD_BLK = 0, 4, 8
QS_BLK, KS_BLK, VS_BLK = 12, 16, 20

TM_IN = 1024
TM_PROJ = 1024
COL_CHUNK = 512
GROUP_SLAB = 256
TQD = 512
TKD = 512
HEADS_PER_STEP = 4
ONES_ROWS = 16
SCORE_LOOKAHEAD = 2
LOG2E = math.log2(math.e)
TQ = 256
TK = 256
SB_PAIRS = SB_HEADS // 2
FF_CHUNK = 256
SAFE_SHIFT_GAP = 90.0
BOUND_SLACK = 1.01
BOUND_MARGIN = 0.5
MASK_VALUE = -1e30
SB_STOP_COST = 150.1
SOFTPLUS_LINEAR_FROM = 60.0
VMEM_LIMIT = 56 * 1024 * 1024

NT_DIMS = (((1,), (1,)), ((), ()))


def _t5_bucket(rel):
    nb = NUM_BUCKETS // 2
    max_exact = nb // 2
    ret = (rel > 0).astype(jnp.int32) * nb
    n = jnp.abs(rel)
    nf = jnp.maximum(n, 1).astype(jnp.float32)
    large = max_exact + (jnp.log(nf / max_exact) / jnp.log(jnp.float32(MAX_DISTANCE / max_exact))
                         * (nb - max_exact)).astype(jnp.int32)
    large = jnp.minimum(large, nb - 1)
    return ret + jnp.where(n < max_exact, n, large)


def _split_hi_lo(x):
    hi = x.astype(jnp.bfloat16)
    lo = (x - hi.astype(jnp.float32)).astype(jnp.bfloat16)
    return hi, lo


def _in_proj_kernel(x_ref, w1_ref, win_ref, qkw_ref, g_ref, o_ref):
    x = x_ref[...]
    ms = jnp.mean(x * x, axis=-1, keepdims=True)
    u = ((x * lax.rsqrt(ms + EPS)) * w1_ref[...]).astype(jnp.bfloat16)
    g = g_ref[...]
    for c in range(IN_COLS // COL_CHUNK):
        lo_c, hi_c = c * COL_CHUNK, (c + 1) * COL_CHUNK
        p = jnp.dot(u, win_ref[:, lo_c:hi_c], preferred_element_type=jnp.float32)
        if hi_c <= 2 * DIFF_HEADS * 2 * DIFF_QK_DIM:
            for j in range(COL_CHUNK // GROUP_SLAB):
                sl = slice(j * GROUP_SLAB, (j + 1) * GROUP_SLAB)
                pj = p[:, sl]
                msq = jnp.dot((pj * pj).astype(jnp.bfloat16), g, preferred_element_type=jnp.float32)
                w = qkw_ref[:, lo_c + j * GROUP_SLAB: lo_c + (j + 1) * GROUP_SLAB]
                o_ref[:, lo_c + j * GROUP_SLAB: lo_c + (j + 1) * GROUP_SLAB] = (
                    (pj * lax.rsqrt(msq + EPS)) * w).astype(jnp.bfloat16)
        else:
            if lo_c == QS_BLK * LANES:
                p = p * (SB_DIM ** -0.5 * LOG2E)
            o_ref[:, lo_c:hi_c] = p.astype(jnp.bfloat16)


def _in_proj(x2d, norm1_w, w_in_bf, qkw, gmat):
    n = x2d.shape[0]
    const = lambda i: (0, 0)
    return pl.pallas_call(
        _in_proj_kernel,
        grid=(n // TM_IN,),
        in_specs=[
            pl.BlockSpec((TM_IN, D_MODEL), lambda i: (i, 0)),
            pl.BlockSpec((1, D_MODEL), const),
            pl.BlockSpec((D_MODEL, IN_COLS), const, pipeline_mode=pl.Buffered(1)),
            pl.BlockSpec((1, 2 * DIFF_HEADS * 2 * DIFF_QK_DIM), const),
            pl.BlockSpec((GROUP_SLAB, GROUP_SLAB), const),
        ],
        out_specs=pl.BlockSpec((TM_IN, IN_COLS), lambda i: (i, 0)),
        out_shape=jax.ShapeDtypeStruct((n, IN_COLS), jnp.bfloat16),
        compiler_params=pltpu.CompilerParams(
            dimension_semantics=("arbitrary",), vmem_limit_bytes=VMEM_LIMIT),
        name="in_proj",
    )(x2d, norm1_w, w_in_bf, qkw, gmat)


def _stack_subheads(q):
    lane = lax.broadcasted_iota(jnp.int32, q.shape, 1)
    zero = jnp.zeros_like(q)
    return jnp.concatenate([jnp.where(lane < SB_DIM, q, zero),
                            jnp.where(lane >= SB_DIM, q, zero)], axis=0)


def _diff_attn_kernel(bsc_ref, lam_ref, q_ref, k_ref, v_ref, bdiag_ref, bsub_ref, w_ref, o_ref,
                      vt_sc, m_sc, acc_sc, kmax_sc):
    hp = pl.program_id(1)
    qi = pl.program_id(2)
    n_maps = 2 * HEADS_PER_STEP
    lane = lax.broadcasted_iota(jnp.int32, (TQD, LANES), 1)
    sub_lanes = (lane < DIFF_QK_DIM, lane >= DIFF_QK_DIM)
    ones8 = jnp.ones((8, LANES), jnp.bfloat16)

    def lane_sums(x):
        return lax.dot_general(ones8, x.astype(jnp.bfloat16), NT_DIMS,
                               preferred_element_type=jnp.float32)[0:1, :]

    @pl.when(qi == 0)
    def _per_sequence_setup():
        ones = jnp.ones((ONES_ROWS, TKD), vt_sc.dtype)
        for hh in range(HEADS_PER_STEP):
            kmax = [None, None]
            for t in range(vt_sc.shape[1]):
                blk = v_ref[0, t * TKD:(t + 1) * TKD, hh * LANES:(hh + 1) * LANES]
                vt_sc[hh, t, :DIFF_V_DIM, :] = blk.astype(jnp.float32).T.astype(vt_sc.dtype)
                vt_sc[hh, t, DIFF_V_DIM:, :] = ones
                kf = k_ref[0, t * TKD:(t + 1) * TKD, hh * LANES:(hh + 1) * LANES].astype(jnp.float32)
                for c in range(2):
                    n2 = jnp.max(lane_sums(jnp.where(sub_lanes[c], kf * kf, 0.0)))
                    kmax[c] = n2 if kmax[c] is None else jnp.maximum(kmax[c], n2)
            for c in range(2):
                kmax_sc[2 * hh + c] = kmax[c]

    zero = jnp.zeros((TQD, LANES), q_ref.dtype)
    q_sub = []
    for hh in range(HEADS_PER_STEP):
        q = q_ref[0, :, hh * LANES:(hh + 1) * LANES]
        q_sub.append(jnp.where(sub_lanes[0], q, zero))
        q_sub.append(jnp.where(sub_lanes[1], q, zero))

    gap = None
    for idx in range(n_maps):
        head = HEADS_PER_STEP * hp + idx // 2
        qf = q_sub[idx].astype(jnp.float32)
        reach = jnp.sqrt(lane_sums(qf * qf) * kmax_sc[idx]) * BOUND_SLACK
        m_sc[idx] = jnp.broadcast_to(reach + (bsc_ref[2, head] + BOUND_MARGIN), (8, TQD))
        g = 2.0 * jnp.max(reach) + (bsc_ref[2, head] - bsc_ref[1, head] + BOUND_MARGIN)
        gap = g if gap is None else jnp.maximum(gap, g)
    fixed_shift_ok = gap <= SAFE_SHIFT_GAP

    def run_tiles(tiles, fixed):
        def scores(kt, idx):
            ks = pl.multiple_of(kt * TKD, TKD)
            k = k_ref[0, pl.ds(ks, TKD), (idx // 2) * LANES:(idx // 2 + 1) * LANES]
            return lax.dot_general(k, q_sub[idx], NT_DIMS, preferred_element_type=jnp.float32)

        def weights(s, idx, bias_ref, cfar, first):
            hh = idx // 2
            if fixed:
                shift = m_sc[idx][0:1, :]
                if bias_ref is None:
                    shift = shift - cfar[hh]
                else:
                    s = s + bias_ref[0, hh]
                return jnp.exp2(s - shift).astype(jnp.bfloat16), None
            if first:
                s = s + bias_ref[0, hh]
                m_new = jnp.max(s, axis=0, keepdims=True)
                m_sc[idx] = jnp.broadcast_to(m_new, (8, TQD))
                return jnp.exp2(s - m_new).astype(jnp.bfloat16), None
            m_old = m_sc[idx][0:1, :]
            if bias_ref is None:
                m_new = jnp.maximum(m_old, jnp.max(s, axis=0, keepdims=True) + cfar[hh])
                shift = m_new - cfar[hh]
            else:
                s = s + bias_ref[0, hh]
                m_new = jnp.maximum(m_old, jnp.max(s, axis=0, keepdims=True))
                shift = m_new
            m_sc[idx] = jnp.broadcast_to(m_new, (8, TQD))
            return jnp.exp2(s - shift).astype(jnp.bfloat16), jnp.exp2(m_old - m_new)

        def accumulate(kt, idx, p, alpha, first):
            pv = jnp.dot(vt_sc[idx // 2, kt], p, preferred_element_type=jnp.float32)
            if first:
                acc_sc[idx] = pv
            elif alpha is None:
                acc_sc[idx] = acc_sc[idx] + pv
            else:
                acc_sc[idx] = alpha * acc_sc[idx] + pv

        work = [(kt, idx, bias_ref, cfar, first)
                for (kt, bias_ref, cfar, first) in tiles for idx in range(n_maps)]
        s_next = [scores(w[0], w[1]) for w in work[:SCORE_LOOKAHEAD]]
        pending = None
        for n, (kt, idx, bias_ref, cfar, first) in enumerate(work):
            p, alpha = weights(s_next.pop(0), idx, bias_ref, cfar, first)
            if n + SCORE_LOOKAHEAD < len(work):
                nxt = work[n + SCORE_LOOKAHEAD]
                s_next.append(scores(nxt[0], nxt[1]))
            if pending is not None:
                accumulate(*pending)
            pending = (kt, idx, p, alpha, first)
        accumulate(*pending)

    cfar = [bsc_ref[0, HEADS_PER_STEP * hp + hh] for hh in range(HEADS_PER_STEP)]
    n_far = jnp.maximum(qi - 1, 0)

    def all_tiles(fixed):
        @pl.when(qi == 0)
        def _first_query_tile():
            run_tiles([(qi, bdiag_ref, None, True)], fixed)

        @pl.when(qi >= 1)
        def _near_tiles():
            run_tiles([(qi, bdiag_ref, None, True), (qi - 1, bsub_ref, None, False)], fixed)

        def far_pair(i, carry):
            run_tiles([(2 * i, None, cfar, False), (2 * i + 1, None, cfar, False)], fixed)
            return carry

        lax.fori_loop(0, n_far // 2, far_pair, 0)

        @pl.when(n_far % 2 == 1)
        def _last_far_tile():
            run_tiles([(n_far - 1, None, cfar, False)], fixed)

    @pl.when(fixed_shift_ok)
    def _fixed_shift_softmax():
        all_tiles(True)

    @pl.when(jnp.logical_not(fixed_shift_ok))
    def _online_softmax():
        all_tiles(False)

    for hh in range(HEADS_PER_STEP):
        acc0 = acc_sc[2 * hh]
        acc1 = acc_sc[2 * hh + 1]
        o_t = (acc0[:DIFF_V_DIM] / acc0[DIFF_V_DIM:DIFF_V_DIM + 1]
               - lam_ref[0] * (acc1[:DIFF_V_DIM] / acc1[DIFF_V_DIM:DIFF_V_DIM + 1]))
        o = o_t.T
        ms = jnp.mean(o * o, axis=-1, keepdims=True)
        y = ((o * lax.rsqrt(ms + EPS)) * w_ref[...]) * (1.0 - LAMBDA_INIT)
        o_ref[0, :, hh * LANES:(hh + 1) * LANES] = y.astype(o_ref.dtype)


def _diff_attn(proj, bias_tiles, bias_scalars, lam, w_out_norm):
    b, s, _ = proj.shape
    nq = s // TQD
    hps = HEADS_PER_STEP
    wide = hps * LANES
    smem = pl.BlockSpec(memory_space=pltpu.SMEM)
    return pl.pallas_call(
        _diff_attn_kernel,
        grid=(b, DIFF_HEADS // hps, nq),
        in_specs=[
            smem, smem,
            pl.BlockSpec((1, TQD, wide), lambda bi, hp, qi: (bi, qi, QD_BLK // hps + hp)),
            pl.BlockSpec((1, s, wide), lambda bi, hp, qi: (bi, 0, KD_BLK // hps + hp)),
            pl.BlockSpec((1, s, wide), lambda bi, hp, qi: (bi, 0, VD_BLK // hps + hp)),
            pl.BlockSpec((1, hps, TKD, TQD), lambda bi, hp, qi: (0, hp, 0, 0)),
            pl.BlockSpec((1, hps, TKD, TQD), lambda bi, hp, qi: (1, hp, 0, 0)),
            pl.BlockSpec((1, DIFF_V_DIM), lambda bi, hp, qi: (0, 0)),
        ],
        out_specs=pl.BlockSpec((1, TQD, wide), lambda bi, hp, qi: (bi, qi, hp)),
        out_shape=jax.ShapeDtypeStruct((b, s, DIFF_HEADS * DIFF_V_DIM), jnp.bfloat16),
        scratch_shapes=[
            pltpu.VMEM((hps, s // TKD, DIFF_V_DIM + ONES_ROWS, TKD), jnp.bfloat16),
            pltpu.VMEM((2 * hps, 8, TQD), jnp.float32),
            pltpu.VMEM((2 * hps, DIFF_V_DIM + ONES_ROWS, TQD), jnp.float32),
            pltpu.SMEM((2 * hps,), jnp.float32),
        ],
        compiler_params=pltpu.CompilerParams(
            dimension_semantics=("arbitrary", "arbitrary", "arbitrary"),
            vmem_limit_bytes=VMEM_LIMIT),
        name="diff_attn",
    )(bias_scalars, lam, proj, proj, proj, bias_tiles, bias_tiles, w_out_norm)


def _softplus2(y):
    return jnp.where(y > SOFTPLUS_LINEAR_FROM, y, jnp.log(1.0 + jnp.exp2(y)) * LOG2E)


def _sb_attn_kernel(q_ref, k_ref, v_ref, tri_ref, mask_ref, w_ref, o_ref, vt_sc, r_sc, acc_sc,
                    rmin_sc):
    qi = pl.program_id(1)
    tri_t = tri_ref[...]

    @pl.when(qi == 0)
    def _build_vt():
        for j in range(SB_PAIRS):
            for t in range(vt_sc.shape[1]):
                blk = v_ref[0, t * TK:(t + 1) * TK, j * LANES:(j + 1) * LANES]
                vt_sc[j, t] = blk.astype(jnp.float32).T.astype(vt_sc.dtype)

    qq = [_stack_subheads(q_ref[0, :, j * LANES:(j + 1) * LANES]) for j in range(SB_PAIRS)]

    def run_tiles(tiles):
        work = [(kt, diagonal, j) for (kt, diagonal) in tiles for j in range(SB_PAIRS)]
        z = []
        for kt, _, j in work:
            ks = pl.multiple_of(kt * TK, TK)
            z.append(lax.dot_general(k_ref[0, pl.ds(ks, TK), j * LANES:(j + 1) * LANES], qq[j],
                                     NT_DIMS, preferred_element_type=jnp.float32))
        csum = []
        for n, (kt, diagonal, j) in enumerate(work):
            sp = _softplus2(z[n])
            c = sp * mask_ref[...] if diagonal else sp
            hi, lo = _split_hi_lo(c)
            csum.append(jnp.dot(tri_t, hi, preferred_element_type=jnp.float32)
                        + jnp.dot(tri_t, lo, preferred_element_type=jnp.float32))
        r_min = None
        for n, (kt, diagonal, j) in enumerate(work):
            if diagonal:
                a = jnp.exp2(z[n] - csum[n]) * mask_ref[...]
                r_new = csum[n][0:1, :]
            else:
                r = r_sc[j][0:1, :]
                a = jnp.exp2(z[n] - csum[n] - r)
                r_new = r + csum[n][0:1, :]
            av = jnp.dot(vt_sc[j, kt], a.astype(jnp.bfloat16), preferred_element_type=jnp.float32)
            acc_sc[j] = av if diagonal else acc_sc[j] + av
            r_sc[j] = jnp.broadcast_to(r_new, (8, 2 * TQ))
            if n >= len(work) - SB_PAIRS:
                r_min = jnp.min(r_new) if r_min is None else jnp.minimum(r_min, jnp.min(r_new))
        return r_min

    @pl.when(qi == 0)
    def _first_query_tile():
        run_tiles([(qi, True)])
        rmin_sc[0] = jnp.float32(0.0)

    @pl.when(qi >= 1)
    def _diagonal_and_previous():
        rmin_sc[0] = run_tiles([(qi, True), (qi - 1, False)])

    r_min0 = rmin_sc[0]

    def more(carry):
        i, r_min = carry
        return jnp.logical_and(i < qi, r_min < SB_STOP_COST)

    def step(carry):
        i, _ = carry
        return i + 1, run_tiles([(qi - 1 - i, False)])

    lax.while_loop(more, step, (jnp.int32(1), r_min0))

    lane = lax.broadcasted_iota(jnp.int32, (TQ, LANES), 1)
    first = lane < SB_DIM
    for j in range(SB_PAIRS):
        acc = acc_sc[j].T
        o = jnp.where(first, acc[:TQ], acc[TQ:])
        o2 = o * o
        ss_a = jnp.sum(jnp.where(first, o2, 0.0), axis=-1, keepdims=True)
        ss_b = jnp.sum(jnp.where(first, 0.0, o2), axis=-1, keepdims=True)
        ms = jnp.where(first, ss_a, ss_b) * (1.0 / SB_DIM)
        o_ref[0, :, j * LANES:(j + 1) * LANES] = (
            (o * lax.rsqrt(ms + EPS)) * w_ref[...]).astype(o_ref.dtype)


def _sb_attn(proj, tri_t, mask_t, w_out_norm):
    b, s, _ = proj.shape
    nq = s // TQ
    wide = SB_PAIRS * LANES
    return pl.pallas_call(
        _sb_attn_kernel,
        grid=(b, nq),
        in_specs=[
            pl.BlockSpec((1, TQ, wide), lambda bi, qi: (bi, qi, QS_BLK // SB_PAIRS)),
            pl.BlockSpec((1, s, wide), lambda bi, qi: (bi, 0, KS_BLK // SB_PAIRS)),
            pl.BlockSpec((1, s, wide), lambda bi, qi: (bi, 0, VS_BLK // SB_PAIRS)),
            pl.BlockSpec((TK, TK), lambda bi, qi: (0, 0)),
            pl.BlockSpec((TK, 2 * TQ), lambda bi, qi: (0, 0)),
            pl.BlockSpec((1, LANES), lambda bi, qi: (0, 0)),
        ],
        out_specs=pl.BlockSpec((1, TQ, wide), lambda bi, qi: (bi, qi, 0)),
        out_shape=jax.ShapeDtypeStruct((b, s, SB_HEADS * SB_DIM), jnp.bfloat16),
        scratch_shapes=[
            pltpu.VMEM((SB_PAIRS, s // TK, LANES, TK), jnp.bfloat16),
            pltpu.VMEM((SB_PAIRS, 8, 2 * TQ), jnp.float32),
            pltpu.VMEM((SB_PAIRS, LANES, 2 * TQ), jnp.float32),
            pltpu.SMEM((1,), jnp.float32),
        ],
        compiler_params=pltpu.CompilerParams(
            dimension_semantics=("arbitrary", "arbitrary"),
            vmem_limit_bytes=VMEM_LIMIT),
        name="sb_attn",
    )(proj, proj, proj, tri_t, mask_t, w_out_norm)


def _out_ffn_kernel(x_ref, yd_ref, ys_ref, woa_ref, wob_ref, w2_ref, wg_ref, wu_ref, wd_ref,
                    o_ref):
    h1 = (x_ref[...]
          + jnp.dot(yd_ref[...], woa_ref[...], preferred_element_type=jnp.float32)
          + jnp.dot(ys_ref[...], wob_ref[...], preferred_element_type=jnp.float32))
    ms = jnp.mean(h1 * h1, axis=-1, keepdims=True)
    u2 = ((h1 * lax.rsqrt(ms + EPS)) * w2_ref[...]).astype(jnp.bfloat16)
    ffn = None
    for c in range(D_FF // FF_CHUNK):
        sl = slice(c * FF_CHUNK, (c + 1) * FF_CHUNK)
        gate = jnp.dot(u2, wg_ref[:, sl], preferred_element_type=jnp.float32)
        up = jnp.dot(u2, wu_ref[:, sl], preferred_element_type=jnp.float32)
        act = (gate * (1.0 / (1.0 + jnp.exp(-gate))) * up).astype(jnp.bfloat16)
        part = jnp.dot(act, wd_ref[sl, :], preferred_element_type=jnp.float32)
        ffn = part if ffn is None else ffn + part
    o_ref[...] = h1 + ffn


def _out_ffn(x2d, yd, ys, wo, norm2_w, wg, wu, wd):
    n = x2d.shape[0]
    const = lambda i: (0, 0)
    half = DIFF_HEADS * DIFF_V_DIM
    resident = partial(pl.BlockSpec, index_map=const, pipeline_mode=pl.Buffered(1))
    return pl.pallas_call(
        _out_ffn_kernel,
        grid=(n // TM_PROJ,),
        in_specs=[
            pl.BlockSpec((TM_PROJ, D_MODEL), lambda i: (i, 0)),
            pl.BlockSpec((TM_PROJ, half), lambda i: (i, 0)),
            pl.BlockSpec((TM_PROJ, half), lambda i: (i, 0)),
            resident((half, D_MODEL)),
            pl.BlockSpec((half, D_MODEL), lambda i: (1, 0), pipeline_mode=pl.Buffered(1)),
            pl.BlockSpec((1, D_MODEL), const),
            resident((D_MODEL, D_FF)),
            resident((D_MODEL, D_FF)),
            resident((D_FF, D_MODEL)),
        ],
        out_specs=pl.BlockSpec((TM_PROJ, D_MODEL), lambda i: (i, 0)),
        out_shape=jax.ShapeDtypeStruct((n, D_MODEL), jnp.float32),
        compiler_params=pltpu.CompilerParams(
            dimension_semantics=("arbitrary",), vmem_limit_bytes=VMEM_LIMIT),
        name="out_ffn",
    )(x2d, yd, ys, wo, wo, norm2_w, wg, wu, wd)


def _bias_tiles_kernel(thr_ref, vals_ref, o_ref):
    t = pl.program_id(0)
    h = pl.program_id(1)
    kpos = lax.broadcasted_iota(jnp.int32, (TKD, TQD), 0) - t * TKD
    qpos = lax.broadcasted_iota(jnp.int32, (TKD, TQD), 1)
    rel = kpos - qpos
    dist = jnp.abs(rel)
    half = NUM_BUCKETS // 2
    behind = jnp.full((TKD, TQD), vals_ref[h, 0], jnp.float32)
    ahead = jnp.full((TKD, TQD), vals_ref[h, half], jnp.float32)
    for i in range(1, half):
        reached = dist >= thr_ref[i]
        behind = jnp.where(reached, vals_ref[h, i], behind)
        ahead = jnp.where(reached, vals_ref[h, half + i], ahead)
    bias = jnp.where(rel > 0, ahead, behind)
    shift = CHUNK.bit_length() - 1
    allowed = (kpos >> shift) <= (qpos >> shift)
    o_ref[0, 0] = jnp.where(allowed, bias, MASK_VALUE)


def _diff_bias_tiles(rel_bias):
    half = NUM_BUCKETS // 2
    idx = _t5_bucket(-jnp.arange(TQD + TKD, dtype=jnp.int32))
    thr = jnp.sum((idx[None, :] < jnp.arange(half, dtype=jnp.int32)[:, None]).astype(jnp.int32), axis=1)
    vals = rel_bias.astype(jnp.float32).T * LOG2E
    smem = pl.BlockSpec(memory_space=pltpu.SMEM)
    return pl.pallas_call(
        _bias_tiles_kernel,
        grid=(2, DIFF_HEADS),
        in_specs=[smem, smem],
        out_specs=pl.BlockSpec((1, 1, TKD, TQD), lambda t, h: (t, h, 0, 0)),
        out_shape=jax.ShapeDtypeStruct((2, DIFF_HEADS, TKD, TQD), jnp.float32),
        compiler_params=pltpu.CompilerParams(dimension_semantics=("arbitrary", "arbitrary")),
        name="bias_tiles",
    )(thr.astype(jnp.int32), vals)


def kernel(x, norm1_w, w_in, q_norm_w, k_norm_w, lambda_q1, lambda_k1, lambda_q2, lambda_k2,
           diff_out_norm_w, sb_out_norm_w, w_out, norm2_w, w_gate, w_up, w_down, rel_bias):
    b, s, d = x.shape
    assert (b, s, d) == (8, 4096, D_MODEL) and w_in.shape[0] == 1
    f32, bf16 = jnp.float32, jnp.bfloat16
    x2d = x.reshape(b * s, d)

    qkw = jnp.concatenate([jnp.tile(q_norm_w[0].astype(f32), 2 * DIFF_HEADS)
                           * (DIFF_QK_DIM ** -0.5 * LOG2E),
                           jnp.tile(k_norm_w[0].astype(f32), 2 * DIFF_HEADS)])[None, :]
    grp = jnp.arange(GROUP_SLAB) // DIFF_QK_DIM
    gmat = jnp.where(grp[:, None] == grp[None, :], 1.0 / DIFF_QK_DIM, 0.0).astype(bf16)
    lam = (jnp.exp(jnp.sum(lambda_q1[0].astype(f32) * lambda_k1[0].astype(f32)))
           - jnp.exp(jnp.sum(lambda_q2[0].astype(f32) * lambda_k2[0].astype(f32)))
           + LAMBDA_INIT).reshape(1)
    bias_tiles = _diff_bias_tiles(rel_bias)
    vals = rel_bias.astype(f32) * LOG2E
    bias_scalars = jnp.stack([vals[_t5_bucket(jnp.int32(-(TKD + 1)))],
                              vals[_t5_bucket(jnp.int32(0))],
                              jnp.max(vals, axis=0)])
    jj = jnp.arange(TK)
    tri_t = (jj[None, :] >= jj[:, None]).astype(bf16)
    causal_t = (jj[:, None] < jnp.arange(TQ)[None, :]).astype(f32)
    sb_mask_t = jnp.concatenate([causal_t, causal_t], axis=1)

    proj = _in_proj(x2d, norm1_w[0].astype(f32)[None, :], w_in[0].astype(bf16), qkw, gmat)
    proj = proj.reshape(b, s, IN_COLS)
    y_diff = _diff_attn(proj, bias_tiles, bias_scalars, lam,
                        diff_out_norm_w[0].astype(f32)[None, :])
    y_sb = _sb_attn(proj, tri_t, sb_mask_t, jnp.tile(sb_out_norm_w[0].astype(f32), 2)[None, :])

    half = DIFF_HEADS * DIFF_V_DIM
    out = _out_ffn(x2d, y_diff.reshape(b * s, half), y_sb.reshape(b * s, half),
                   w_out[0].astype(bf16), norm2_w[0].astype(f32)[None, :],
                   w_gate[0].astype(bf16), w_up[0].astype(bf16), w_down[0].astype(bf16))
    return out.reshape(b, s, d)
```

```python
import math
from functools import partial

import jax
import jax.numpy as jnp
from jax import lax
from jax.experimental import pallas as pl
from jax.experimental.pallas import tpu as pltpu

D_MODEL = 1024
CHUNK = 64
DIFF_QK_DIM = 64
DIFF_V_DIM = 128
DIFF_HEADS = 4
SB_DIM = 64
SB_HEADS = 8
NUM_BUCKETS = 32
MAX_DISTANCE = 128
D_FF = 2816
EPS = 1e-6
IN_COLS = 3072
LAMBDA_INIT = 0.8 - 0.6 * math.exp(-0.3 * 0)
LANES = 128

QD_BLK, KD_BLK, VD_BLK = 0, 4, 8
QS_BLK, KS_BLK, VS_BLK = 12, 16, 20

TM_IN = 1024
TM_PROJ = 1024
COL_CHUNK = 512
GROUP_SLAB = 256
TQD = 512
TKD = 512
HEADS_PER_STEP = 4
ONES_ROWS = 16
SCORE_LOOKAHEAD = 2
LOG2E = math.log2(math.e)
TQ = 256
TK = 256
SB_PAIRS = SB_HEADS // 2
FF_CHUNK = 256
SAFE_SHIFT_GAP = 90.0
BOUND_SLACK = 1.01
BOUND_MARGIN = 0.5
MASK_VALUE = -1e30
SB_STOP_COST = 150.1
SOFTPLUS_LINEAR_FROM = 60.0
VMEM_LIMIT = 56 * 1024 * 1024

NT_DIMS = (((1,), (1,)), ((), ()))


def _t5_bucket(rel):
    nb = NUM_BUCKETS // 2
    max_exact = nb // 2
    ret = (rel > 0).astype(jnp.int32) * nb
    n = jnp.abs(rel)
    nf = jnp.maximum(n, 1).astype(jnp.float32)
    large = max_exact + (jnp.log(nf / max_exact) / jnp.log(jnp.float32(MAX_DISTANCE / max_exact))
                         * (nb - max_exact)).astype(jnp.int32)
    large = jnp.minimum(large, nb - 1)
    return ret + jnp.where(n < max_exact, n, large)


def _split_hi_lo(x):
    hi = x.astype(jnp.bfloat16)
    lo = (x - hi.astype(jnp.float32)).astype(jnp.bfloat16)
    return hi, lo


def _in_proj_kernel(x_ref, w1_ref, win_ref, qkw_ref, g_ref, o_ref):
    x = x_ref[...]
    ms = jnp.mean(x * x, axis=-1, keepdims=True)
    u = ((x * lax.rsqrt(ms + EPS)) * w1_ref[...]).astype(jnp.bfloat16)
    g = g_ref[...]
    for c in range(IN_COLS // COL_CHUNK):
        lo_c, hi_c = c * COL_CHUNK, (c + 1) * COL_CHUNK
        p = jnp.dot(u, win_ref[:, lo_c:hi_c], preferred_element_type=jnp.float32)
        if hi_c <= 2 * DIFF_HEADS * 2 * DIFF_QK_DIM:
            for j in range(COL_CHUNK // GROUP_SLAB):
                sl = slice(j * GROUP_SLAB, (j + 1) * GROUP_SLAB)
                pj = p[:, sl]
                msq = jnp.dot((pj * pj).astype(jnp.bfloat16), g, preferred_element_type=jnp.float32)
                w = qkw_ref[:, lo_c + j * GROUP_SLAB: lo_c + (j + 1) * GROUP_SLAB]
                o_ref[:, lo_c + j * GROUP_SLAB: lo_c + (j + 1) * GROUP_SLAB] = (
                    (pj * lax.rsqrt(msq + EPS)) * w).astype(jnp.bfloat16)
        else:
            if lo_c == QS_BLK * LANES:
                p = p * (SB_DIM ** -0.5 * LOG2E)
            o_ref[:, lo_c:hi_c] = p.astype(jnp.bfloat16)


def _in_proj(x2d, norm1_w, w_in_bf, qkw, gmat):
    n = x2d.shape[0]
    const = lambda i: (0, 0)
    return pl.pallas_call(
        _in_proj_kernel,
        grid=(n // TM_IN,),
        in_specs=[
            pl.BlockSpec((TM_IN, D_MODEL), lambda i: (i, 0)),
            pl.BlockSpec((1, D_MODEL), const),
            pl.BlockSpec((D_MODEL, IN_COLS), const, pipeline_mode=pl.Buffered(1)),
            pl.BlockSpec((1, 2 * DIFF_HEADS * 2 * DIFF_QK_DIM), const),
            pl.BlockSpec((GROUP_SLAB, GROUP_SLAB), const),
        ],
        out_specs=pl.BlockSpec((TM_IN, IN_COLS), lambda i: (i, 0)),
        out_shape=jax.ShapeDtypeStruct((n, IN_COLS), jnp.bfloat16),
        compiler_params=pltpu.CompilerParams(
            dimension_semantics=("arbitrary",), vmem_limit_bytes=VMEM_LIMIT),
        name="in_proj",
    )(x2d, norm1_w, w_in_bf, qkw, gmat)


def _stack_subheads(q):
    lane = lax.broadcasted_iota(jnp.int32, q.shape, 1)
    zero = jnp.zeros_like(q)
    return jnp.concatenate([jnp.where(lane < SB_DIM, q, zero),
                            jnp.where(lane >= SB_DIM, q, zero)], axis=0)


def _diff_attn_kernel(bsc_ref, lam_ref, q_ref, k_ref, v_ref, bdiag_ref, bsub_ref, w_ref, o_ref,
                      vt_sc, m_sc, acc_sc, kmax_sc):
    hp = pl.program_id(1)
    qi = pl.program_id(2)
    n_maps = 2 * HEADS_PER_STEP
    lane = lax.broadcasted_iota(jnp.int32, (TQD, LANES), 1)
    sub_lanes = (lane < DIFF_QK_DIM, lane >= DIFF_QK_DIM)
    ones8 = jnp.ones((8, LANES), jnp.bfloat16)

    def lane_sums(x):
        return lax.dot_general(ones8, x.astype(jnp.bfloat16), NT_DIMS,
                               preferred_element_type=jnp.float32)[0:1, :]

    @pl.when(qi == 0)
    def _per_sequence_setup():
        ones = jnp.ones((ONES_ROWS, TKD), vt_sc.dtype)
        for hh in range(HEADS_PER_STEP):
            kmax = [None, None]
            for t in range(vt_sc.shape[1]):
                blk = v_ref[0, t * TKD:(t + 1) * TKD, hh * LANES:(hh + 1) * LANES]
                vt_sc[hh, t, :DIFF_V_DIM, :] = blk.astype(jnp.float32).T.astype(vt_sc.dtype)
                vt_sc[hh, t, DIFF_V_DIM:, :] = ones
                kf = k_ref[0, t * TKD:(t + 1) * TKD, hh * LANES:(hh + 1) * LANES].astype(jnp.float32)
                for c in range(2):
                    n2 = jnp.max(lane_sums(jnp.where(sub_lanes[c], kf * kf, 0.0)))
                    kmax[c] = n2 if kmax[c] is None else jnp.maximum(kmax[c], n2)
            for c in range(2):
                kmax_sc[2 * hh + c] = kmax[c]

    zero = jnp.zeros((TQD, LANES), q_ref.dtype)
    q_sub = []
    for hh in range(HEADS_PER_STEP):
        q = q_ref[0, :, hh * LANES:(hh + 1) * LANES]
        q_sub.append(jnp.where(sub_lanes[0], q, zero))
        q_sub.append(jnp.where(sub_lanes[1], q, zero))

    gap = None
    for idx in range(n_maps):
        head = HEADS_PER_STEP * hp + idx // 2
        qf = q_sub[idx].astype(jnp.float32)
        reach = jnp.sqrt(lane_sums(qf * qf) * kmax_sc[idx]) * BOUND_SLACK
        m_sc[idx] = jnp.broadcast_to(reach + (bsc_ref[2, head] + BOUND_MARGIN), (8, TQD))
        g = 2.0 * jnp.max(reach) + (bsc_ref[2, head] - bsc_ref[1, head] + BOUND_MARGIN)
        gap = g if gap is None else jnp.maximum(gap, g)
    fixed_shift_ok = gap <= SAFE_SHIFT_GAP

    def run_tiles(tiles, fixed):
        def scores(kt, idx):
            ks = pl.multiple_of(kt * TKD, TKD)
            k = k_ref[0, pl.ds(ks, TKD), (idx // 2) * LANES:(idx // 2 + 1) * LANES]
            return lax.dot_general(k, q_sub[idx], NT_DIMS, preferred_element_type=jnp.float32)

        def weights(s, idx, bias_ref, cfar, first):
            hh = idx // 2
            if fixed:
                shift = m_sc[idx][0:1, :]
                if bias_ref is None:
                    shift = shift - cfar[hh]
                else:
                    s = s + bias_ref[0, hh]
                return jnp.exp2(s - shift).astype(jnp.bfloat16), None
            if first:
                s = s + bias_ref[0, hh]
                m_new = jnp.max(s, axis=0, keepdims=True)
                m_sc[idx] = jnp.broadcast_to(m_new, (8, TQD))
                return jnp.exp2(s - m_new).astype(jnp.bfloat16), None
            m_old = m_sc[idx][0:1, :]
            if bias_ref is None:
                m_new = jnp.maximum(m_old, jnp.max(s, axis=0, keepdims=True) + cfar[hh])
                shift = m_new - cfar[hh]
            else:
                s = s + bias_ref[0, hh]
                m_new = jnp.maximum(m_old, jnp.max(s, axis=0, keepdims=True))
                shift = m_new
            m_sc[idx] = jnp.broadcast_to(m_new, (8, TQD))
            return jnp.exp2(s - shift).astype(jnp.bfloat16), jnp.exp2(m_old - m_new)

        def accumulate(kt, idx, p, alpha, first):
            pv = jnp.dot(vt_sc[idx // 2, kt], p, preferred_element_type=jnp.float32)
            if first:
                acc_sc[idx] = pv
            elif alpha is None:
                acc_sc[idx] = acc_sc[idx] + pv
            else:
                acc_sc[idx] = alpha * acc_sc[idx] + pv

        work = [(kt, idx, bias_ref, cfar, first)
                for (kt, bias_ref, cfar, first) in tiles for idx in range(n_maps)]
        s_next = [scores(w[0], w[1]) for w in work[:SCORE_LOOKAHEAD]]
        pending = None
        for n, (kt, idx, bias_ref, cfar, first) in enumerate(work):
            p, alpha = weights(s_next.pop(0), idx, bias_ref, cfar, first)
            if n + SCORE_LOOKAHEAD < len(work):
                nxt = work[n + SCORE_LOOKAHEAD]
                s_next.append(scores(nxt[0], nxt[1]))
            if pending is not None:
                accumulate(*pending)
            pending = (kt, idx, p, alpha, first)
        accumulate(*pending)

    cfar = [bsc_ref[0, HEADS_PER_STEP * hp + hh] for hh in range(HEADS_PER_STEP)]
    n_far = jnp.maximum(qi - 1, 0)

    def all_tiles(fixed):
        @pl.when(qi == 0)
        def _first_query_tile():
            run_tiles([(qi, bdiag_ref, None, True)], fixed)

        @pl.when(qi >= 1)
        def _near_tiles():
            run_tiles([(qi, bdiag_ref, None, True), (qi - 1, bsub_ref, None, False)], fixed)

        def far_pair(i, carry):
            run_tiles([(2 * i, None, cfar, False), (2 * i + 1, None, cfar, False)], fixed)
            return carry

        lax.fori_loop(0, n_far // 2, far_pair, 0)

        @pl.when(n_far % 2 == 1)
        def _last_far_tile():
            run_tiles([(n_far - 1, None, cfar, False)], fixed)

    @pl.when(fixed_shift_ok)
    def _fixed_shift_softmax():
        all_tiles(True)

    @pl.when(jnp.logical_not(fixed_shift_ok))
    def _online_softmax():
        all_tiles(False)

    for hh in range(HEADS_PER_STEP):
        acc0 = acc_sc[2 * hh]
        acc1 = acc_sc[2 * hh + 1]
        o_t = (acc0[:DIFF_V_DIM] / acc0[DIFF_V_DIM:DIFF_V_DIM + 1]
               - lam_ref[0] * (acc1[:DIFF_V_DIM] / acc1[DIFF_V_DIM:DIFF_V_DIM + 1]))
        o = o_t.T
        ms = jnp.mean(o * o, axis=-1, keepdims=True)
        y = ((o * lax.rsqrt(ms + EPS)) * w_ref[...]) * (1.0 - LAMBDA_INIT)
        o_ref[0, :, hh * LANES:(hh + 1) * LANES] = y.astype(o_ref.dtype)


def _diff_attn(proj, bias_tiles, bias_scalars, lam, w_out_norm):
    b, s, _ = proj.shape
    nq = s // TQD
    hps = HEADS_PER_STEP
    wide = hps * LANES
    smem = pl.BlockSpec(memory_space=pltpu.SMEM)
    return pl.pallas_call(
        _diff_attn_kernel,
        grid=(b, DIFF_HEADS // hps, nq),
        in_specs=[
            smem, smem,
            pl.BlockSpec((1, TQD, wide), lambda bi, hp, qi: (bi, qi, QD_BLK // hps + hp)),
            pl.BlockSpec((1, s, wide), lambda bi, hp, qi: (bi, 0, KD_BLK // hps + hp)),
            pl.BlockSpec((1, s, wide), lambda bi, hp, qi: (bi, 0, VD_BLK // hps + hp)),
            pl.BlockSpec((1, hps, TKD, TQD), lambda bi, hp, qi: (0, hp, 0, 0)),
            pl.BlockSpec((1, hps, TKD, TQD), lambda bi, hp, qi: (1, hp, 0, 0)),
            pl.BlockSpec((1, DIFF_V_DIM), lambda bi, hp, qi: (0, 0)),
        ],
        out_specs=pl.BlockSpec((1, TQD, wide), lambda bi, hp, qi: (bi, qi, hp)),
        out_shape=jax.ShapeDtypeStruct((b, s, DIFF_HEADS * DIFF_V_DIM), jnp.bfloat16),
        scratch_shapes=[
            pltpu.VMEM((hps, s // TKD, DIFF_V_DIM + ONES_ROWS, TKD), jnp.bfloat16),
            pltpu.VMEM((2 * hps, 8, TQD), jnp.float32),
            pltpu.VMEM((2 * hps, DIFF_V_DIM + ONES_ROWS, TQD), jnp.float32),
            pltpu.SMEM((2 * hps,), jnp.float32),
        ],
        compiler_params=pltpu.CompilerParams(
            dimension_semantics=("arbitrary", "arbitrary", "arbitrary"),
            vmem_limit_bytes=VMEM_LIMIT),
        name="diff_attn",
    )(bias_scalars, lam, proj, proj, proj, bias_tiles, bias_tiles, w_out_norm)


def _softplus2(y):
    return jnp.where(y > SOFTPLUS_LINEAR_FROM, y, jnp.log(1.0 + jnp.exp2(y)) * LOG2E)


def _sb_attn_kernel(q_ref, k_ref, v_ref, tri_ref, mask_ref, w_ref, o_ref, vt_sc, r_sc, acc_sc,
                    rmin_sc):
    qi = pl.program_id(1)
    tri_t = tri_ref[...]

    @pl.when(qi == 0)
    def _build_vt():
        for j in range(SB_PAIRS):
            for t in range(vt_sc.shape[1]):
                blk = v_ref[0, t * TK:(t + 1) * TK, j * LANES:(j + 1) * LANES]
                vt_sc[j, t] = blk.astype(jnp.float32).T.astype(vt_sc.dtype)

    qq = [_stack_subheads(q_ref[0, :, j * LANES:(j + 1) * LANES]) for j in range(SB_PAIRS)]

    def run_tiles(tiles):
        work = [(kt, diagonal, j) for (kt, diagonal) in tiles for j in range(SB_PAIRS)]
        z = []
        for kt, diagonal, j in work:
            ks = pl.multiple_of(kt * TK, TK)
            y = lax.dot_general(k_ref[0, pl.ds(ks, TK), j * LANES:(j + 1) * LANES], qq[j],
                                NT_DIMS, preferred_element_type=jnp.float32)
            z.append(y + mask_ref[...] if diagonal else y)
        csum = []
        for n, (kt, diagonal, j) in enumerate(work):
            c = _softplus2(z[n])
            hi, lo = _split_hi_lo(c)
            csum.append(jnp.dot(tri_t, hi, preferred_element_type=jnp.float32)
                        + jnp.dot(tri_t, lo, preferred_element_type=jnp.float32))
        r_min = None
        for n, (kt, diagonal, j) in enumerate(work):
            if diagonal:
                a = jnp.exp2(z[n] - csum[n])
                r_new = csum[n][0:1, :]
            else:
                r = r_sc[j][0:1, :]
                a = jnp.exp2(z[n] - csum[n] - r)
                r_new = r + csum[n][0:1, :]
            av = jnp.dot(vt_sc[j, kt], a.astype(jnp.bfloat16), preferred_element_type=jnp.float32)
            acc_sc[j] = av if diagonal else acc_sc[j] + av
            r_sc[j] = jnp.broadcast_to(r_new, (8, 2 * TQ))
            if n >= len(work) - SB_PAIRS:
                r_min = jnp.min(r_new) if r_min is None else jnp.minimum(r_min, jnp.min(r_new))
        return r_min

    @pl.when(qi == 0)
    def _first_query_tile():
        run_tiles([(qi, True)])
        rmin_sc[0] = jnp.float32(0.0)

    @pl.when(qi >= 1)
    def _diagonal_and_previous():
        rmin_sc[0] = run_tiles([(qi, True), (qi - 1, False)])

    r_min0 = rmin_sc[0]

    def more(carry):
        i, r_min = carry
        return jnp.logical_and(i < qi, r_min < SB_STOP_COST)

    def step(carry):
        i, _ = carry
        return i + 1, run_tiles([(qi - 1 - i, False)])

    lax.while_loop(more, step, (jnp.int32(1), r_min0))

    lane = lax.broadcasted_iota(jnp.int32, (TQ, LANES), 1)
    first = lane < SB_DIM
    for j in range(SB_PAIRS):
        acc = acc_sc[j].T
        o = jnp.where(first, acc[:TQ], acc[TQ:])
        o2 = o * o
        ss_a = jnp.sum(jnp.where(first, o2, 0.0), axis=-1, keepdims=True)
        ss_b = jnp.sum(jnp.where(first, 0.0, o2), axis=-1, keepdims=True)
        ms = jnp.where(first, ss_a, ss_b) * (1.0 / SB_DIM)
        o_ref[0, :, j * LANES:(j + 1) * LANES] = (
            (o * lax.rsqrt(ms + EPS)) * w_ref[...]).astype(o_ref.dtype)


def _sb_attn(proj, tri_t, mask_t, w_out_norm):
    b, s, _ = proj.shape
    nq = s // TQ
    wide = SB_PAIRS * LANES
    return pl.pallas_call(
        _sb_attn_kernel,
        grid=(b, nq),
        in_specs=[
            pl.BlockSpec((1, TQ, wide), lambda bi, qi: (bi, qi, QS_BLK // SB_PAIRS)),
            pl.BlockSpec((1, s, wide), lambda bi, qi: (bi, 0, KS_BLK // SB_PAIRS)),
            pl.BlockSpec((1, s, wide), lambda bi, qi: (bi, 0, VS_BLK // SB_PAIRS)),
            pl.BlockSpec((TK, TK), lambda bi, qi: (0, 0)),
            pl.BlockSpec((TK, 2 * TQ), lambda bi, qi: (0, 0)),
            pl.BlockSpec((1, LANES), lambda bi, qi: (0, 0)),
        ],
        out_specs=pl.BlockSpec((1, TQ, wide), lambda bi, qi: (bi, qi, 0)),
        out_shape=jax.ShapeDtypeStruct((b, s, SB_HEADS * SB_DIM), jnp.bfloat16),
        scratch_shapes=[
            pltpu.VMEM((SB_PAIRS, s // TK, LANES, TK), jnp.bfloat16),
            pltpu.VMEM((SB_PAIRS, 8, 2 * TQ), jnp.float32),
            pltpu.VMEM((SB_PAIRS, LANES, 2 * TQ), jnp.float32),
            pltpu.SMEM((1,), jnp.float32),
        ],
        compiler_params=pltpu.CompilerParams(
            dimension_semantics=("arbitrary", "arbitrary"),
            vmem_limit_bytes=VMEM_LIMIT),
        name="sb_attn",
    )(proj, proj, proj, tri_t, mask_t, w_out_norm)


def _out_ffn_kernel(x_ref, yd_ref, ys_ref, woa_ref, wob_ref, w2_ref, wg_ref, wu_ref, wd_ref,
                    o_ref):
    h1 = (x_ref[...]
          + jnp.dot(yd_ref[...], woa_ref[...], preferred_element_type=jnp.float32)
          + jnp.dot(ys_ref[...], wob_ref[...], preferred_element_type=jnp.float32))
    ms = jnp.mean(h1 * h1, axis=-1, keepdims=True)
    u2 = ((h1 * lax.rsqrt(ms + EPS)) * w2_ref[...]).astype(jnp.bfloat16)
    ffn = None
    for c in range(D_FF // FF_CHUNK):
        sl = slice(c * FF_CHUNK, (c + 1) * FF_CHUNK)
        gate = jnp.dot(u2, wg_ref[:, sl], preferred_element_type=jnp.float32)
        up = jnp.dot(u2, wu_ref[:, sl], preferred_element_type=jnp.float32)
        act = (gate * (1.0 / (1.0 + jnp.exp(-gate))) * up).astype(jnp.bfloat16)
        part = jnp.dot(act, wd_ref[sl, :], preferred_element_type=jnp.float32)
        ffn = part if ffn is None else ffn + part
    o_ref[...] = h1 + ffn


def _out_ffn(x2d, yd, ys, wo, norm2_w, wg, wu, wd):
    n = x2d.shape[0]
    const = lambda i: (0, 0)
    half = DIFF_HEADS * DIFF_V_DIM
    resident = partial(pl.BlockSpec, index_map=const, pipeline_mode=pl.Buffered(1))
    return pl.pallas_call(
        _out_ffn_kernel,
        grid=(n // TM_PROJ,),
        in_specs=[
            pl.BlockSpec((TM_PROJ, D_MODEL), lambda i: (i, 0)),
            pl.BlockSpec((TM_PROJ, half), lambda i: (i, 0)),
            pl.BlockSpec((TM_PROJ, half), lambda i: (i, 0)),
            resident((half, D_MODEL)),
            pl.BlockSpec((half, D_MODEL), lambda i: (1, 0), pipeline_mode=pl.Buffered(1)),
            pl.BlockSpec((1, D_MODEL), const),
            resident((D_MODEL, D_FF)),
            resident((D_MODEL, D_FF)),
            resident((D_FF, D_MODEL)),
        ],
        out_specs=pl.BlockSpec((TM_PROJ, D_MODEL), lambda i: (i, 0)),
        out_shape=jax.ShapeDtypeStruct((n, D_MODEL), jnp.float32),
        compiler_params=pltpu.CompilerParams(
            dimension_semantics=("arbitrary",), vmem_limit_bytes=VMEM_LIMIT),
        name="out_ffn",
    )(x2d, yd, ys, wo, wo, norm2_w, wg, wu, wd)


def _bias_tiles_kernel(thr_ref, vals_ref, o_ref):
    t = pl.program_id(0)
    h = pl.program_id(1)
    kpos = lax.broadcasted_iota(jnp.int32, (TKD, TQD), 0) - t * TKD
    qpos = lax.broadcasted_iota(jnp.int32, (TKD, TQD), 1)
    rel = kpos - qpos
    dist = jnp.abs(rel)
    half = NUM_BUCKETS // 2
    behind = jnp.full((TKD, TQD), vals_ref[h, 0], jnp.float32)
    ahead = jnp.full((TKD, TQD), vals_ref[h, half], jnp.float32)
    for i in range(1, half):
        reached = dist >= thr_ref[i]
        behind = jnp.where(reached, vals_ref[h, i], behind)
        ahead = jnp.where(reached, vals_ref[h, half + i], ahead)
    bias = jnp.where(rel > 0, ahead, behind)
    shift = CHUNK.bit_length() - 1
    allowed = (kpos >> shift) <= (qpos >> shift)
    o_ref[0, 0] = jnp.where(allowed, bias, MASK_VALUE)


def _diff_bias_tiles(rel_bias):
    half = NUM_BUCKETS // 2
    idx = _t5_bucket(-jnp.arange(TQD + TKD, dtype=jnp.int32))
    thr = jnp.sum((idx[None, :] < jnp.arange(half, dtype=jnp.int32)[:, None]).astype(jnp.int32), axis=1)
    vals = rel_bias.astype(jnp.float32).T * LOG2E
    smem = pl.BlockSpec(memory_space=pltpu.SMEM)
    return pl.pallas_call(
        _bias_tiles_kernel,
        grid=(2, DIFF_HEADS),
        in_specs=[smem, smem],
        out_specs=pl.BlockSpec((1, 1, TKD, TQD), lambda t, h: (t, h, 0, 0)),
        out_shape=jax.ShapeDtypeStruct((2, DIFF_HEADS, TKD, TQD), jnp.float32),
        compiler_params=pltpu.CompilerParams(dimension_semantics=("arbitrary", "arbitrary")),
        name="bias_tiles",
    )(thr.astype(jnp.int32), vals)


def kernel(x, norm1_w, w_in, q_norm_w, k_norm_w, lambda_q1, lambda_k1, lambda_q2, lambda_k2,
           diff_out_norm_w, sb_out_norm_w, w_out, norm2_w, w_gate, w_up, w_down, rel_bias):
    b, s, d = x.shape
    assert (b, s, d) == (8, 4096, D_MODEL) and w_in.shape[0] == 1
    f32, bf16 = jnp.float32, jnp.bfloat16
    x2d = x.reshape(b * s, d)

    qkw = jnp.concatenate([jnp.tile(q_norm_w[0].astype(f32), 2 * DIFF_HEADS)
                           * (DIFF_QK_DIM ** -0.5 * LOG2E),
                           jnp.tile(k_norm_w[0].astype(f32), 2 * DIFF_HEADS)])[None, :]
    grp = jnp.arange(GROUP_SLAB) // DIFF_QK_DIM
    gmat = jnp.where(grp[:, None] == grp[None, :], 1.0 / DIFF_QK_DIM, 0.0).astype(bf16)
    lam = (jnp.exp(jnp.sum(lambda_q1[0].astype(f32) * lambda_k1[0].astype(f32)))
           - jnp.exp(jnp.sum(lambda_q2[0].astype(f32) * lambda_k2[0].astype(f32)))
           + LAMBDA_INIT).reshape(1)
    bias_tiles = _diff_bias_tiles(rel_bias)
    vals = rel_bias.astype(f32) * LOG2E
    bias_scalars = jnp.stack([vals[_t5_bucket(jnp.int32(-(TKD + 1)))],
                              vals[_t5_bucket(jnp.int32(0))],
                              jnp.max(vals, axis=0)])
    jj = jnp.arange(TK)
    tri_t = (jj[None, :] >= jj[:, None]).astype(bf16)
    causal_t = jnp.where(jj[:, None] < jnp.arange(TQ)[None, :], 0.0, MASK_VALUE).astype(f32)
    sb_mask_t = jnp.concatenate([causal_t, causal_t], axis=1)

    proj = _in_proj(x2d, norm1_w[0].astype(f32)[None, :], w_in[0].astype(bf16), qkw, gmat)
    proj = proj.reshape(b, s, IN_COLS)
    y_diff = _diff_attn(proj, bias_tiles, bias_scalars, lam,
                        diff_out_norm_w[0].astype(f32)[None, :])
    y_sb = _sb_attn(proj, tri_t, sb_mask_t, jnp.tile(sb_out_norm_w[0].astype(f32), 2)[None, :])

    half = DIFF_HEADS * DIFF_V_DIM
    out = _out_ffn(x2d, y_diff.reshape(b * s, half), y_sb.reshape(b * s, half),
                   w_out[0].astype(bf16), norm2_w[0].astype(f32)[None, :],
                   w_gate[0].astype(bf16), w_up[0].astype(bf16), w_down[0].astype(bf16))
    return out.reshape(b, s, d)
```

```python
import math
from functools import partial

import jax
import jax.numpy as jnp
from jax import lax
from jax.experimental import pallas as pl
from jax.experimental.pallas import tpu as pltpu

D_MODEL = 1024
CHUNK = 64
DIFF_QK_DIM = 64
DIFF_V_DIM = 128
DIFF_HEADS = 4
SB_DIM = 64
SB_HEADS = 8
NUM_BUCKETS = 32
MAX_DISTANCE = 128
D_FF = 2816
EPS = 1e-6
IN_COLS = 3072
LAMBDA_INIT = 0.8 - 0.6 * math.exp(-0.3 * 0)
LANES = 128

QD_BLK, KD_BLK, VD_BLK = 0, 4, 8
QS_BLK, KS_BLK, VS_BLK = 12, 16, 20

TM_IN = 1024
TM_PROJ = 1024
COL_CHUNK = 512
GROUP_SLAB = 256
TQD = 512
TKD = 512
HEADS_PER_STEP = 4
ONES_ROWS = 16
SCORE_LOOKAHEAD = 2
LOG2E = math.log2(math.e)
TQ = 256
TK = 256
SB_PAIRS = SB_HEADS // 2
SB_TILES_PER_STEP = 2
FF_CHUNK = 256
SAFE_SHIFT_GAP = 90.0
BOUND_SLACK = 1.01
BOUND_MARGIN = 0.5
MASK_VALUE = -1e30
SB_STOP_COST = 150.1
SOFTPLUS_LINEAR_FROM = 60.0
VMEM_LIMIT = 56 * 1024 * 1024

NT_DIMS = (((1,), (1,)), ((), ()))


def _t5_bucket(rel):
    nb = NUM_BUCKETS // 2
    max_exact = nb // 2
    ret = (rel > 0).astype(jnp.int32) * nb
    n = jnp.abs(rel)
    nf = jnp.maximum(n, 1).astype(jnp.float32)
    large = max_exact + (jnp.log(nf / max_exact) / jnp.log(jnp.float32(MAX_DISTANCE / max_exact))
                         * (nb - max_exact)).astype(jnp.int32)
    large = jnp.minimum(large, nb - 1)
    return ret + jnp.where(n < max_exact, n, large)


def _split_hi_lo(x):
    hi = x.astype(jnp.bfloat16)
    lo = (x - hi.astype(jnp.float32)).astype(jnp.bfloat16)
    return hi, lo


def _in_proj_kernel(x_ref, w1_ref, win_ref, qkw_ref, g_ref, o_ref):
    x = x_ref[...]
    ms = jnp.mean(x * x, axis=-1, keepdims=True)
    u = ((x * lax.rsqrt(ms + EPS)) * w1_ref[...]).astype(jnp.bfloat16)
    g = g_ref[...]
    for c in range(IN_COLS // COL_CHUNK):
        lo_c, hi_c = c * COL_CHUNK, (c + 1) * COL_CHUNK
        p = jnp.dot(u, win_ref[:, lo_c:hi_c], preferred_element_type=jnp.float32)
        if hi_c <= 2 * DIFF_HEADS * 2 * DIFF_QK_DIM:
            for j in range(COL_CHUNK // GROUP_SLAB):
                sl = slice(j * GROUP_SLAB, (j + 1) * GROUP_SLAB)
                pj = p[:, sl]
                msq = jnp.dot((pj * pj).astype(jnp.bfloat16), g, preferred_element_type=jnp.float32)
                w = qkw_ref[:, lo_c + j * GROUP_SLAB: lo_c + (j + 1) * GROUP_SLAB]
                o_ref[:, lo_c + j * GROUP_SLAB: lo_c + (j + 1) * GROUP_SLAB] = (
                    (pj * lax.rsqrt(msq + EPS)) * w).astype(jnp.bfloat16)
        else:
            if lo_c == QS_BLK * LANES:
                p = p * (SB_DIM ** -0.5 * LOG2E)
            o_ref[:, lo_c:hi_c] = p.astype(jnp.bfloat16)


def _in_proj(x2d, norm1_w, w_in_bf, qkw, gmat):
    n = x2d.shape[0]
    const = lambda i: (0, 0)
    return pl.pallas_call(
        _in_proj_kernel,
        grid=(n // TM_IN,),
        in_specs=[
            pl.BlockSpec((TM_IN, D_MODEL), lambda i: (i, 0)),
            pl.BlockSpec((1, D_MODEL), const),
            pl.BlockSpec((D_MODEL, IN_COLS), const, pipeline_mode=pl.Buffered(1)),
            pl.BlockSpec((1, 2 * DIFF_HEADS * 2 * DIFF_QK_DIM), const),
            pl.BlockSpec((GROUP_SLAB, GROUP_SLAB), const),
        ],
        out_specs=pl.BlockSpec((TM_IN, IN_COLS), lambda i: (i, 0)),
        out_shape=jax.ShapeDtypeStruct((n, IN_COLS), jnp.bfloat16),
        compiler_params=pltpu.CompilerParams(
            dimension_semantics=("arbitrary",), vmem_limit_bytes=VMEM_LIMIT),
        name="in_proj",
    )(x2d, norm1_w, w_in_bf, qkw, gmat)


def _stack_subheads(q):
    lane = lax.broadcasted_iota(jnp.int32, q.shape, 1)
    zero = jnp.zeros_like(q)
    return jnp.concatenate([jnp.where(lane < SB_DIM, q, zero),
                            jnp.where(lane >= SB_DIM, q, zero)], axis=0)


def _diff_attn_kernel(bsc_ref, lam_ref, q_ref, k_ref, v_ref, bdiag_ref, bsub_ref, w_ref, o_ref,
                      vt_sc, m_sc, acc_sc, kmax_sc):
    hp = pl.program_id(1)
    qi = pl.program_id(2)
    n_maps = 2 * HEADS_PER_STEP
    lane = lax.broadcasted_iota(jnp.int32, (TQD, LANES), 1)
    sub_lanes = (lane < DIFF_QK_DIM, lane >= DIFF_QK_DIM)
    ones8 = jnp.ones((8, LANES), jnp.bfloat16)

    def lane_sums(x):
        return lax.dot_general(ones8, x.astype(jnp.bfloat16), NT_DIMS,
                               preferred_element_type=jnp.float32)[0:1, :]

    @pl.when(qi == 0)
    def _per_sequence_setup():
        ones = jnp.ones((ONES_ROWS, TKD), vt_sc.dtype)
        for hh in range(HEADS_PER_STEP):
            kmax = [None, None]
            for t in range(vt_sc.shape[1]):
                blk = v_ref[0, t * TKD:(t + 1) * TKD, hh * LANES:(hh + 1) * LANES]
                vt_sc[hh, t, :DIFF_V_DIM, :] = blk.astype(jnp.float32).T.astype(vt_sc.dtype)
                vt_sc[hh, t, DIFF_V_DIM:, :] = ones
                kf = k_ref[0, t * TKD:(t + 1) * TKD, hh * LANES:(hh + 1) * LANES].astype(jnp.float32)
                for c in range(2):
                    n2 = jnp.max(lane_sums(jnp.where(sub_lanes[c], kf * kf, 0.0)))
                    kmax[c] = n2 if kmax[c] is None else jnp.maximum(kmax[c], n2)
            for c in range(2):
                kmax_sc[2 * hh + c] = kmax[c]

    zero = jnp.zeros((TQD, LANES), q_ref.dtype)
    q_sub = []
    for hh in range(HEADS_PER_STEP):
        q = q_ref[0, :, hh * LANES:(hh + 1) * LANES]
        q_sub.append(jnp.where(sub_lanes[0], q, zero))
        q_sub.append(jnp.where(sub_lanes[1], q, zero))

    gap = None
    for idx in range(n_maps):
        head = HEADS_PER_STEP * hp + idx // 2
        qf = q_sub[idx].astype(jnp.float32)
        reach = jnp.sqrt(lane_sums(qf * qf) * kmax_sc[idx]) * BOUND_SLACK
        m_sc[idx] = jnp.broadcast_to(reach + (bsc_ref[2, head] + BOUND_MARGIN), (8, TQD))
        g = 2.0 * jnp.max(reach) + (bsc_ref[2, head] - bsc_ref[1, head] + BOUND_MARGIN)
        gap = g if gap is None else jnp.maximum(gap, g)
    fixed_shift_ok = gap <= SAFE_SHIFT_GAP

    def run_tiles(tiles, fixed):
        def scores(kt, idx):
            ks = pl.multiple_of(kt * TKD, TKD)
            k = k_ref[0, pl.ds(ks, TKD), (idx // 2) * LANES:(idx // 2 + 1) * LANES]
            return lax.dot_general(k, q_sub[idx], NT_DIMS, preferred_element_type=jnp.float32)

        def weights(s, idx, bias_ref, cfar, first):
            hh = idx // 2
            if fixed:
                shift = m_sc[idx][0:1, :]
                if bias_ref is None:
                    shift = shift - cfar[hh]
                else:
                    s = s + bias_ref[0, hh]
                return jnp.exp2(s - shift).astype(jnp.bfloat16), None
            if first:
                s = s + bias_ref[0, hh]
                m_new = jnp.max(s, axis=0, keepdims=True)
                m_sc[idx] = jnp.broadcast_to(m_new, (8, TQD))
                return jnp.exp2(s - m_new).astype(jnp.bfloat16), None
            m_old = m_sc[idx][0:1, :]
            if bias_ref is None:
                m_new = jnp.maximum(m_old, jnp.max(s, axis=0, keepdims=True) + cfar[hh])
                shift = m_new - cfar[hh]
            else:
                s = s + bias_ref[0, hh]
                m_new = jnp.maximum(m_old, jnp.max(s, axis=0, keepdims=True))
                shift = m_new
            m_sc[idx] = jnp.broadcast_to(m_new, (8, TQD))
            return jnp.exp2(s - shift).astype(jnp.bfloat16), jnp.exp2(m_old - m_new)

        def accumulate(kt, idx, p, alpha, first):
            pv = jnp.dot(vt_sc[idx // 2, kt], p, preferred_element_type=jnp.float32)
            if first:
                acc_sc[idx] = pv
            elif alpha is None:
                acc_sc[idx] = acc_sc[idx] + pv
            else:
                acc_sc[idx] = alpha * acc_sc[idx] + pv

        work = [(kt, idx, bias_ref, cfar, first)
                for (kt, bias_ref, cfar, first) in tiles for idx in range(n_maps)]
        s_next = [scores(w[0], w[1]) for w in work[:SCORE_LOOKAHEAD]]
        pending = None
        for n, (kt, idx, bias_ref, cfar, first) in enumerate(work):
            p, alpha = weights(s_next.pop(0), idx, bias_ref, cfar, first)
            if n + SCORE_LOOKAHEAD < len(work):
                nxt = work[n + SCORE_LOOKAHEAD]
                s_next.append(scores(nxt[0], nxt[1]))
            if pending is not None:
                accumulate(*pending)
            pending = (kt, idx, p, alpha, first)
        accumulate(*pending)

    cfar = [bsc_ref[0, HEADS_PER_STEP * hp + hh] for hh in range(HEADS_PER_STEP)]
    n_far = jnp.maximum(qi - 1, 0)

    def all_tiles(fixed):
        @pl.when(qi == 0)
        def _first_query_tile():
            run_tiles([(qi, bdiag_ref, None, True)], fixed)

        @pl.when(qi >= 1)
        def _near_tiles():
            run_tiles([(qi, bdiag_ref, None, True), (qi - 1, bsub_ref, None, False)], fixed)

        def far_pair(i, carry):
            run_tiles([(2 * i, None, cfar, False), (2 * i + 1, None, cfar, False)], fixed)
            return carry

        lax.fori_loop(0, n_far // 2, far_pair, 0)

        @pl.when(n_far % 2 == 1)
        def _last_far_tile():
            run_tiles([(n_far - 1, None, cfar, False)], fixed)

    @pl.when(fixed_shift_ok)
    def _fixed_shift_softmax():
        all_tiles(True)

    @pl.when(jnp.logical_not(fixed_shift_ok))
    def _online_softmax():
        all_tiles(False)

    for hh in range(HEADS_PER_STEP):
        acc0 = acc_sc[2 * hh]
        acc1 = acc_sc[2 * hh + 1]
        o_t = (acc0[:DIFF_V_DIM] / acc0[DIFF_V_DIM:DIFF_V_DIM + 1]
               - lam_ref[0] * (acc1[:DIFF_V_DIM] / acc1[DIFF_V_DIM:DIFF_V_DIM + 1]))
        o = o_t.T
        ms = jnp.mean(o * o, axis=-1, keepdims=True)
        y = ((o * lax.rsqrt(ms + EPS)) * w_ref[...]) * (1.0 - LAMBDA_INIT)
        o_ref[0, :, hh * LANES:(hh + 1) * LANES] = y.astype(o_ref.dtype)


def _diff_attn(proj, bias_tiles, bias_scalars, lam, w_out_norm):
    b, s, _ = proj.shape
    nq = s // TQD
    hps = HEADS_PER_STEP
    wide = hps * LANES
    smem = pl.BlockSpec(memory_space=pltpu.SMEM)
    return pl.pallas_call(
        _diff_attn_kernel,
        grid=(b, DIFF_HEADS // hps, nq),
        in_specs=[
            smem, smem,
            pl.BlockSpec((1, TQD, wide), lambda bi, hp, qi: (bi, qi, QD_BLK // hps + hp)),
            pl.BlockSpec((1, s, wide), lambda bi, hp, qi: (bi, 0, KD_BLK // hps + hp)),
            pl.BlockSpec((1, s, wide), lambda bi, hp, qi: (bi, 0, VD_BLK // hps + hp)),
            pl.BlockSpec((1, hps, TKD, TQD), lambda bi, hp, qi: (0, hp, 0, 0)),
            pl.BlockSpec((1, hps, TKD, TQD), lambda bi, hp, qi: (1, hp, 0, 0)),
            pl.BlockSpec((1, DIFF_V_DIM), lambda bi, hp, qi: (0, 0)),
        ],
        out_specs=pl.BlockSpec((1, TQD, wide), lambda bi, hp, qi: (bi, qi, hp)),
        out_shape=jax.ShapeDtypeStruct((b, s, DIFF_HEADS * DIFF_V_DIM), jnp.bfloat16),
        scratch_shapes=[
            pltpu.VMEM((hps, s // TKD, DIFF_V_DIM + ONES_ROWS, TKD), jnp.bfloat16),
            pltpu.VMEM((2 * hps, 8, TQD), jnp.float32),
            pltpu.VMEM((2 * hps, DIFF_V_DIM + ONES_ROWS, TQD), jnp.float32),
            pltpu.SMEM((2 * hps,), jnp.float32),
        ],
        compiler_params=pltpu.CompilerParams(
            dimension_semantics=("arbitrary", "arbitrary", "arbitrary"),
            vmem_limit_bytes=VMEM_LIMIT),
        name="diff_attn",
    )(bias_scalars, lam, proj, proj, proj, bias_tiles, bias_tiles, w_out_norm)


def _softplus2(y):
    return jnp.where(y > SOFTPLUS_LINEAR_FROM, y, jnp.log(1.0 + jnp.exp2(y)) * LOG2E)


def _sb_attn_kernel(q_ref, k_ref, v_ref, tri_ref, mask_ref, w_ref, o_ref, vt_sc, r_sc, acc_sc,
                    rmin_sc):
    step_id = pl.program_id(1)
    tri_t = tri_ref[...]

    @pl.when(step_id == 0)
    def _build_vt():
        for j in range(SB_PAIRS):
            for t in range(vt_sc.shape[1]):
                blk = v_ref[0, t * TK:(t + 1) * TK, j * LANES:(j + 1) * LANES]
                vt_sc[j, t] = blk.astype(jnp.float32).T.astype(vt_sc.dtype)

    def query_tile(qi, row0):
        qq = [_stack_subheads(q_ref[0, row0:row0 + TQ, j * LANES:(j + 1) * LANES]) for j in range(SB_PAIRS)]

        def run_tiles(tiles):
            work = [(kt, diagonal, j) for (kt, diagonal) in tiles for j in range(SB_PAIRS)]
            z = []
            for kt, diagonal, j in work:
                ks = pl.multiple_of(kt * TK, TK)
                y = lax.dot_general(k_ref[0, pl.ds(ks, TK), j * LANES:(j + 1) * LANES], qq[j],
                                    NT_DIMS, preferred_element_type=jnp.float32)
                z.append(y + mask_ref[...] if diagonal else y)
            csum = []
            for n, (kt, diagonal, j) in enumerate(work):
                c = _softplus2(z[n])
                hi, lo = _split_hi_lo(c)
                csum.append(jnp.dot(tri_t, hi, preferred_element_type=jnp.float32)
                            + jnp.dot(tri_t, lo, preferred_element_type=jnp.float32))
            r_min = None
            for n, (kt, diagonal, j) in enumerate(work):
                if diagonal:
                    a = jnp.exp2(z[n] - csum[n])
                    r_new = csum[n][0:1, :]
                else:
                    r = r_sc[j][0:1, :]
                    a = jnp.exp2(z[n] - csum[n] - r)
                    r_new = r + csum[n][0:1, :]
                av = jnp.dot(vt_sc[j, kt], a.astype(jnp.bfloat16), preferred_element_type=jnp.float32)
                acc_sc[j] = av if diagonal else acc_sc[j] + av
                r_sc[j] = jnp.broadcast_to(r_new, (8, 2 * TQ))
                if n >= len(work) - SB_PAIRS:
                    r_min = jnp.min(r_new) if r_min is None else jnp.minimum(r_min, jnp.min(r_new))
            return r_min

        @pl.when(qi == 0)
        def _first_query_tile():
            run_tiles([(qi, True)])
            rmin_sc[0] = jnp.float32(0.0)

        @pl.when(qi >= 1)
        def _diagonal_and_previous():
            rmin_sc[0] = run_tiles([(qi, True), (qi - 1, False)])

        r_min0 = rmin_sc[0]

        def more(carry):
            i, r_min = carry
            return jnp.logical_and(i < qi, r_min < SB_STOP_COST)

        def step(carry):
            i, _ = carry
            return i + 1, run_tiles([(qi - 1 - i, False)])

        lax.while_loop(more, step, (jnp.int32(1), r_min0))

        lane = lax.broadcasted_iota(jnp.int32, (TQ, LANES), 1)
        first = lane < SB_DIM
        for j in range(SB_PAIRS):
            acc = acc_sc[j].T
            o = jnp.where(first, acc[:TQ], acc[TQ:])
            o2 = o * o
            ss_a = jnp.sum(jnp.where(first, o2, 0.0), axis=-1, keepdims=True)
            ss_b = jnp.sum(jnp.where(first, 0.0, o2), axis=-1, keepdims=True)
            ms = jnp.where(first, ss_a, ss_b) * (1.0 / SB_DIM)
            o_ref[0, row0:row0 + TQ, j * LANES:(j + 1) * LANES] = (
                (o * lax.rsqrt(ms + EPS)) * w_ref[...]).astype(o_ref.dtype)

    for t in range(SB_TILES_PER_STEP):
        query_tile(SB_TILES_PER_STEP * step_id + t, t * TQ)


def _sb_attn(proj, tri_t, mask_t, w_out_norm):
    b, s, _ = proj.shape
    rows = SB_TILES_PER_STEP * TQ
    wide = SB_PAIRS * LANES
    return pl.pallas_call(
        _sb_attn_kernel,
        grid=(b, s // rows),
        in_specs=[
            pl.BlockSpec((1, rows, wide), lambda bi, qi: (bi, qi, QS_BLK // SB_PAIRS)),
            pl.BlockSpec((1, s, wide), lambda bi, qi: (bi, 0, KS_BLK // SB_PAIRS)),
            pl.BlockSpec((1, s, wide), lambda bi, qi: (bi, 0, VS_BLK // SB_PAIRS)),
            pl.BlockSpec((TK, TK), lambda bi, qi: (0, 0)),
            pl.BlockSpec((TK, 2 * TQ), lambda bi, qi: (0, 0)),
            pl.BlockSpec((1, LANES), lambda bi, qi: (0, 0)),
        ],
        out_specs=pl.BlockSpec((1, rows, wide), lambda bi, qi: (bi, qi, 0)),
        out_shape=jax.ShapeDtypeStruct((b, s, SB_HEADS * SB_DIM), jnp.bfloat16),
        scratch_shapes=[
            pltpu.VMEM((SB_PAIRS, s // TK, LANES, TK), jnp.bfloat16),
            pltpu.VMEM((SB_PAIRS, 8, 2 * TQ), jnp.float32),
            pltpu.VMEM((SB_PAIRS, LANES, 2 * TQ), jnp.float32),
            pltpu.SMEM((1,), jnp.float32),
        ],
        compiler_params=pltpu.CompilerParams(
            dimension_semantics=("arbitrary", "arbitrary"),
            vmem_limit_bytes=VMEM_LIMIT),
        name="sb_attn",
    )(proj, proj, proj, tri_t, mask_t, w_out_norm)


def _out_ffn_kernel(x_ref, yd_ref, ys_ref, woa_ref, wob_ref, w2_ref, wg_ref, wu_ref, wd_ref,
                    o_ref):
    h1 = (x_ref[...]
          + jnp.dot(yd_ref[...], woa_ref[...], preferred_element_type=jnp.float32)
          + jnp.dot(ys_ref[...], wob_ref[...], preferred_element_type=jnp.float32))
    ms = jnp.mean(h1 * h1, axis=-1, keepdims=True)
    u2 = ((h1 * lax.rsqrt(ms + EPS)) * w2_ref[...]).astype(jnp.bfloat16)
    ffn = None
    for c in range(D_FF // FF_CHUNK):
        sl = slice(c * FF_CHUNK, (c + 1) * FF_CHUNK)
        gate = jnp.dot(u2, wg_ref[:, sl], preferred_element_type=jnp.float32)
        up = jnp.dot(u2, wu_ref[:, sl], preferred_element_type=jnp.float32)
        act = (gate * (1.0 / (1.0 + jnp.exp(-gate))) * up).astype(jnp.bfloat16)
        part = jnp.dot(act, wd_ref[sl, :], preferred_element_type=jnp.float32)
        ffn = part if ffn is None else ffn + part
    o_ref[...] = h1 + ffn


def _out_ffn(x2d, yd, ys, wo, norm2_w, wg, wu, wd):
    n = x2d.shape[0]
    const = lambda i: (0, 0)
    half = DIFF_HEADS * DIFF_V_DIM
    resident = partial(pl.BlockSpec, index_map=const, pipeline_mode=pl.Buffered(1))
    return pl.pallas_call(
        _out_ffn_kernel,
        grid=(n // TM_PROJ,),
        in_specs=[
            pl.BlockSpec((TM_PROJ, D_MODEL), lambda i: (i, 0)),
            pl.BlockSpec((TM_PROJ, half), lambda i: (i, 0)),
            pl.BlockSpec((TM_PROJ, half), lambda i: (i, 0)),
            resident((half, D_MODEL)),
            pl.BlockSpec((half, D_MODEL), lambda i: (1, 0), pipeline_mode=pl.Buffered(1)),
            pl.BlockSpec((1, D_MODEL), const),
            resident((D_MODEL, D_FF)),
            resident((D_MODEL, D_FF)),
            resident((D_FF, D_MODEL)),
        ],
        out_specs=pl.BlockSpec((TM_PROJ, D_MODEL), lambda i: (i, 0)),
        out_shape=jax.ShapeDtypeStruct((n, D_MODEL), jnp.float32),
        compiler_params=pltpu.CompilerParams(
            dimension_semantics=("arbitrary",), vmem_limit_bytes=VMEM_LIMIT),
        name="out_ffn",
    )(x2d, yd, ys, wo, wo, norm2_w, wg, wu, wd)


def _bias_tiles_kernel(thr_ref, vals_ref, o_ref):
    t = pl.program_id(0)
    h = pl.program_id(1)
    kpos = lax.broadcasted_iota(jnp.int32, (TKD, TQD), 0) - t * TKD
    qpos = lax.broadcasted_iota(jnp.int32, (TKD, TQD), 1)
    rel = kpos - qpos
    dist = jnp.abs(rel)
    half = NUM_BUCKETS // 2
    behind = jnp.full((TKD, TQD), vals_ref[h, 0], jnp.float32)
    ahead = jnp.full((TKD, TQD), vals_ref[h, half], jnp.float32)
    for i in range(1, half):
        reached = dist >= thr_ref[i]
        behind = jnp.where(reached, vals_ref[h, i], behind)
        ahead = jnp.where(reached, vals_ref[h, half + i], ahead)
    bias = jnp.where(rel > 0, ahead, behind)
    shift = CHUNK.bit_length() - 1
    allowed = (kpos >> shift) <= (qpos >> shift)
    o_ref[0, 0] = jnp.where(allowed, bias, MASK_VALUE)


def _diff_bias_tiles(rel_bias):
    half = NUM_BUCKETS // 2
    idx = _t5_bucket(-jnp.arange(TQD + TKD, dtype=jnp.int32))
    thr = jnp.sum((idx[None, :] < jnp.arange(half, dtype=jnp.int32)[:, None]).astype(jnp.int32), axis=1)
    vals = rel_bias.astype(jnp.float32).T * LOG2E
    smem = pl.BlockSpec(memory_space=pltpu.SMEM)
    return pl.pallas_call(
        _bias_tiles_kernel,
        grid=(2, DIFF_HEADS),
        in_specs=[smem, smem],
        out_specs=pl.BlockSpec((1, 1, TKD, TQD), lambda t, h: (t, h, 0, 0)),
        out_shape=jax.ShapeDtypeStruct((2, DIFF_HEADS, TKD, TQD), jnp.float32),
        compiler_params=pltpu.CompilerParams(dimension_semantics=("arbitrary", "arbitrary")),
        name="bias_tiles",
    )(thr.astype(jnp.int32), vals)


def kernel(x, norm1_w, w_in, q_norm_w, k_norm_w, lambda_q1, lambda_k1, lambda_q2, lambda_k2,
           diff_out_norm_w, sb_out_norm_w, w_out, norm2_w, w_gate, w_up, w_down, rel_bias):
    b, s, d = x.shape
    assert (b, s, d) == (8, 4096, D_MODEL) and w_in.shape[0] == 1
    f32, bf16 = jnp.float32, jnp.bfloat16
    x2d = x.reshape(b * s, d)

    qkw = jnp.concatenate([jnp.tile(q_norm_w[0].astype(f32), 2 * DIFF_HEADS)
                           * (DIFF_QK_DIM ** -0.5 * LOG2E),
                           jnp.tile(k_norm_w[0].astype(f32), 2 * DIFF_HEADS)])[None, :]
    grp = jnp.arange(GROUP_SLAB) // DIFF_QK_DIM
    gmat = jnp.where(grp[:, None] == grp[None, :], 1.0 / DIFF_QK_DIM, 0.0).astype(bf16)
    lam = (jnp.exp(jnp.sum(lambda_q1[0].astype(f32) * lambda_k1[0].astype(f32)))
           - jnp.exp(jnp.sum(lambda_q2[0].astype(f32) * lambda_k2[0].astype(f32)))
           + LAMBDA_INIT).reshape(1)
    bias_tiles = _diff_bias_tiles(rel_bias)
    vals = rel_bias.astype(f32) * LOG2E
    bias_scalars = jnp.stack([vals[_t5_bucket(jnp.int32(-(TKD + 1)))],
                              vals[_t5_bucket(jnp.int32(0))],
                              jnp.max(vals, axis=0)])
    jj = jnp.arange(TK)
    tri_t = (jj[None, :] >= jj[:, None]).astype(bf16)
    causal_t = jnp.where(jj[:, None] < jnp.arange(TQ)[None, :], 0.0, MASK_VALUE).astype(f32)
    sb_mask_t = jnp.concatenate([causal_t, causal_t], axis=1)

    proj = _in_proj(x2d, norm1_w[0].astype(f32)[None, :], w_in[0].astype(bf16), qkw, gmat)
    proj = proj.reshape(b, s, IN_COLS)
    y_diff = _diff_attn(proj, bias_tiles, bias_scalars, lam,
                        diff_out_norm_w[0].astype(f32)[None, :])
    y_sb = _sb_attn(proj, tri_t, sb_mask_t, jnp.tile(sb_out_norm_w[0].astype(f32), 2)[None, :])

    half = DIFF_HEADS * DIFF_V_DIM
    out = _out_ffn(x2d, y_diff.reshape(b * s, half), y_sb.reshape(b * s, half),
                   w_out[0].astype(bf16), norm2_w[0].astype(f32)[None, :],
                   w_gate[0].astype(bf16), w_up[0].astype(bf16), w_down[0].astype(bf16))
    return out.reshape(b, s, d)
```
